```python
import jax
import jax.numpy as jnp
from jax import lax
import numpy as np

D_MODEL = 2048
BATCH = 2
SEQ = 8192
DEPTH = 2

GRID_W = 64
CTX_LEN = 256
HEAD_DIM = 128
N_Q_HEADS = 8
N_KV_HEADS = 2
Q_PER_KV = N_Q_HEADS // N_KV_HEADS
Q_DIM = N_Q_HEADS * HEAD_DIM
KV_DIM = N_KV_HEADS * HEAD_DIM
WINDOW = 128
ATTN_BLOCK = 128
ATTN_SCALE = HEAD_DIM ** -0.5
ROPE_BASE = 10000.0
ROPE_PAIRS = HEAD_DIM // 4
CONV_DIM = D_MODEL // 2
SHORT_CONV_W = 3
EVEN_IN_DIM = Q_DIM + 2 * KV_DIM + 3 * CONV_DIM
EVEN_SPLITS = (Q_DIM, Q_DIM + KV_DIM, Q_DIM + 2 * KV_DIM, Q_DIM + 2 * KV_DIM + CONV_DIM,
               Q_DIM + 2 * KV_DIM + 2 * CONV_DIM)
D_FF = 5632
D_RNN = D_MODEL
N_RNN_HEADS = 16
RNN_HEAD_DIM = D_RNN // N_RNN_HEADS
N_DIRS = 2
RG_CONV_W = 4
RG_CONV_PAD_LEFT = 2
RG_C = 8.0
N_EXPERTS = 8
TOP_K = 2
D_FF_EXPERT = 7168
MOE_BLOCK = 128
NORM_EPS = 1e-6
NEG_INF = -1e30

kernel_name = 'hybrid_swa_shortconv_rglru_moe_prefix_dit'


def rmsnorm(h, g):
    h32 = h.astype(jnp.float32)
    y = h32 * lax.rsqrt(jnp.mean(h32 * h32, axis=-1, keepdims=True) + NORM_EPS)
    return y.astype(h.dtype) * g


def modulate(u, shift, scale):
    return u * (1 + scale) + shift


def adaln(cvec, w_mod, b_mod, n_chunks):
    width = n_chunks * D_MODEL
    m = jax.nn.silu(cvec) @ w_mod[:, :width] + b_mod[:width]
    return jnp.split(m[..., None, :], n_chunks, axis=-1)


def depthwise_conv(u, w, b, pad_left):
    width = w.shape[0]
    t = u.shape[1]
    up = jnp.pad(u, ((0, 0), (pad_left, width - 1 - pad_left), (0, 0)))
    y = b
    for tap in range(width):
        y = y + w[tap] * up[:, tap:tap + t]
    return y


def swiglu(u, w_gate, w_up, w_down):
    return (jax.nn.silu(u @ w_gate) * (u @ w_up)) @ w_down


def axial_rope_tables(n_tokens):
    rows = n_tokens // GRID_W
    row_id = jnp.broadcast_to(jnp.arange(rows)[:, None], (rows, GRID_W)).reshape(-1).astype(jnp.float32)
    col_id = jnp.broadcast_to(jnp.arange(GRID_W)[None, :], (rows, GRID_W)).reshape(-1).astype(jnp.float32)
    inv_freq = ROPE_BASE ** (-jnp.arange(ROPE_PAIRS, dtype=jnp.float32) / ROPE_PAIRS)
    ang_r = row_id[:, None] * inv_freq
    ang_c = col_id[:, None] * inv_freq
    ang = jnp.concatenate([ang_r, ang_r, ang_c, ang_c], axis=-1)
    return jnp.cos(ang), jnp.sin(ang)


def apply_rope(t, cos, sin):
    q = ROPE_PAIRS
    shape = (1, t.shape[1]) + (1,) * (t.ndim - 3) + (HEAD_DIM,)
    cos = cos.reshape(shape).astype(t.dtype)
    sin = sin.reshape(shape).astype(t.dtype)
    rot = jnp.concatenate([-t[..., q:2 * q], t[..., :q], -t[..., 3 * q:], t[..., 2 * q:3 * q]], axis=-1)
    return t * cos + rot * sin


def context_attention(q, k, v, sinks):
    bn, t = q.shape[:2]
    s = jnp.einsum('bqhgd,bkhd->bhgqk', q, k).astype(jnp.float32) * ATTN_SCALE
    sink = jnp.broadcast_to(sinks.astype(jnp.float32).reshape(1, N_KV_HEADS, Q_PER_KV, 1, 1), s.shape[:-1] + (1,))
    p = jax.nn.softmax(jnp.concatenate([sink, s], axis=-1), axis=-1)[..., 1:].astype(v.dtype)
    o = jnp.einsum('bhgqk,bkhd->bqhgd', p, v)
    return o.reshape(bn, t, Q_DIM)


def banded_blocks(t, n_blocks):
    bn = t.shape[0]
    tp = jnp.pad(t, ((0, 0), (ATTN_BLOCK, ATTN_BLOCK), (0, 0), (0, 0)))
    tp = tp.reshape(bn, n_blocks + 2, ATTN_BLOCK, N_KV_HEADS, HEAD_DIM)
    return jnp.concatenate([tp[:, :-2], tp[:, 1:-1], tp[:, 2:]], axis=2)


def windowed_attention(q, k, v, k_ctx, v_ctx, sinks):
    bn, s_len = q.shape[:2]
    n_ctx = k_ctx.shape[1]
    nb = s_len // ATTN_BLOCK
    qb = q.reshape(bn, nb, ATTN_BLOCK, N_KV_HEADS, Q_PER_KV, HEAD_DIM)
    kw = banded_blocks(k, nb)
    vw = banded_blocks(v, nb)
    s_win = jnp.einsum('bnqhgd,bnkhd->bnhgqk', qb, kw).astype(jnp.float32) * ATTN_SCALE
    s_ctx = jnp.einsum('bnqhgd,bchd->bnhgqc', qb, k_ctx).astype(jnp.float32) * ATTN_SCALE
    qi = jnp.arange(ATTN_BLOCK)
    kj = jnp.arange(3 * ATTN_BLOCK)
    rel = kj[None, :] - ATTN_BLOCK - qi[:, None]
    key_pos = jnp.arange(nb)[:, None] * ATTN_BLOCK - ATTN_BLOCK + kj[None, :]
    mask = (jnp.abs(rel) <= WINDOW)[None] & ((key_pos >= 0) & (key_pos < s_len))[:, None, :]
    s_win = jnp.where(mask[None, :, None, None], s_win, NEG_INF)
    sink = jnp.broadcast_to(sinks.astype(jnp.float32).reshape(1, 1, N_KV_HEADS, Q_PER_KV, 1, 1),
                            s_win.shape[:-1] + (1,))
    p = jax.nn.softmax(jnp.concatenate([sink, s_ctx, s_win], axis=-1), axis=-1).astype(v.dtype)
    p_ctx = p[..., 1:1 + n_ctx]
    p_win = p[..., 1 + n_ctx:]
    o = (jnp.einsum('bnhgqc,bchd->bnqhgd', p_ctx, v_ctx)
         + jnp.einsum('bnhgqk,bnkhd->bnqhgd', p_win, vw))
    return o.reshape(bn, s_len, Q_DIM)


def rglru_coeffs(xc, ga_w, ga_b, gx_w, gx_b, lam):
    bn, t, _ = xc.shape
    xh = xc.reshape(bn, t, N_RNN_HEADS, RNN_HEAD_DIM)

    def block_diag(w, b):
        y = jnp.einsum('bthi,zhij->zbthj', xh, w).reshape(N_DIRS, bn, t, D_RNN)
        return y.astype(jnp.float32) + b.astype(jnp.float32)[:, None, None, :]

    r = jax.nn.sigmoid(block_diag(ga_w, ga_b))
    i = jax.nn.sigmoid(block_diag(gx_w, gx_b))
    log_a = -RG_C * r * jax.nn.softplus(-lam.astype(jnp.float32))[:, None, None, :]
    a = jnp.exp(log_a)
    b = jnp.sqrt(-jnp.expm1(2.0 * log_a)) * i * xc.astype(jnp.float32)[None]
    return a, b


def _affine_combine(left, right):
    a_l, b_l = left
    a_r, b_r = right
    return a_l * a_r, a_r * b_l + b_r


def linear_scan(a, b, reverse):
    return lax.associative_scan(_affine_combine, (a, b), axis=1, reverse=reverse)


def moe_swiglu(xf, router_w, router_b, w_gate, w_up, w_down):
    n_tok, d = xf.shape
    logits = (xf @ router_w).astype(jnp.float32) + router_b.astype(jnp.float32)
    top_logits, top_idx = lax.top_k(logits, TOP_K)
    top_w = jax.nn.softmax(top_logits, axis=-1)
    n_assign = n_tok * TOP_K
    flat_e = top_idx.reshape(-1)
    flat_tok = jnp.arange(n_assign, dtype=jnp.int32) // TOP_K
    flat_w = top_w.reshape(-1)
    order = jnp.argsort(flat_e)
    sorted_e = flat_e[order]
    counts = jnp.zeros((N_EXPERTS,), jnp.int32).at[flat_e].add(1)
    padded = (counts + MOE_BLOCK - 1) // MOE_BLOCK * MOE_BLOCK
    pad_end = jnp.cumsum(padded)
    pad_start = pad_end - padded
    grp_start = jnp.cumsum(counts) - counts
    rank = jnp.arange(n_assign, dtype=jnp.int32) - grp_start[sorted_e]
    dest = pad_start[sorted_e] + rank
    n_blocks = -(-n_assign // MOE_BLOCK) + N_EXPERTS
    cap = n_blocks * MOE_BLOCK
    slot_tok = jnp.full((cap,), n_tok, jnp.int32).at[dest].set(flat_tok[order])
    slot_w = jnp.zeros((cap,), jnp.float32).at[dest].set(flat_w[order])
    block_e = jnp.minimum(jnp.searchsorted(pad_end, jnp.arange(n_blocks) * MOE_BLOCK, side='right'),
                          N_EXPERTS - 1)
    x_pad = jnp.concatenate([xf, jnp.zeros((1, d), xf.dtype)], axis=0)
    xb = x_pad[slot_tok].reshape(n_blocks, MOE_BLOCK, d)

    def expert_block(args):
        xblk, e = args
        return (jax.nn.silu(xblk @ w_gate[e]) * (xblk @ w_up[e])) @ w_down[e]

    yb = lax.map(expert_block, (xb, block_e)).reshape(cap, d)
    y = jax.ops.segment_sum(yb * slot_w[:, None].astype(yb.dtype), slot_tok, num_segments=n_tok + 1)
    return y[:n_tok]


def even_layer(h_c, h_l, c, c_ctx, w_mod, b_mod, norm1_g, w_in, sinks, conv_w, conv_b, w_out,
               norm2_g, ffn_w_gate, ffn_w_up, ffn_w_down, rope_cos, rope_sin):
    sh1_l, sc1_l, g1_l, sh2_l, sc2_l, g2_l = adaln(c, w_mod, b_mod, 6)
    sh1_c, sc1_c, g1_c, sh2_c, sc2_c, g2_c = adaln(c_ctx, w_mod, b_mod, 6)

    def mixer_inputs(h, shift, scale):
        z = modulate(rmsnorm(h, norm1_g), shift, scale) @ w_in
        bn, t, _ = z.shape
        q, k, v, xin, b_gate, c_gate = jnp.split(z, EVEN_SPLITS, axis=-1)
        q = q.reshape(bn, t, N_KV_HEADS, Q_PER_KV, HEAD_DIM)
        k = k.reshape(bn, t, N_KV_HEADS, HEAD_DIM)
        v = v.reshape(bn, t, N_KV_HEADS, HEAD_DIM)
        conv_out = b_gate * depthwise_conv(c_gate * xin, conv_w, conv_b, (SHORT_CONV_W - 1) // 2)
        return q, k, v, conv_out

    q_c, k_c, v_c, conv_c = mixer_inputs(h_c, sh1_c, sc1_c)
    q_l, k_l, v_l, conv_l = mixer_inputs(h_l, sh1_l, sc1_l)
    q_l = apply_rope(q_l, rope_cos, rope_sin)
    k_l = apply_rope(k_l, rope_cos, rope_sin)
    attn_c = context_attention(q_c, k_c, v_c, sinks)
    attn_l = windowed_attention(q_l, k_l, v_l, k_c, v_c, sinks)
    h_c = h_c + g1_c * (jnp.concatenate([attn_c, conv_c], axis=-1) @ w_out)
    h_l = h_l + g1_l * (jnp.concatenate([attn_l, conv_l], axis=-1) @ w_out)
    h_c = h_c + g2_c * swiglu(modulate(rmsnorm(h_c, norm2_g), sh2_c, sc2_c), ffn_w_gate, ffn_w_up, ffn_w_down)
    h_l = h_l + g2_l * swiglu(modulate(rmsnorm(h_l, norm2_g), sh2_l, sc2_l), ffn_w_gate, ffn_w_up, ffn_w_down)
    return h_c, h_l


def odd_layer(h_c, h_l, c, c_ctx, w_mod, b_mod, norm1_g, w_in, conv_w, conv_b, ga_w, ga_b, gx_w, gx_b,
              lam, w_out, norm2_g, router_w, router_b, moe_w_gate, moe_w_up, moe_w_down):
    sh1_l, sc1_l, g1_l, sh2_l, sc2_l, g2_l = adaln(c, w_mod, b_mod, 6)
    sh1_c, sc1_c = adaln(c_ctx, w_mod, b_mod, 2)
    xr_c = modulate(rmsnorm(h_c, norm1_g), sh1_c, sc1_c) @ w_in[:, D_RNN:]
    a_c, b_c = rglru_coeffs(depthwise_conv(xr_c, conv_w, conv_b, RG_CONV_PAD_LEFT), ga_w, ga_b, gx_w, gx_b, lam)
    h0_f = linear_scan(a_c[0], b_c[0], False)[1][:, -1]
    h0_b = linear_scan(a_c[1], b_c[1], True)[1][:, 0]
    z_l = modulate(rmsnorm(h_l, norm1_g), sh1_l, sc1_l) @ w_in
    gate_l, xr_l = jnp.split(z_l, [D_RNN], axis=-1)
    a_l, b_l = rglru_coeffs(depthwise_conv(xr_l, conv_w, conv_b, RG_CONV_PAD_LEFT), ga_w, ga_b, gx_w, gx_b, lam)
    cum_f, hz_f = linear_scan(a_l[0], b_l[0], False)
    cum_b, hz_b = linear_scan(a_l[1], b_l[1], True)
    rec = cum_f * h0_f[:, None] + hz_f + cum_b * h0_b[:, None] + hz_b
    y = (rec.astype(h_l.dtype) * jax.nn.gelu(gate_l)) @ w_out
    h_l = h_l + g1_l * y
    u = modulate(rmsnorm(h_l, norm2_g), sh2_l, sc2_l)
    bn, t, d = u.shape
    moe_out = moe_swiglu(u.reshape(bn * t, d), router_w, router_b, moe_w_gate, moe_w_up, moe_w_down)
    return h_l + g2_l * moe_out.reshape(bn, t, d)


def setup_inputs(seed: int = 0) -> dict:
    key = jax.random.key(seed)
    ks = iter(jax.random.split(key, 40))
    d = D_MODEL

    def nrm(shape, scale):
        return jax.random.normal(next(ks), shape, jnp.float32) * scale

    def gain():
        return 1.0 + nrm((d,), 0.02)

    u = jax.random.uniform(next(ks), (N_DIRS, D_RNN), jnp.float32, minval=0.9, maxval=0.999)
    a0 = u ** (1.0 / RG_C)
    lam = jnp.log(a0) - jnp.log1p(-a0)
    return {
        'x': nrm((BATCH, SEQ, d), 1.0),
        'c': nrm((BATCH, d), 1.0),
        'ctx': nrm((BATCH, CTX_LEN, d), 1.0),
        'c_ctx': nrm((d,), 1.0),
        'l0_w_mod': nrm((d, 6 * d), 0.5 * d ** -0.5),
        'l0_b_mod': nrm((6 * d,), 0.01),
        'l0_norm1_g': gain(),
        'l0_w_in': nrm((d, EVEN_IN_DIM), d ** -0.5),
        'l0_sinks': nrm((N_Q_HEADS,), 0.5),
        'l0_conv_w': nrm((SHORT_CONV_W, CONV_DIM), SHORT_CONV_W ** -0.5),
        'l0_conv_b': nrm((CONV_DIM,), 0.01),
        'l0_w_out': nrm((Q_DIM + CONV_DIM, d), (Q_DIM + CONV_DIM) ** -0.5),
        'l0_norm2_g': gain(),
        'l0_ffn_w_gate': nrm((d, D_FF), d ** -0.5),
        'l0_ffn_w_up': nrm((d, D_FF), d ** -0.5),
        'l0_ffn_w_down': nrm((D_FF, d), D_FF ** -0.5),
        'l1_w_mod': nrm((d, 6 * d), 0.5 * d ** -0.5),
        'l1_b_mod': nrm((6 * d,), 0.01),
        'l1_norm1_g': gain(),
        'l1_w_in': nrm((d, 2 * D_RNN), d ** -0.5),
        'l1_conv_w': nrm((RG_CONV_W, D_RNN), RG_CONV_W ** -0.5),
        'l1_conv_b': nrm((D_RNN,), 0.01),
        'l1_gate_a_w': nrm((N_DIRS, N_RNN_HEADS, RNN_HEAD_DIM, RNN_HEAD_DIM), RNN_HEAD_DIM ** -0.5),
        'l1_gate_a_b': nrm((N_DIRS, D_RNN), 0.01),
        'l1_gate_x_w': nrm((N_DIRS, N_RNN_HEADS, RNN_HEAD_DIM, RNN_HEAD_DIM), RNN_HEAD_DIM ** -0.5),
        'l1_gate_x_b': nrm((N_DIRS, D_RNN), 0.01),
        'l1_lambda': lam,
        'l1_w_out': nrm((D_RNN, d), D_RNN ** -0.5),
        'l1_norm2_g': gain(),
        'l1_router_w': nrm((d, N_EXPERTS), d ** -0.5),
        'l1_router_b': nrm((N_EXPERTS,), 0.01),
        'l1_moe_w_gate': nrm((N_EXPERTS, d, D_FF_EXPERT), d ** -0.5),
        'l1_moe_w_up': nrm((N_EXPERTS, d, D_FF_EXPERT), d ** -0.5),
        'l1_moe_w_down': nrm((N_EXPERTS, D_FF_EXPERT, d), D_FF_EXPERT ** -0.5),
        'final_norm_g': gain(),
    }


def reference(x, c, ctx, c_ctx,
              l0_w_mod, l0_b_mod, l0_norm1_g, l0_w_in, l0_sinks, l0_conv_w, l0_conv_b, l0_w_out,
              l0_norm2_g, l0_ffn_w_gate, l0_ffn_w_up, l0_ffn_w_down,
              l1_w_mod, l1_b_mod, l1_norm1_g, l1_w_in, l1_conv_w, l1_conv_b, l1_gate_a_w, l1_gate_a_b,
              l1_gate_x_w, l1_gate_x_b, l1_lambda, l1_w_out, l1_norm2_g, l1_router_w, l1_router_b,
              l1_moe_w_gate, l1_moe_w_up, l1_moe_w_down, final_norm_g):
    rope_cos, rope_sin = axial_rope_tables(x.shape[1])
    layer_params = [
        (l0_w_mod, l0_b_mod, l0_norm1_g, l0_w_in, l0_sinks, l0_conv_w, l0_conv_b, l0_w_out,
         l0_norm2_g, l0_ffn_w_gate, l0_ffn_w_up, l0_ffn_w_down),
        (l1_w_mod, l1_b_mod, l1_norm1_g, l1_w_in, l1_conv_w, l1_conv_b, l1_gate_a_w, l1_gate_a_b,
         l1_gate_x_w, l1_gate_x_b, l1_lambda, l1_w_out, l1_norm2_g, l1_router_w, l1_router_b,
         l1_moe_w_gate, l1_moe_w_up, l1_moe_w_down),
    ]
    h_c, h_l = ctx, x
    for layer in range(DEPTH):
        if layer % 2 == 0:
            h_c, h_l = even_layer(h_c, h_l, c, c_ctx, *layer_params[layer], rope_cos, rope_sin)
        else:
            h_l = odd_layer(h_c, h_l, c, c_ctx, *layer_params[layer])
    return rmsnorm(h_l, final_norm_g)
```

```python
import functools

import jax
import jax.numpy as jnp
from jax import lax
from jax.experimental import pallas as pl
from jax.experimental.pallas import tpu as pltpu

F32 = jnp.float32
BF16 = jnp.bfloat16

GRID_W = 64
HEAD_DIM = 128
N_Q_HEADS = 8
N_KV_HEADS = 2
Q_PER_KV = N_Q_HEADS // N_KV_HEADS
WINDOW = 128
ATTN_SCALE = HEAD_DIM ** -0.5
ROPE_BASE = 10000.0
ROPE_PAIRS = HEAD_DIM // 4
SHORT_CONV_OFFSETS = (-1, 0, 1)
RG_CONV_OFFSETS = (-2, -1, 0, 1)
N_RNN_HEADS = 16
RG_C = 8.0
N_EXPERTS = 8
TOP_K = 2
NORM_EPS = 1e-6
NEG_INF = -1e30
N_MOD = 6

LANES = 128
TM = 512
TR = 256
HALO = 16
TN_OUT = 1024
TF_FFN = 512
TF_MOE = 1024
CONV_CW = 512
MOD_TN = 1024
VMEM_CAP = 56 * 1024 * 1024


def _vmem_limit(nbytes):
    return int(min(max(nbytes * 5 // 4 + (4 << 20), 32 << 20), VMEM_CAP))


def _norm_mod(h, g, shift, scale):
    ms = jnp.mean(h * h, axis=-1, keepdims=True)
    y = h * lax.rsqrt(ms + NORM_EPS)
    return (y * g) * (1.0 + scale) + shift


def _adaln_kernel(c_ref, w_ref, b_ref, o_ref):
    c = c_ref[...]
    s = (c * jax.nn.sigmoid(c)).astype(BF16)
    o_ref[...] = jnp.dot(s, w_ref[...].astype(BF16), preferred_element_type=F32) + b_ref[...]


def _adaln(cvecs, w_mod, b_mod):
    d, n = w_mod.shape
    return pl.pallas_call(
        _adaln_kernel,
        out_shape=jax.ShapeDtypeStruct((8, n), F32),
        grid=(n // MOD_TN,),
        in_specs=[pl.BlockSpec((8, d), lambda j: (0, 0)),
                  pl.BlockSpec((d, MOD_TN), lambda j: (0, j)),
                  pl.BlockSpec((1, MOD_TN), lambda j: (0, j))],
        out_specs=pl.BlockSpec((8, MOD_TN), lambda j: (0, j)),
        compiler_params=pltpu.CompilerParams(
            dimension_semantics=("arbitrary",),
            vmem_limit_bytes=_vmem_limit(2 * d * MOD_TN * 4 + d * MOD_TN * 2)),
        name="adaln",
    )(cvecs, w_mod, b_mod.reshape(1, n))


def _mod_spec(chunk, width, tile_to_mod, col_from_j):
    if col_from_j:
        return pl.BlockSpec((None, 1, width), lambda i, j: (tile_to_mod(i) * N_MOD + chunk, 0, j))
    return pl.BlockSpec((None, 1, width), lambda i, j: (tile_to_mod(i) * N_MOD + chunk, 0, 0))


def _l0_in_kernel(h_ref, g_ref, sh_ref, sc_ref, w_ref, cos_ref, sa_ref, sb_ref, o_ref, u_scr):
    j = pl.program_id(1)

    @pl.when(j == 0)
    def _():
        u_scr[...] = _norm_mod(h_ref[...], g_ref[...], sh_ref[...], sc_ref[...]).astype(BF16)

    z = jnp.dot(u_scr[...], w_ref[...], preferred_element_type=F32)

    @pl.when(j == 0)
    def _():
        cos, sa, sb = cos_ref[...], sa_ref[...], sb_ref[...]
        n_rot = N_Q_HEADS + N_KV_HEADS
        for hh in range(n_rot):
            t = z[:, hh * HEAD_DIM:(hh + 1) * HEAD_DIM]
            r = (t * cos + pltpu.roll(t, HEAD_DIM - ROPE_PAIRS, 1) * sa
                 + pltpu.roll(t, ROPE_PAIRS, 1) * sb)
            if hh < N_Q_HEADS:
                r = r * ATTN_SCALE
            o_ref[:, hh * HEAD_DIM:(hh + 1) * HEAD_DIM] = r.astype(BF16)
        o_ref[:, n_rot * HEAD_DIM:] = z[:, n_rot * HEAD_DIM:].astype(BF16)

    @pl.when(j != 0)
    def _():
        o_ref[...] = z.astype(BF16)


def _l0_in_proj(h, g, mods, w_in, rope, tile_to_mod, rope_blk):
    r, d = h.shape
    n = w_in.shape[1]
    tn = (N_Q_HEADS + 2 * N_KV_HEADS) * HEAD_DIM
    assert n % tn == 0
    cos, sa, sb = rope
    rope_spec = pl.BlockSpec((TM, HEAD_DIM), lambda i, j: (rope_blk(i), 0))
    return pl.pallas_call(
        _l0_in_kernel,
        out_shape=jax.ShapeDtypeStruct((r, n), BF16),
        grid=(r // TM, n // tn),
        in_specs=[pl.BlockSpec((TM, d), lambda i, j: (i, 0)),
                  pl.BlockSpec((1, d), lambda i, j: (0, 0)),
                  _mod_spec(0, d, tile_to_mod, False),
                  _mod_spec(1, d, tile_to_mod, False),
                  pl.BlockSpec((d, tn), lambda i, j: (0, j)),
                  rope_spec, rope_spec, rope_spec],
        out_specs=pl.BlockSpec((TM, tn), lambda i, j: (i, j)),
        scratch_shapes=[pltpu.VMEM((TM, d), BF16)],
        compiler_params=pltpu.CompilerParams(
            dimension_semantics=("arbitrary", "arbitrary"),
            vmem_limit_bytes=_vmem_limit(2 * TM * d * 4 + 2 * d * tn * 2 + 2 * TM * tn * 2
                                         + TM * d * 2 + 2 * TM * tn * 4)),
        name="l0_in_proj",
    )(h, g.reshape(1, d), mods, mods, w_in, cos, sa, sb)


def _dwconv_kernel(*refs, offsets, gated, n_ctx_tiles, tiles_per_seg):
    if gated:
        x_ref, bg_ref, cg_ref, xp_ref, cgp_ref, xn_ref, cgn_ref, w_ref, b_ref, o_ref = refs
    else:
        x_ref, xp_ref, xn_ref, w_ref, b_ref, o_ref = refs
    i = pl.program_id(0)
    li = i - n_ctx_tiles
    is_ctx = i < n_ctx_tiles
    first = is_ctx | (li % tiles_per_seg == 0)
    last = is_ctx | (li % tiles_per_seg == tiles_per_seg - 1)
    x = x_ref[...].astype(F32)
    xp = xp_ref[...].astype(F32)
    xn = xn_ref[...].astype(F32)
    if gated:
        x = x * cg_ref[...].astype(F32)
        xp = xp * cgp_ref[...].astype(F32)
        xn = xn * cgn_ref[...].astype(F32)
    xp = jnp.where(first, 0.0, xp)
    xn = jnp.where(last, 0.0, xn)
    tr, cw = x.shape
    row8 = lax.broadcasted_iota(jnp.int32, (8, cw), 0)
    acc = jnp.broadcast_to(b_ref[...], (tr, cw))
    for k, off in enumerate(offsets):
        wk = w_ref[k:k + 1, :]
        if off == 0:
            y = x
        elif off < 0:
            s = -off
            r = pltpu.roll(x, s, 0)
            rp = pltpu.roll(xp, s, 0)[0:8]
            head = jnp.where(row8 < s, rp, r[0:8])
            y = jnp.concatenate([head, r[8:]], axis=0)
        else:
            r = pltpu.roll(x, tr - off, 0)
            rn = pltpu.roll(xn, HALO - off, 0)[HALO - 8:HALO]
            tail = jnp.where(row8 >= 8 - off, rn, r[tr - 8:])
            y = jnp.concatenate([r[:tr - 8], tail], axis=0)
        acc = acc + wk * y
    if gated:
        acc = acc * bg_ref[...].astype(F32)
    o_ref[...] = acc.astype(o_ref.dtype)


def _dwconv(x, w, b, offsets, n_ctx_tiles, tiles_per_seg, gated_cols, out_dtype):
    r = x.shape[0]
    c = w.shape[1]
    nh = TR // HALO
    n_halo = r // HALO
    gated = gated_cols is not None
    xo, bo, co = gated_cols if gated else (0, 0, 0)

    def main(o):
        return pl.BlockSpec((TR, CONV_CW), lambda i, j: (i, o + j))

    def prev(o):
        return pl.BlockSpec((HALO, CONV_CW), lambda i, j: (jnp.maximum(i * nh - 1, 0), o + j))

    def nxt(o):
        return pl.BlockSpec((HALO, CONV_CW), lambda i, j: (jnp.minimum((i + 1) * nh, n_halo - 1), o + j))

    if gated:
        in_specs = [main(xo), main(bo), main(co), prev(xo), prev(co), nxt(xo), nxt(co)]
        args = [x] * 7
    else:
        in_specs = [main(0), prev(0), nxt(0)]
        args = [x] * 3
    in_specs += [pl.BlockSpec((len(offsets), CONV_CW), lambda i, j: (0, j)),
                 pl.BlockSpec((1, CONV_CW), lambda i, j: (0, j))]
    return pl.pallas_call(
        functools.partial(_dwconv_kernel, offsets=offsets, gated=gated, n_ctx_tiles=n_ctx_tiles,
                          tiles_per_seg=tiles_per_seg),
        out_shape=jax.ShapeDtypeStruct((r, c), out_dtype),
        grid=(r // TR, c // CONV_CW),
        in_specs=in_specs,
        out_specs=pl.BlockSpec((TR, CONV_CW), lambda i, j: (i, j)),
        compiler_params=pltpu.CompilerParams(dimension_semantics=("arbitrary", "arbitrary")),
        name="dwconv_gated" if gated else "dwconv",
    )(*args, w, b.reshape(1, c))


def _attn_kernel(sink_ref, q_ref, kc_ref, vc_ref, kp_ref, ko_ref, kn_ref, vp_ref, vo_ref, vn_ref, o_ref,
                 *, tiles_per_seg):
    t = pl.program_id(1)
    tt = t - 1
    tr = q_ref.shape[0]
    n_ctx = kc_ref.shape[0]
    halo = kp_ref.shape[0]
    n_win = tr + 2 * halo
    qi = lax.broadcasted_iota(jnp.int32, (tr, n_win), 0)
    c = lax.broadcasted_iota(jnp.int32, (tr, n_win), 1)
    ok = (c >= qi) & (c <= qi + 2 * WINDOW)
    ok = ok & ((c >= halo) | (tt > 0)) & ((c < halo + tr) | (tt < tiles_per_seg - 1)) & (t > 0)
    bias = jnp.concatenate([jnp.zeros((tr, n_ctx), F32), jnp.where(ok, 0.0, NEG_INF)], axis=1)
    for hk in range(N_KV_HEADS):
        cs = slice(hk * HEAD_DIM, (hk + 1) * HEAD_DIM)
        k_all = jnp.concatenate([kc_ref[:, cs], kp_ref[:, cs], ko_ref[:, cs], kn_ref[:, cs]], axis=0)
        v_all = jnp.concatenate([vc_ref[:, cs], vp_ref[:, cs], vo_ref[:, cs], vn_ref[:, cs]], axis=0)
        for g in range(Q_PER_KV):
            hq = hk * Q_PER_KV + g
            qs = slice(hq * HEAD_DIM, (hq + 1) * HEAD_DIM)
            s = lax.dot_general(q_ref[:, qs], k_all, (((1,), (1,)), ((), ())),
                                preferred_element_type=F32) + bias
            sink = sink_ref[hq]
            m = jnp.maximum(jnp.max(s, axis=-1, keepdims=True), sink)
            p = jnp.exp(s - m)
            denom = jnp.sum(p, axis=-1, keepdims=True) + jnp.exp(sink - m)
            o = jnp.dot(p.astype(BF16), v_all, preferred_element_type=F32)
            o_ref[:, qs] = (o / denom).astype(o_ref.dtype)


def _attention(z, sinks, n_batch, n_ctx_tiles, tiles_per_seg):
    r = z.shape[0]
    q_dim = N_Q_HEADS * HEAD_DIM
    kv_dim = N_KV_HEADS * HEAD_DIM
    k_col = q_dim // kv_dim
    v_col = k_col + 1
    halo = WINDOW
    per = TR // halo
    n_halo_blk = r // halo

    def qblk(b, t):
        return jnp.where(t == 0, b, n_ctx_tiles + b * tiles_per_seg + t - 1)

    def own(col):
        return pl.BlockSpec((TR, kv_dim), lambda b, t: (qblk(b, t), col))

    def ctx(col):
        return pl.BlockSpec((TR, kv_dim), lambda b, t: (b, col))

    def prev(col):
        return pl.BlockSpec((halo, kv_dim), lambda b, t: (jnp.maximum(qblk(b, t) * per - 1, 0), col))

    def nxt(col):
        return pl.BlockSpec((halo, kv_dim),
                            lambda b, t: (jnp.minimum((qblk(b, t) + 1) * per, n_halo_blk - 1), col))

    return pl.pallas_call(
        functools.partial(_attn_kernel, tiles_per_seg=tiles_per_seg),
        out_shape=jax.ShapeDtypeStruct((r, q_dim), BF16),
        grid=(n_batch, 1 + tiles_per_seg),
        in_specs=[pl.BlockSpec(memory_space=pltpu.SMEM),
                  pl.BlockSpec((TR, q_dim), lambda b, t: (qblk(b, t), 0)),
                  ctx(k_col), ctx(v_col),
                  prev(k_col), own(k_col), nxt(k_col),
                  prev(v_col), own(v_col), nxt(v_col)],
        out_specs=pl.BlockSpec((TR, q_dim), lambda b, t: (qblk(b, t), 0)),
        compiler_params=pltpu.CompilerParams(dimension_semantics=("arbitrary", "arbitrary")),
        name="attention",
    )(sinks, z, z, z, z, z, z, z, z, z)


def _l0_out_kernel(a1_ref, a2_ref, w1_ref, w2_ref, h_ref, gate_ref, o_ref):
    y = jnp.dot(a1_ref[...], w1_ref[...], preferred_element_type=F32)
    y = y + jnp.dot(a2_ref[...], w2_ref[...], preferred_element_type=F32)
    o_ref[...] = h_ref[...] + gate_ref[...] * y


def _l0_out_proj(attn, conv, w_out, h, mods, tile_to_mod):
    r, d = h.shape
    k1 = attn.shape[1]
    k2 = conv.shape[1]
    assert k1 == k2 and w_out.shape[0] == k1 + k2
    tn = TN_OUT
    return pl.pallas_call(
        _l0_out_kernel,
        out_shape=jax.ShapeDtypeStruct((r, d), F32),
        grid=(r // TM, d // tn),
        in_specs=[pl.BlockSpec((TM, k1), lambda i, j: (i, 0)),
                  pl.BlockSpec((TM, k2), lambda i, j: (i, 0)),
                  pl.BlockSpec((k1, tn), lambda i, j: (0, j)),
                  pl.BlockSpec((k2, tn), lambda i, j: (1, j)),
                  pl.BlockSpec((TM, tn), lambda i, j: (i, j)),
                  _mod_spec(2, tn, tile_to_mod, True)],
        out_specs=pl.BlockSpec((TM, tn), lambda i, j: (i, j)),
        compiler_params=pltpu.CompilerParams(
            dimension_semantics=("arbitrary", "arbitrary"),
            vmem_limit_bytes=_vmem_limit(2 * TM * (k1 + k2) * 2 + 2 * (k1 + k2) * tn * 2
                                         + 4 * TM * tn * 4 + TM * tn * 4)),
        name="l0_out_proj",
    )(attn, conv, w_out, w_out, h, mods)


def _ffn_kernel(h_ref, g_ref, sh_ref, sc_ref, gate_ref, wg_ref, wu_ref, wd_ref, o_ref, u_scr):
    j = pl.program_id(1)
    nj = pl.num_programs(1)

    @pl.when(j == 0)
    def _():
        u_scr[...] = _norm_mod(h_ref[...], g_ref[...], sh_ref[...], sc_ref[...]).astype(BF16)

    u = u_scr[...]
    gt = jnp.dot(u, wg_ref[...], preferred_element_type=F32)
    up = jnp.dot(u, wu_ref[...], preferred_element_type=F32)
    act = (gt * jax.nn.sigmoid(gt) * up).astype(BF16)
    y = jnp.dot(act, wd_ref[...], preferred_element_type=F32)

    @pl.when(j == 0)
    def _():
        o_ref[...] = y

    @pl.when(j != 0)
    def _():
        o_ref[...] += y

    @pl.when(j == nj - 1)
    def _():
        o_ref[...] = h_ref[...] + gate_ref[...] * o_ref[...]


def _ffn(h, g, mods, w_gate, w_up, w_down, tile_to_mod):
    r, d = h.shape
    f = w_gate.shape[1]
    tf = TF_FFN
    return pl.pallas_call(
        _ffn_kernel,
        out_shape=jax.ShapeDtypeStruct((r, d), F32),
        grid=(r // TM, f // tf),
        in_specs=[pl.BlockSpec((TM, d), lambda i, j: (i, 0)),
                  pl.BlockSpec((1, d), lambda i, j: (0, 0)),
                  _mod_spec(3, d, tile_to_mod, False),
                  _mod_spec(4, d, tile_to_mod, False),
                  _mod_spec(5, d, tile_to_mod, False),
                  pl.BlockSpec((d, tf), lambda i, j: (0, j)),
                  pl.BlockSpec((d, tf), lambda i, j: (0, j)),
                  pl.BlockSpec((tf, d), lambda i, j: (j, 0))],
        out_specs=pl.BlockSpec((TM, d), lambda i, j: (i, 0)),
        scratch_shapes=[pltpu.VMEM((TM, d), BF16)],
        compiler_params=pltpu.CompilerParams(
            dimension_semantics=("arbitrary", "arbitrary"),
            vmem_limit_bytes=_vmem_limit(4 * TM * d * 4 + 6 * d * tf * 2 + TM * d * 2
                                         + 3 * TM * tf * 4 + TM * d * 4)),
        name="ffn",
    )(h, g.reshape(1, d), mods, mods, mods, w_gate, w_up, w_down)


def _l1_in_kernel(h_ref, g_ref, sh_ref, sc_ref, w_ref, gg_ref, xr_ref, u_scr, *, n_gate_tiles):
    j = pl.program_id(1)

    @pl.when(j == 0)
    def _():
        u_scr[...] = _norm_mod(h_ref[...], g_ref[...], sh_ref[...], sc_ref[...]).astype(BF16)

    z = jnp.dot(u_scr[...], w_ref[...], preferred_element_type=F32)

    @pl.when(j < n_gate_tiles)
    def _():
        gg_ref[...] = jax.nn.gelu(z, approximate=True).astype(gg_ref.dtype)

    @pl.when(j >= n_gate_tiles)
    def _():
        xr_ref[...] = z


def _l1_in_proj(h, g, mods, w_in, d_rnn, tile_to_mod):
    r, d = h.shape
    tn = TN_OUT
    ng = d_rnn // tn
    return pl.pallas_call(
        functools.partial(_l1_in_kernel, n_gate_tiles=ng),
        out_shape=(jax.ShapeDtypeStruct((r, d_rnn), BF16), jax.ShapeDtypeStruct((r, d_rnn), F32)),
        grid=(r // TM, 2 * ng),
        in_specs=[pl.BlockSpec((TM, d), lambda i, j: (i, 0)),
                  pl.BlockSpec((1, d), lambda i, j: (0, 0)),
                  _mod_spec(0, d, tile_to_mod, False),
                  _mod_spec(1, d, tile_to_mod, False),
                  pl.BlockSpec((d, tn), lambda i, j: (0, j))],
        out_specs=(pl.BlockSpec((TM, tn), lambda i, j: (i, jnp.minimum(j, ng - 1))),
                   pl.BlockSpec((TM, tn), lambda i, j: (i, jnp.maximum(j - ng, 0)))),
        scratch_shapes=[pltpu.VMEM((TM, d), BF16)],
        compiler_params=pltpu.CompilerParams(
            dimension_semantics=("arbitrary", "arbitrary"),
            vmem_limit_bytes=_vmem_limit(2 * TM * d * 4 + 2 * d * tn * 2 + 2 * TM * tn * 6
                                         + TM * d * 2 + 2 * TM * tn * 4)),
        name="l1_in_proj",
    )(h, g.reshape(1, d), mods, mods, w_in)


def _rglru_kernel(xf_ref, xb_ref, gaw_ref, gxw_ref, gab_ref, gxb_ref, lam_ref, hf_ref, hb_ref,
                  a_scr, b_scr, h_scr):
    c = pl.program_id(1)
    t_len, d_rnn = xf_ref.shape
    hd = d_rnn // N_RNN_HEADS

    @pl.when(c == 0)
    def _():
        h_scr[...] = jnp.zeros_like(h_scr)

    for z, x_ref in ((0, xf_ref), (1, xb_ref)):
        lam = lam_ref[z]
        neg_softplus_arg = -lam
        sp = jnp.maximum(neg_softplus_arg, 0.0) + jnp.log1p(jnp.exp(-jnp.abs(neg_softplus_arg)))
        for hh in range(N_RNN_HEADS):
            sl = slice(hh * hd, (hh + 1) * hd)
            xh = x_ref[:, sl]
            xh16 = xh.astype(BF16)
            ra = jnp.dot(xh16, gaw_ref[z, hh], preferred_element_type=F32) + gab_ref[z][:, sl]
            ri = jnp.dot(xh16, gxw_ref[z, hh], preferred_element_type=F32) + gxb_ref[z][:, sl]
            log_a = -RG_C * jax.nn.sigmoid(ra) * sp[:, sl]
            a = jnp.exp(log_a)
            a_scr[z, :, sl] = a
            b_scr[z, :, sl] = jnp.sqrt(1.0 - a * a) * jax.nn.sigmoid(ri) * xh

    row8 = lax.broadcasted_iota(jnp.int32, (8, d_rnn), 0)
    n_grp = t_len // 8

    def fwd_body(g, h):
        r0 = pl.multiple_of(g * 8, 8)
        a = a_scr[0, pl.ds(r0, 8), :]
        b = b_scr[0, pl.ds(r0, 8), :]
        for s in (1, 2, 4):
            a_sh = jnp.where(row8 >= s, pltpu.roll(a, s, 0), 1.0)
            b_sh = jnp.where(row8 >= s, pltpu.roll(b, s, 0), 0.0)
            b = a * b_sh + b
            a = a * a_sh
        out = a * h + b
        hf_ref[pl.ds(r0, 8), :] = out
        return jnp.broadcast_to(out[7:8, :], (8, d_rnn))

    def bwd_body(k, h):
        r0 = pl.multiple_of((n_grp - 1 - k) * 8, 8)
        a = a_scr[1, pl.ds(r0, 8), :]
        b = b_scr[1, pl.ds(r0, 8), :]
        for s in (1, 2, 4):
            a_sh = jnp.where(row8 < 8 - s, pltpu.roll(a, 8 - s, 0), 1.0)
            b_sh = jnp.where(row8 < 8 - s, pltpu.roll(b, 8 - s, 0), 0.0)
            b = a * b_sh + b
            a = a * a_sh
        out = a * h + b
        hb_ref[pl.ds(r0, 8), :] = out
        return jnp.broadcast_to(out[0:1, :], (8, d_rnn))

    h_scr[0] = lax.fori_loop(0, n_grp, fwd_body, h_scr[0])
    h_scr[1] = lax.fori_loop(0, n_grp, bwd_body, h_scr[1])


def _rglru(xc, ga_w, gx_w, ga_b, gx_b, lam, n_batch, n_ctx_tiles, tiles_per_seg):
    d_rnn = xc.shape[1]
    n_lat = n_batch * tiles_per_seg * TR
    hd = d_rnn // N_RNN_HEADS
    nt = tiles_per_seg

    def fblk(b, c):
        return jnp.where(c == 0, b, n_ctx_tiles + b * nt + c - 1)

    def bblk(b, c):
        return jnp.where(c == 0, b, n_ctx_tiles + b * nt + nt - c)

    def full(shape):
        return pl.BlockSpec(shape, lambda b, c: (0,) * len(shape))

    return pl.pallas_call(
        _rglru_kernel,
        out_shape=(jax.ShapeDtypeStruct((n_lat, d_rnn), F32), jax.ShapeDtypeStruct((n_lat, d_rnn), F32)),
        grid=(n_batch, 1 + nt),
        in_specs=[pl.BlockSpec((TR, d_rnn), lambda b, c: (fblk(b, c), 0)),
                  pl.BlockSpec((TR, d_rnn), lambda b, c: (bblk(b, c), 0)),
                  full((2, N_RNN_HEADS, hd, hd)), full((2, N_RNN_HEADS, hd, hd)),
                  full((2, 1, d_rnn)), full((2, 1, d_rnn)), full((2, 1, d_rnn))],
        out_specs=(pl.BlockSpec((TR, d_rnn), lambda b, c: (b * nt + jnp.maximum(c - 1, 0), 0)),
                   pl.BlockSpec((TR, d_rnn), lambda b, c: (b * nt + jnp.where(c == 0, nt - 1, nt - c), 0))),
        scratch_shapes=[pltpu.VMEM((2, TR, d_rnn), F32), pltpu.VMEM((2, TR, d_rnn), F32),
                        pltpu.VMEM((2, 8, d_rnn), F32)],
        compiler_params=pltpu.CompilerParams(
            dimension_semantics=("arbitrary", "arbitrary"),
            vmem_limit_bytes=_vmem_limit(12 * TR * d_rnn * 4 + 8 * N_RNN_HEADS * hd * hd * 2)),
        name="rglru",
    )(xc, xc, ga_w, gx_w, ga_b.reshape(2, 1, d_rnn), gx_b.reshape(2, 1, d_rnn), lam.reshape(2, 1, d_rnn))


def _l1_out_kernel(hf_ref, hb_ref, gg_ref, w_ref, h_ref, gate_ref, o_ref, lhs_scr):
    j = pl.program_id(1)

    @pl.when(j == 0)
    def _():
        lhs_scr[...] = ((hf_ref[...] + hb_ref[...]) * gg_ref[...].astype(F32)).astype(BF16)

    y = jnp.dot(lhs_scr[...], w_ref[...], preferred_element_type=F32)
    o_ref[...] = h_ref[...] + gate_ref[...] * y


def _l1_out_proj(hf, hb, gg, w_out, h, mods, n_ctx_mtiles, lat_tile_to_mod):
    n_lat, d_rnn = hf.shape
    d = h.shape[1]
    tn = TN_OUT
    return pl.pallas_call(
        _l1_out_kernel,
        out_shape=jax.ShapeDtypeStruct((n_lat, d), F32),
        grid=(n_lat // TM, d // tn),
        in_specs=[pl.BlockSpec((TM, d_rnn), lambda i, j: (i, 0)),
                  pl.BlockSpec((TM, d_rnn), lambda i, j: (i, 0)),
                  pl.BlockSpec((TM, d_rnn), lambda i, j: (i + n_ctx_mtiles, 0)),
                  pl.BlockSpec((d_rnn, tn), lambda i, j: (0, j)),
                  pl.BlockSpec((TM, tn), lambda i, j: (i + n_ctx_mtiles, j)),
                  _mod_spec(2, tn, lat_tile_to_mod, True)],
        out_specs=pl.BlockSpec((TM, tn), lambda i, j: (i, j)),
        scratch_shapes=[pltpu.VMEM((TM, d_rnn), BF16)],
        compiler_params=pltpu.CompilerParams(
            dimension_semantics=("arbitrary", "arbitrary"),
            vmem_limit_bytes=_vmem_limit(4 * TM * d_rnn * 4 + 2 * TM * d_rnn * 2 + 2 * d_rnn * tn * 2
                                         + 4 * TM * tn * 4 + TM * d_rnn * 2 + TM * d_rnn * 4)),
        name="l1_out_proj",
    )(hf, hb, gg, w_out, h, mods)


def _router_kernel(h_ref, g_ref, sh_ref, sc_ref, rw_ref, rb_ref, u_ref, route_ref):
    u = _norm_mod(h_ref[...], g_ref[...], sh_ref[...], sc_ref[...])
    u_ref[...] = u
    logits = jnp.dot(u.astype(BF16), rw_ref[...], preferred_element_type=F32) + rb_ref[...]
    lane = lax.broadcasted_iota(jnp.int32, logits.shape, 1)
    m1 = jnp.max(logits, axis=-1, keepdims=True)
    i1 = jnp.min(jnp.where(logits == m1, lane, LANES), axis=-1, keepdims=True)
    rest = jnp.where(lane == i1, -jnp.inf, logits)
    m2 = jnp.max(rest, axis=-1, keepdims=True)
    i2 = jnp.min(jnp.where(rest == m2, lane, LANES), axis=-1, keepdims=True)
    e2 = jnp.exp(m2 - m1)
    w1 = 1.0 / (1.0 + e2)
    w2 = e2 * w1
    route = jnp.where(lane == 0, i1.astype(F32),
                      jnp.where(lane == 1, i2.astype(F32),
                                jnp.where(lane == 2, w1, jnp.where(lane == 3, w2, 0.0))))
    route_ref[...] = route


def _router(h, g, mods, rw_pad, rb_pad, lat_tile_to_mod):
    n, d = h.shape

    def mspec(chunk):
        return pl.BlockSpec((None, 1, d), lambda i: (lat_tile_to_mod(i) * N_MOD + chunk, 0, 0))

    return pl.pallas_call(
        _router_kernel,
        out_shape=(jax.ShapeDtypeStruct((n, d), F32), jax.ShapeDtypeStruct((n, LANES), F32)),
        grid=(n // TM,),
        in_specs=[pl.BlockSpec((TM, d), lambda i: (i, 0)),
                  pl.BlockSpec((1, d), lambda i: (0, 0)),
                  mspec(3), mspec(4),
                  pl.BlockSpec((d, LANES), lambda i: (0, 0)),
                  pl.BlockSpec((1, LANES), lambda i: (0, 0))],
        out_specs=(pl.BlockSpec((TM, d), lambda i: (i, 0)),
                   pl.BlockSpec((TM, LANES), lambda i: (i, 0))),
        compiler_params=pltpu.CompilerParams(
            dimension_semantics=("arbitrary",),
            vmem_limit_bytes=_vmem_limit(6 * TM * d * 4)),
        name="router",
    )(h, g.reshape(1, d), mods, mods, rw_pad, rb_pad)


def _row_copy(src, src_row, dst, dst_row, sem):
    return pltpu.make_async_copy(src.at[pl.ds(src_row, 1)], dst.at[pl.ds(dst_row, 1)], sem)


def _dispatch_kernel(dest_ref, u_hbm, xs_in, xs_out, sem, *, rows):
    del xs_in
    i = pl.program_id(0)

    def start(r, carry):
        for kk in range(TOP_K):
            _row_copy(u_hbm, i * rows + r, xs_out, dest_ref[TOP_K * r + kk], sem).start()
        return carry

    lax.fori_loop(0, rows, start, 0)

    def wait(r, carry):
        for kk in range(TOP_K):
            _row_copy(u_hbm, 0, xs_out, 0, sem).wait()
        return carry

    lax.fori_loop(0, rows, wait, 0)


def _dispatch(u, dest, cap):
    n, d = u.shape
    xs0 = jnp.zeros((cap, d), u.dtype)
    return pl.pallas_call(
        functools.partial(_dispatch_kernel, rows=TM),
        out_shape=jax.ShapeDtypeStruct((cap, d), u.dtype),
        grid=(n // TM,),
        in_specs=[pl.BlockSpec((TOP_K * TM,), lambda i: (i,), memory_space=pltpu.SMEM),
                  pl.BlockSpec(memory_space=pl.ANY),
                  pl.BlockSpec(memory_space=pl.ANY)],
        out_specs=pl.BlockSpec(memory_space=pl.ANY),
        scratch_shapes=[pltpu.SemaphoreType.DMA],
        input_output_aliases={2: 0},
        compiler_params=pltpu.CompilerParams(dimension_semantics=("arbitrary",)),
        name="moe_dispatch",
    )(dest, u, xs0)


def _moe_kernel(te_ref, tv_ref, tr_ref, xs_ref, wg_ref, wu_ref, wd_ref, o_ref, x_scr):
    del te_ref, tr_ref
    i = pl.program_id(0)
    j = pl.program_id(1)

    @pl.when(tv_ref[i] == 1)
    def _():
        @pl.when(j == 0)
        def _():
            x_scr[...] = xs_ref[...].astype(BF16)

        x = x_scr[...]
        gt = jnp.dot(x, wg_ref[...], preferred_element_type=F32)
        up = jnp.dot(x, wu_ref[...], preferred_element_type=F32)
        act = (gt * jax.nn.sigmoid(gt) * up).astype(BF16)
        y = jnp.dot(act, wd_ref[...], preferred_element_type=F32)

        @pl.when(j == 0)
        def _():
            o_ref[...] = y

        @pl.when(j != 0)
        def _():
            o_ref[...] += y

    @pl.when((tv_ref[i] == 0) & (j == 0))
    def _():
        o_ref[...] = jnp.zeros_like(o_ref)


def _moe(xs, tile_e, tile_valid, tile_row, w_gate, w_up, w_down):
    cap, d = xs.shape
    f = w_gate.shape[2]
    tf = TF_MOE
    nj = f // tf

    def jj(i, j, tv):
        return jnp.where(tv[i] == 1, j, nj - 1)

    return pl.pallas_call(
        _moe_kernel,
        out_shape=jax.ShapeDtypeStruct((cap, d), F32),
        grid_spec=pltpu.PrefetchScalarGridSpec(
            num_scalar_prefetch=3,
            grid=(cap // TM, nj),
            in_specs=[pl.BlockSpec((TM, d), lambda i, j, te, tv, tr: (tr[i], 0)),
                      pl.BlockSpec((None, d, tf), lambda i, j, te, tv, tr: (te[i], 0, jj(i, j, tv))),
                      pl.BlockSpec((None, d, tf), lambda i, j, te, tv, tr: (te[i], 0, jj(i, j, tv))),
                      pl.BlockSpec((None, tf, d), lambda i, j, te, tv, tr: (te[i], jj(i, j, tv), 0))],
            out_specs=pl.BlockSpec((TM, d), lambda i, j, te, tv, tr: (i, 0)),
            scratch_shapes=[pltpu.VMEM((TM, d), BF16)]),
        compiler_params=pltpu.CompilerParams(
            dimension_semantics=("arbitrary", "arbitrary"),
            vmem_limit_bytes=_vmem_limit(4 * TM * d * 4 + 6 * d * tf * 2 + TM * d * 2
                                         + 3 * TM * tf * 4 + TM * d * 4)),
        name="moe_experts",
    )(tile_e, tile_valid, tile_row, xs, w_gate, w_up, w_down)


def _combine_kernel(dest_ref, route_ref, h_ref, gate_ref, fg_ref, yb_hbm, o_ref, y_scr, sem, *, rows):
    def start(r, carry):
        for kk in range(TOP_K):
            _row_copy(yb_hbm, dest_ref[TOP_K * r + kk], y_scr.at[kk], r, sem).start()
        return carry

    lax.fori_loop(0, rows, start, 0)

    def wait(r, carry):
        for kk in range(TOP_K):
            _row_copy(yb_hbm, 0, y_scr.at[kk], 0, sem).wait()
        return carry

    lax.fori_loop(0, rows, wait, 0)

    route = route_ref[...]
    moe = route[:, 2:3] * y_scr[0] + route[:, 3:4] * y_scr[1]
    hl = h_ref[...] + gate_ref[...] * moe
    ms = jnp.mean(hl * hl, axis=-1, keepdims=True)
    o_ref[...] = (hl * lax.rsqrt(ms + NORM_EPS)) * fg_ref[...]


def _combine(yb, dest, route, h, mods, final_g, lat_tile_to_mod):
    n, d = h.shape
    return pl.pallas_call(
        functools.partial(_combine_kernel, rows=TM),
        out_shape=jax.ShapeDtypeStruct((n, d), F32),
        grid=(n // TM,),
        in_specs=[pl.BlockSpec((TOP_K * TM,), lambda i: (i,), memory_space=pltpu.SMEM),
                  pl.BlockSpec((TM, LANES), lambda i: (i, 0)),
                  pl.BlockSpec((TM, d), lambda i: (i, 0)),
                  pl.BlockSpec((None, 1, d), lambda i: (lat_tile_to_mod(i) * N_MOD + 5, 0, 0)),
                  pl.BlockSpec((1, d), lambda i: (0, 0)),
                  pl.BlockSpec(memory_space=pl.ANY)],
        out_specs=pl.BlockSpec((TM, d), lambda i: (i, 0)),
        scratch_shapes=[pltpu.VMEM((TOP_K, TM, d), F32), pltpu.SemaphoreType.DMA],
        compiler_params=pltpu.CompilerParams(
            dimension_semantics=("arbitrary",),
            vmem_limit_bytes=_vmem_limit(TOP_K * TM * d * 4 + 4 * TM * d * 4 + 2 * TM * d * 4)),
        name="moe_combine",
    )(dest, route, h, mods, final_g.reshape(1, d), yb)


def _rope_tables(n_ident, s_len):
    t = jnp.arange(s_len)
    row_id = (t // GRID_W).astype(F32)
    col_id = (t % GRID_W).astype(F32)
    inv_freq = ROPE_BASE ** (-jnp.arange(ROPE_PAIRS, dtype=F32) / ROPE_PAIRS)
    ang_r = row_id[:, None] * inv_freq
    ang_c = col_id[:, None] * inv_freq
    ang = jnp.concatenate([ang_r, ang_r, ang_c, ang_c], axis=-1)
    cos, sin = jnp.cos(ang), jnp.sin(ang)
    first_half = (jnp.arange(HEAD_DIM) % (2 * ROPE_PAIRS)) < ROPE_PAIRS
    sa = jnp.where(first_half, -sin, 0.0)
    sb = jnp.where(first_half, 0.0, sin)
    ones = jnp.ones((n_ident, HEAD_DIM), F32)
    zeros = jnp.zeros((n_ident, HEAD_DIM), F32)
    return (jnp.concatenate([ones, cos]), jnp.concatenate([zeros, sa]), jnp.concatenate([zeros, sb]))


def _routing_plan(route, n_tiles):
    e = route[:, :TOP_K].astype(jnp.int32).reshape(-1)
    onehot = (e[:, None] == jnp.arange(N_EXPERTS, dtype=jnp.int32)[None, :]).astype(jnp.int32)
    csum = jnp.cumsum(onehot, axis=0)
    rank = jnp.sum(csum * onehot, axis=1) - 1
    counts = csum[-1]
    padded = (counts + TM - 1) // TM * TM
    pad_end = jnp.cumsum(padded)
    pad_start = pad_end - padded
    dest = jnp.sum(onehot * pad_start[None, :], axis=1) + rank
    tile_start = jnp.arange(n_tiles, dtype=jnp.int32) * TM
    n_valid = pad_end[-1] // TM
    tile_idx = jnp.arange(n_tiles, dtype=jnp.int32)
    tile_valid = (tile_idx < n_valid).astype(jnp.int32)
    tile_row = jnp.minimum(tile_idx, n_valid - 1)
    tile_e = jnp.sum((pad_end[None, :] <= (tile_row * TM)[:, None]).astype(jnp.int32), axis=1)
    tile_e = jnp.minimum(tile_e, N_EXPERTS - 1)
    del tile_start
    return dest.astype(jnp.int32), tile_e.astype(jnp.int32), tile_valid, tile_row.astype(jnp.int32)


def _mods(cvecs, w_mod, b_mod):
    d = w_mod.shape[0]
    m = _adaln(cvecs, w_mod, b_mod)[:3]
    return m.reshape(3 * N_MOD, 1, d)


def kernel(x, c, ctx, c_ctx, l0_w_mod, l0_b_mod, l0_norm1_g, l0_w_in, l0_sinks, l0_conv_w, l0_conv_b, l0_w_out, l0_norm2_g, l0_ffn_w_gate, l0_ffn_w_up, l0_ffn_w_down, l1_w_mod, l1_b_mod, l1_norm1_g, l1_w_in, l1_conv_w, l1_conv_b, l1_gate_a_w, l1_gate_a_b, l1_gate_x_w, l1_gate_x_b, l1_lambda, l1_w_out, l1_norm2_g, l1_router_w, l1_router_b, l1_moe_w_gate, l1_moe_w_up, l1_moe_w_down, final_norm_g):
    n_batch, s_len, d = x.shape
    n_ctx = ctx.shape[1]
    assert n_batch == 2 and n_batch * n_ctx == TM and n_ctx == TR
    assert s_len % TM == 0 and s_len % GRID_W == 0 and WINDOW * 2 == TR
    n_ctx_rows = n_batch * n_ctx
    n_lat = n_batch * s_len
    n_ctx_rtiles = n_ctx_rows // TR
    tiles_per_seg = s_len // TR
    mtiles_per_seg = s_len // TM
    d_rnn = l1_w_out.shape[0]
    conv_dim = l0_conv_w.shape[1]
    q_dim = N_Q_HEADS * HEAD_DIM
    kv_dim = N_KV_HEADS * HEAD_DIM

    def tile_to_mod(i):
        return jnp.where(i == 0, n_batch, (i - 1) // mtiles_per_seg)

    def lat_tile_to_mod(i):
        return i // mtiles_per_seg

    def rope_blk(i):
        return jnp.where(i == 0, 0, 1 + (i - 1) % mtiles_per_seg)

    cvecs = jnp.concatenate([c, c_ctx[None, :], jnp.zeros((8 - n_batch - 1, d), F32)], axis=0)
    mods0 = _mods(cvecs, l0_w_mod, l0_b_mod)
    mods1 = _mods(cvecs, l1_w_mod, l1_b_mod)
    h = jnp.concatenate([ctx.reshape(n_ctx_rows, d), x.reshape(n_lat, d)], axis=0)

    rope = _rope_tables(TM, s_len)
    z = _l0_in_proj(h, l0_norm1_g, mods0, l0_w_in.astype(BF16), rope, tile_to_mod, rope_blk)
    x_col = (q_dim + 2 * kv_dim) // CONV_CW
    per = conv_dim // CONV_CW
    conv = _dwconv(z, l0_conv_w, l0_conv_b, SHORT_CONV_OFFSETS, n_ctx_rtiles, tiles_per_seg,
                   (x_col, x_col + per, x_col + 2 * per), BF16)
    attn = _attention(z, l0_sinks, n_batch, n_ctx_rtiles, tiles_per_seg)
    h = _l0_out_proj(attn, conv, l0_w_out.astype(BF16), h, mods0, tile_to_mod)
    h = _ffn(h, l0_norm2_g, mods0, l0_ffn_w_gate.astype(BF16), l0_ffn_w_up.astype(BF16),
             l0_ffn_w_down.astype(BF16), tile_to_mod)

    gg, xr = _l1_in_proj(h, l1_norm1_g, mods1, l1_w_in.astype(BF16), d_rnn, tile_to_mod)
    xc = _dwconv(xr, l1_conv_w, l1_conv_b, RG_CONV_OFFSETS, n_ctx_rtiles, tiles_per_seg, None, F32)
    hf, hb = _rglru(xc, l1_gate_a_w.astype(BF16), l1_gate_x_w.astype(BF16), l1_gate_a_b, l1_gate_x_b,
                    l1_lambda, n_batch, n_ctx_rtiles, tiles_per_seg)
    h_lat = _l1_out_proj(hf, hb, gg, l1_w_out.astype(BF16), h, mods1, n_ctx_rows // TM, lat_tile_to_mod)

    rw_pad = jnp.zeros((d, LANES), BF16).at[:, :N_EXPERTS].set(l1_router_w.astype(BF16))
    rb_pad = jnp.full((1, LANES), NEG_INF, F32).at[0, :N_EXPERTS].set(l1_router_b)
    u, route = _router(h_lat, l1_norm2_g, mods1, rw_pad, rb_pad, lat_tile_to_mod)
    n_tiles = (n_lat * TOP_K) // TM + N_EXPERTS
    dest, tile_e, tile_valid, tile_row = _routing_plan(route, n_tiles)
    xs = _dispatch(u, dest, n_tiles * TM)
    yb = _moe(xs, tile_e, tile_valid, tile_row, l1_moe_w_gate.astype(BF16), l1_moe_w_up.astype(BF16),
              l1_moe_w_down.astype(BF16))
    out = _combine(yb, dest, route, h_lat, mods1, final_norm_g, lat_tile_to_mod)
    return out.reshape(n_batch, s_len, d)
```

```python
import functools

import jax
import jax.numpy as jnp
from jax import lax
from jax.experimental import pallas as pl
from jax.experimental.pallas import tpu as pltpu

F32 = jnp.float32
BF16 = jnp.bfloat16

GRID_W = 64
HEAD_DIM = 128
N_Q_HEADS = 8
N_KV_HEADS = 2
Q_PER_KV = N_Q_HEADS // N_KV_HEADS
WINDOW = 128
ATTN_SCALE = HEAD_DIM ** -0.5
ROPE_BASE = 10000.0
ROPE_PAIRS = HEAD_DIM // 4
SHORT_CONV_OFFSETS = (-1, 0, 1)
RG_CONV_OFFSETS = (-2, -1, 0, 1)
N_RNN_HEADS = 16
RG_C = 8.0
N_EXPERTS = 8
TOP_K = 2
NORM_EPS = 1e-6
NEG_INF = -1e30
N_MOD = 6

LANES = 128
TM = 512
TR = 256
HALO = 16
TN_OUT = 1024
TF_FFN = 512
TF_MOE = 1024
CONV_CW = 512
MOD_TN = 1024
VMEM_CAP = 56 * 1024 * 1024


def _vmem_limit(nbytes):
    return int(min(max(nbytes * 5 // 4 + (4 << 20), 32 << 20), VMEM_CAP))


def _norm_mod(h, g, shift, scale):
    ms = jnp.mean(h * h, axis=-1, keepdims=True)
    y = h * lax.rsqrt(ms + NORM_EPS)
    return (y * g) * (1.0 + scale) + shift


def _adaln_kernel(c_ref, w_ref, b_ref, o_ref):
    c = c_ref[...]
    s = (c * jax.nn.sigmoid(c)).astype(BF16)
    o_ref[...] = jnp.dot(s, w_ref[...].astype(BF16), preferred_element_type=F32) + b_ref[...]


def _adaln(cvecs, w_mod, b_mod):
    d, n = w_mod.shape
    return pl.pallas_call(
        _adaln_kernel,
        out_shape=jax.ShapeDtypeStruct((8, n), F32),
        grid=(n // MOD_TN,),
        in_specs=[pl.BlockSpec((8, d), lambda j: (0, 0)),
                  pl.BlockSpec((d, MOD_TN), lambda j: (0, j)),
                  pl.BlockSpec((1, MOD_TN), lambda j: (0, j))],
        out_specs=pl.BlockSpec((8, MOD_TN), lambda j: (0, j)),
        compiler_params=pltpu.CompilerParams(
            dimension_semantics=("arbitrary",),
            vmem_limit_bytes=_vmem_limit(2 * d * MOD_TN * 4 + d * MOD_TN * 2)),
        name="adaln",
    )(cvecs, w_mod, b_mod.reshape(1, n))


def _mod_spec(chunk, width, tile_to_mod, col_from_j):
    if col_from_j:
        return pl.BlockSpec((None, 1, width), lambda i, j: (tile_to_mod(i) * N_MOD + chunk, 0, j))
    return pl.BlockSpec((None, 1, width), lambda i, j: (tile_to_mod(i) * N_MOD + chunk, 0, 0))


def _l0_in_kernel(h_ref, g_ref, sh_ref, sc_ref, w_ref, cos_ref, sa_ref, sb_ref, o_ref, u_scr):
    j = pl.program_id(1)

    @pl.when(j == 0)
    def _():
        u_scr[...] = _norm_mod(h_ref[...], g_ref[...], sh_ref[...], sc_ref[...]).astype(BF16)

    z = jnp.dot(u_scr[...], w_ref[...], preferred_element_type=F32)

    @pl.when(j == 0)
    def _():
        cos, sa, sb = cos_ref[...], sa_ref[...], sb_ref[...]
        n_rot = N_Q_HEADS + N_KV_HEADS
        for hh in range(n_rot):
            t = z[:, hh * HEAD_DIM:(hh + 1) * HEAD_DIM]
            r = (t * cos + pltpu.roll(t, HEAD_DIM - ROPE_PAIRS, 1) * sa
                 + pltpu.roll(t, ROPE_PAIRS, 1) * sb)
            if hh < N_Q_HEADS:
                r = r * ATTN_SCALE
            o_ref[:, hh * HEAD_DIM:(hh + 1) * HEAD_DIM] = r.astype(BF16)
        o_ref[:, n_rot * HEAD_DIM:] = z[:, n_rot * HEAD_DIM:].astype(BF16)

    @pl.when(j != 0)
    def _():
        o_ref[...] = z.astype(BF16)


def _l0_in_proj(h, g, mods, w_in, rope, tile_to_mod, rope_blk):
    r, d = h.shape
    n = w_in.shape[1]
    tn = (N_Q_HEADS + 2 * N_KV_HEADS) * HEAD_DIM
    assert n % tn == 0
    cos, sa, sb = rope
    rope_spec = pl.BlockSpec((TM, HEAD_DIM), lambda i, j: (rope_blk(i), 0))
    return pl.pallas_call(
        _l0_in_kernel,
        out_shape=jax.ShapeDtypeStruct((r, n), BF16),
        grid=(r // TM, n // tn),
        in_specs=[pl.BlockSpec((TM, d), lambda i, j: (i, 0)),
                  pl.BlockSpec((1, d), lambda i, j: (0, 0)),
                  _mod_spec(0, d, tile_to_mod, False),
                  _mod_spec(1, d, tile_to_mod, False),
                  pl.BlockSpec((d, tn), lambda i, j: (0, j)),
                  rope_spec, rope_spec, rope_spec],
        out_specs=pl.BlockSpec((TM, tn), lambda i, j: (i, j)),
        scratch_shapes=[pltpu.VMEM((TM, d), BF16)],
        compiler_params=pltpu.CompilerParams(
            dimension_semantics=("arbitrary", "arbitrary"),
            vmem_limit_bytes=_vmem_limit(2 * TM * d * 4 + 2 * d * tn * 2 + 2 * TM * tn * 2
                                         + TM * d * 2 + 2 * TM * tn * 4)),
        name="l0_in_proj",
    )(h, g.reshape(1, d), mods, mods, w_in, cos, sa, sb)


def _dwconv_kernel(*refs, offsets, gated, n_ctx_tiles, tiles_per_seg):
    if gated:
        x_ref, bg_ref, cg_ref, xp_ref, cgp_ref, xn_ref, cgn_ref, w_ref, b_ref, o_ref = refs
    else:
        x_ref, xp_ref, xn_ref, w_ref, b_ref, o_ref = refs
    i = pl.program_id(0)
    li = i - n_ctx_tiles
    is_ctx = i < n_ctx_tiles
    first = is_ctx | (li % tiles_per_seg == 0)
    last = is_ctx | (li % tiles_per_seg == tiles_per_seg - 1)
    x = x_ref[...].astype(F32)
    xp = xp_ref[...].astype(F32)
    xn = xn_ref[...].astype(F32)
    if gated:
        x = x * cg_ref[...].astype(F32)
        xp = xp * cgp_ref[...].astype(F32)
        xn = xn * cgn_ref[...].astype(F32)
    xp = jnp.where(first, 0.0, xp)
    xn = jnp.where(last, 0.0, xn)
    tr, cw = x.shape
    row8 = lax.broadcasted_iota(jnp.int32, (8, cw), 0)
    acc = jnp.broadcast_to(b_ref[...], (tr, cw))
    for k, off in enumerate(offsets):
        wk = w_ref[k:k + 1, :]
        if off == 0:
            y = x
        elif off < 0:
            s = -off
            r = pltpu.roll(x, s, 0)
            rp = pltpu.roll(xp, s, 0)[0:8]
            head = jnp.where(row8 < s, rp, r[0:8])
            y = jnp.concatenate([head, r[8:]], axis=0)
        else:
            r = pltpu.roll(x, tr - off, 0)
            rn = pltpu.roll(xn, HALO - off, 0)[HALO - 8:HALO]
            tail = jnp.where(row8 >= 8 - off, rn, r[tr - 8:])
            y = jnp.concatenate([r[:tr - 8], tail], axis=0)
        acc = acc + wk * y
    if gated:
        acc = acc * bg_ref[...].astype(F32)
    o_ref[...] = acc.astype(o_ref.dtype)


def _dwconv(x, w, b, offsets, n_ctx_tiles, tiles_per_seg, gated_cols, out_dtype):
    r = x.shape[0]
    c = w.shape[1]
    nh = TR // HALO
    n_halo = r // HALO
    gated = gated_cols is not None
    xo, bo, co = gated_cols if gated else (0, 0, 0)

    def main(o):
        return pl.BlockSpec((TR, CONV_CW), lambda i, j: (i, o + j))

    def prev(o):
        return pl.BlockSpec((HALO, CONV_CW), lambda i, j: (jnp.maximum(i * nh - 1, 0), o + j))

    def nxt(o):
        return pl.BlockSpec((HALO, CONV_CW), lambda i, j: (jnp.minimum((i + 1) * nh, n_halo - 1), o + j))

    if gated:
        in_specs = [main(xo), main(bo), main(co), prev(xo), prev(co), nxt(xo), nxt(co)]
        args = [x] * 7
    else:
        in_specs = [main(0), prev(0), nxt(0)]
        args = [x] * 3
    in_specs += [pl.BlockSpec((len(offsets), CONV_CW), lambda i, j: (0, j)),
                 pl.BlockSpec((1, CONV_CW), lambda i, j: (0, j))]
    return pl.pallas_call(
        functools.partial(_dwconv_kernel, offsets=offsets, gated=gated, n_ctx_tiles=n_ctx_tiles,
                          tiles_per_seg=tiles_per_seg),
        out_shape=jax.ShapeDtypeStruct((r, c), out_dtype),
        grid=(r // TR, c // CONV_CW),
        in_specs=in_specs,
        out_specs=pl.BlockSpec((TR, CONV_CW), lambda i, j: (i, j)),
        compiler_params=pltpu.CompilerParams(dimension_semantics=("arbitrary", "arbitrary")),
        name="dwconv_gated" if gated else "dwconv",
    )(*args, w, b.reshape(1, c))


def _attn_kernel(sink_ref, q_ref, kc_ref, vc_ref, kp_ref, ko_ref, kn_ref, vp_ref, vo_ref, vn_ref, o_ref,
                 *, tiles_per_seg):
    t = pl.program_id(1)
    tt = t - 1
    tr = q_ref.shape[0]
    n_ctx = kc_ref.shape[0]
    halo = kp_ref.shape[0]
    n_win = tr + 2 * halo
    qi = lax.broadcasted_iota(jnp.int32, (tr, n_win), 0)
    c = lax.broadcasted_iota(jnp.int32, (tr, n_win), 1)
    ok = (c >= qi) & (c <= qi + 2 * WINDOW)
    ok = ok & ((c >= halo) | (tt > 0)) & ((c < halo + tr) | (tt < tiles_per_seg - 1)) & (t > 0)
    bias = jnp.concatenate([jnp.zeros((tr, n_ctx), F32), jnp.where(ok, 0.0, NEG_INF)], axis=1)
    for hk in range(N_KV_HEADS):
        cs = slice(hk * HEAD_DIM, (hk + 1) * HEAD_DIM)
        k_all = jnp.concatenate([kc_ref[:, cs], kp_ref[:, cs], ko_ref[:, cs], kn_ref[:, cs]], axis=0)
        v_all = jnp.concatenate([vc_ref[:, cs], vp_ref[:, cs], vo_ref[:, cs], vn_ref[:, cs]], axis=0)
        for g in range(Q_PER_KV):
            hq = hk * Q_PER_KV + g
            qs = slice(hq * HEAD_DIM, (hq + 1) * HEAD_DIM)
            s = lax.dot_general(q_ref[:, qs], k_all, (((1,), (1,)), ((), ())),
                                preferred_element_type=F32) + bias
            sink = sink_ref[hq]
            m = jnp.maximum(jnp.max(s, axis=-1, keepdims=True), sink)
            p = jnp.exp(s - m)
            denom = jnp.sum(p, axis=-1, keepdims=True) + jnp.exp(sink - m)
            o = jnp.dot(p.astype(BF16), v_all, preferred_element_type=F32)
            o_ref[:, qs] = (o / denom).astype(o_ref.dtype)


def _attention(z, sinks, n_batch, n_ctx_tiles, tiles_per_seg):
    r = z.shape[0]
    q_dim = N_Q_HEADS * HEAD_DIM
    kv_dim = N_KV_HEADS * HEAD_DIM
    k_col = q_dim // kv_dim
    v_col = k_col + 1
    halo = WINDOW
    per = TR // halo
    n_halo_blk = r // halo

    def qblk(b, t):
        return jnp.where(t == 0, b, n_ctx_tiles + b * tiles_per_seg + t - 1)

    def own(col):
        return pl.BlockSpec((TR, kv_dim), lambda b, t: (qblk(b, t), col))

    def ctx(col):
        return pl.BlockSpec((TR, kv_dim), lambda b, t: (b, col))

    def prev(col):
        return pl.BlockSpec((halo, kv_dim), lambda b, t: (jnp.maximum(qblk(b, t) * per - 1, 0), col))

    def nxt(col):
        return pl.BlockSpec((halo, kv_dim),
                            lambda b, t: (jnp.minimum((qblk(b, t) + 1) * per, n_halo_blk - 1), col))

    return pl.pallas_call(
        functools.partial(_attn_kernel, tiles_per_seg=tiles_per_seg),
        out_shape=jax.ShapeDtypeStruct((r, q_dim), BF16),
        grid=(n_batch, 1 + tiles_per_seg),
        in_specs=[pl.BlockSpec(memory_space=pltpu.SMEM),
                  pl.BlockSpec((TR, q_dim), lambda b, t: (qblk(b, t), 0)),
                  ctx(k_col), ctx(v_col),
                  prev(k_col), own(k_col), nxt(k_col),
                  prev(v_col), own(v_col), nxt(v_col)],
        out_specs=pl.BlockSpec((TR, q_dim), lambda b, t: (qblk(b, t), 0)),
        compiler_params=pltpu.CompilerParams(dimension_semantics=("arbitrary", "arbitrary")),
        name="attention",
    )(sinks, z, z, z, z, z, z, z, z, z)


def _l0_out_kernel(a1_ref, a2_ref, w1_ref, w2_ref, h_ref, gate_ref, o_ref):
    y = jnp.dot(a1_ref[...], w1_ref[...], preferred_element_type=F32)
    y = y + jnp.dot(a2_ref[...], w2_ref[...], preferred_element_type=F32)
    o_ref[...] = h_ref[...] + gate_ref[...] * y


def _l0_out_proj(attn, conv, w_out, h, mods, tile_to_mod):
    r, d = h.shape
    k1 = attn.shape[1]
    k2 = conv.shape[1]
    assert k1 == k2 and w_out.shape[0] == k1 + k2
    tn = TN_OUT
    return pl.pallas_call(
        _l0_out_kernel,
        out_shape=jax.ShapeDtypeStruct((r, d), F32),
        grid=(r // TM, d // tn),
        in_specs=[pl.BlockSpec((TM, k1), lambda i, j: (i, 0)),
                  pl.BlockSpec((TM, k2), lambda i, j: (i, 0)),
                  pl.BlockSpec((k1, tn), lambda i, j: (0, j)),
                  pl.BlockSpec((k2, tn), lambda i, j: (1, j)),
                  pl.BlockSpec((TM, tn), lambda i, j: (i, j)),
                  _mod_spec(2, tn, tile_to_mod, True)],
        out_specs=pl.BlockSpec((TM, tn), lambda i, j: (i, j)),
        compiler_params=pltpu.CompilerParams(
            dimension_semantics=("arbitrary", "arbitrary"),
            vmem_limit_bytes=_vmem_limit(2 * TM * (k1 + k2) * 2 + 2 * (k1 + k2) * tn * 2
                                         + 4 * TM * tn * 4 + TM * tn * 4)),
        name="l0_out_proj",
    )(attn, conv, w_out, w_out, h, mods)


def _ffn_kernel(h_ref, g_ref, sh_ref, sc_ref, gate_ref, wg_ref, wu_ref, wd_ref, o_ref, u_scr):
    j = pl.program_id(1)
    nj = pl.num_programs(1)

    @pl.when(j == 0)
    def _():
        u_scr[...] = _norm_mod(h_ref[...], g_ref[...], sh_ref[...], sc_ref[...]).astype(BF16)

    u = u_scr[...]
    gt = jnp.dot(u, wg_ref[...], preferred_element_type=F32)
    up = jnp.dot(u, wu_ref[...], preferred_element_type=F32)
    act = (gt * jax.nn.sigmoid(gt) * up).astype(BF16)
    y = jnp.dot(act, wd_ref[...], preferred_element_type=F32)

    @pl.when(j == 0)
    def _():
        o_ref[...] = y

    @pl.when(j != 0)
    def _():
        o_ref[...] += y

    @pl.when(j == nj - 1)
    def _():
        o_ref[...] = h_ref[...] + gate_ref[...] * o_ref[...]


def _ffn(h, g, mods, w_gate, w_up, w_down, tile_to_mod):
    r, d = h.shape
    f = w_gate.shape[1]
    tf = TF_FFN
    return pl.pallas_call(
        _ffn_kernel,
        out_shape=jax.ShapeDtypeStruct((r, d), F32),
        grid=(r // TM, f // tf),
        in_specs=[pl.BlockSpec((TM, d), lambda i, j: (i, 0)),
                  pl.BlockSpec((1, d), lambda i, j: (0, 0)),
                  _mod_spec(3, d, tile_to_mod, False),
                  _mod_spec(4, d, tile_to_mod, False),
                  _mod_spec(5, d, tile_to_mod, False),
                  pl.BlockSpec((d, tf), lambda i, j: (0, j)),
                  pl.BlockSpec((d, tf), lambda i, j: (0, j)),
                  pl.BlockSpec((tf, d), lambda i, j: (j, 0))],
        out_specs=pl.BlockSpec((TM, d), lambda i, j: (i, 0)),
        scratch_shapes=[pltpu.VMEM((TM, d), BF16)],
        compiler_params=pltpu.CompilerParams(
            dimension_semantics=("arbitrary", "arbitrary"),
            vmem_limit_bytes=_vmem_limit(4 * TM * d * 4 + 6 * d * tf * 2 + TM * d * 2
                                         + 3 * TM * tf * 4 + TM * d * 4)),
        name="ffn",
    )(h, g.reshape(1, d), mods, mods, mods, w_gate, w_up, w_down)


def _l1_in_kernel(h_ref, g_ref, sh_ref, sc_ref, w_ref, gg_ref, xr_ref, u_scr, *, n_gate_tiles):
    j = pl.program_id(1)

    @pl.when(j == 0)
    def _():
        u_scr[...] = _norm_mod(h_ref[...], g_ref[...], sh_ref[...], sc_ref[...]).astype(BF16)

    z = jnp.dot(u_scr[...], w_ref[...], preferred_element_type=F32)

    @pl.when(j < n_gate_tiles)
    def _():
        gg_ref[...] = jax.nn.gelu(z, approximate=True).astype(gg_ref.dtype)

    @pl.when(j >= n_gate_tiles)
    def _():
        xr_ref[...] = z


def _l1_in_proj(h, g, mods, w_in, d_rnn, tile_to_mod):
    r, d = h.shape
    tn = TN_OUT
    ng = d_rnn // tn
    return pl.pallas_call(
        functools.partial(_l1_in_kernel, n_gate_tiles=ng),
        out_shape=(jax.ShapeDtypeStruct((r, d_rnn), BF16), jax.ShapeDtypeStruct((r, d_rnn), F32)),
        grid=(r // TM, 2 * ng),
        in_specs=[pl.BlockSpec((TM, d), lambda i, j: (i, 0)),
                  pl.BlockSpec((1, d), lambda i, j: (0, 0)),
                  _mod_spec(0, d, tile_to_mod, False),
                  _mod_spec(1, d, tile_to_mod, False),
                  pl.BlockSpec((d, tn), lambda i, j: (0, j))],
        out_specs=(pl.BlockSpec((TM, tn), lambda i, j: (i, jnp.minimum(j, ng - 1))),
                   pl.BlockSpec((TM, tn), lambda i, j: (i, jnp.maximum(j - ng, 0)))),
        scratch_shapes=[pltpu.VMEM((TM, d), BF16)],
        compiler_params=pltpu.CompilerParams(
            dimension_semantics=("arbitrary", "arbitrary"),
            vmem_limit_bytes=_vmem_limit(2 * TM * d * 4 + 2 * d * tn * 2 + 2 * TM * tn * 6
                                         + TM * d * 2 + 2 * TM * tn * 4)),
        name="l1_in_proj",
    )(h, g.reshape(1, d), mods, mods, w_in)


def _rglru_kernel(xf_ref, xb_ref, gaw_ref, gxw_ref, gab_ref, gxb_ref, lam_ref, hf_ref, hb_ref,
                  a_scr, b_scr, h_scr):
    c = pl.program_id(1)
    t_len, d_rnn = xf_ref.shape
    hd = d_rnn // N_RNN_HEADS

    @pl.when(c == 0)
    def _():
        h_scr[...] = jnp.zeros_like(h_scr)

    for z, x_ref in ((0, xf_ref), (1, xb_ref)):
        lam = lam_ref[z]
        neg_softplus_arg = -lam
        sp = jnp.maximum(neg_softplus_arg, 0.0) + jnp.log1p(jnp.exp(-jnp.abs(neg_softplus_arg)))
        for hh in range(N_RNN_HEADS):
            sl = slice(hh * hd, (hh + 1) * hd)
            xh = x_ref[:, sl]
            xh16 = xh.astype(BF16)
            ra = jnp.dot(xh16, gaw_ref[z, hh], preferred_element_type=F32) + gab_ref[z][:, sl]
            ri = jnp.dot(xh16, gxw_ref[z, hh], preferred_element_type=F32) + gxb_ref[z][:, sl]
            log_a = -RG_C * jax.nn.sigmoid(ra) * sp[:, sl]
            a = jnp.exp(log_a)
            a_scr[z, :, sl] = a
            b_scr[z, :, sl] = jnp.sqrt(1.0 - a * a) * jax.nn.sigmoid(ri) * xh

    row8 = lax.broadcasted_iota(jnp.int32, (8, d_rnn), 0)
    n_grp = t_len // 8

    def fwd_group(r0, h):
        a = a_scr[0, pl.ds(r0, 8), :]
        b = b_scr[0, pl.ds(r0, 8), :]
        for s in (1, 2, 4):
            a_sh = jnp.where(row8 >= s, pltpu.roll(a, s, 0), 1.0)
            b_sh = jnp.where(row8 >= s, pltpu.roll(b, s, 0), 0.0)
            b = a * b_sh + b
            a = a * a_sh
        out = a * h + b
        return out, jnp.broadcast_to(out[7:8, :], (8, d_rnn))

    def bwd_group(r0, h):
        a = a_scr[1, pl.ds(r0, 8), :]
        b = b_scr[1, pl.ds(r0, 8), :]
        for s in (1, 2, 4):
            a_sh = jnp.where(row8 < 8 - s, pltpu.roll(a, 8 - s, 0), 1.0)
            b_sh = jnp.where(row8 < 8 - s, pltpu.roll(b, 8 - s, 0), 0.0)
            b = a * b_sh + b
            a = a * a_sh
        out = a * h + b
        return out, jnp.broadcast_to(out[0:1, :], (8, d_rnn))

    def fwd_body(g, h):
        r0 = pl.multiple_of(g * 16, 16)
        lo, h = fwd_group(r0, h)
        hi, h = fwd_group(r0 + 8, h)
        hf_ref[pl.ds(r0, 16), :] = jnp.concatenate([lo, hi], axis=0).astype(hf_ref.dtype)
        return h

    def bwd_body(k, h):
        r0 = pl.multiple_of((n_grp // 2 - 1 - k) * 16, 16)
        hi, h = bwd_group(r0 + 8, h)
        lo, h = bwd_group(r0, h)
        hb_ref[pl.ds(r0, 16), :] = jnp.concatenate([lo, hi], axis=0).astype(hb_ref.dtype)
        return h

    h_scr[0] = lax.fori_loop(0, n_grp // 2, fwd_body, h_scr[0])
    h_scr[1] = lax.fori_loop(0, n_grp // 2, bwd_body, h_scr[1])


def _rglru(xc, ga_w, gx_w, ga_b, gx_b, lam, n_batch, n_ctx_tiles, tiles_per_seg):
    d_rnn = xc.shape[1]
    n_lat = n_batch * tiles_per_seg * TR
    hd = d_rnn // N_RNN_HEADS
    nt = tiles_per_seg

    def fblk(b, c):
        return jnp.where(c == 0, b, n_ctx_tiles + b * nt + c - 1)

    def bblk(b, c):
        return jnp.where(c == 0, b, n_ctx_tiles + b * nt + nt - c)

    def full(shape):
        return pl.BlockSpec(shape, lambda b, c: (0,) * len(shape))

    return pl.pallas_call(
        _rglru_kernel,
        out_shape=(jax.ShapeDtypeStruct((n_lat, d_rnn), BF16), jax.ShapeDtypeStruct((n_lat, d_rnn), BF16)),
        grid=(n_batch, 1 + nt),
        in_specs=[pl.BlockSpec((TR, d_rnn), lambda b, c: (fblk(b, c), 0)),
                  pl.BlockSpec((TR, d_rnn), lambda b, c: (bblk(b, c), 0)),
                  full((2, N_RNN_HEADS, hd, hd)), full((2, N_RNN_HEADS, hd, hd)),
                  full((2, 1, d_rnn)), full((2, 1, d_rnn)), full((2, 1, d_rnn))],
        out_specs=(pl.BlockSpec((TR, d_rnn), lambda b, c: (b * nt + jnp.maximum(c - 1, 0), 0)),
                   pl.BlockSpec((TR, d_rnn), lambda b, c: (b * nt + jnp.where(c == 0, nt - 1, nt - c), 0))),
        scratch_shapes=[pltpu.VMEM((2, TR, d_rnn), F32), pltpu.VMEM((2, TR, d_rnn), F32),
                        pltpu.VMEM((2, 8, d_rnn), F32)],
        compiler_params=pltpu.CompilerParams(
            dimension_semantics=("arbitrary", "arbitrary"),
            vmem_limit_bytes=_vmem_limit(12 * TR * d_rnn * 4 + 8 * N_RNN_HEADS * hd * hd * 2)),
        name="rglru",
    )(xc, xc, ga_w, gx_w, ga_b.reshape(2, 1, d_rnn), gx_b.reshape(2, 1, d_rnn), lam.reshape(2, 1, d_rnn))


def _l1_out_kernel(hf_ref, hb_ref, gg_ref, w_ref, h_ref, gate_ref, o_ref, lhs_scr):
    j = pl.program_id(1)

    @pl.when(j == 0)
    def _():
        rec = hf_ref[...].astype(F32) + hb_ref[...].astype(F32)
        lhs_scr[...] = (rec * gg_ref[...].astype(F32)).astype(BF16)

    y = jnp.dot(lhs_scr[...], w_ref[...], preferred_element_type=F32)
    o_ref[...] = h_ref[...] + gate_ref[...] * y


def _l1_out_proj(hf, hb, gg, w_out, h, mods, n_ctx_mtiles, lat_tile_to_mod):
    n_lat, d_rnn = hf.shape
    d = h.shape[1]
    tn = TN_OUT
    return pl.pallas_call(
        _l1_out_kernel,
        out_shape=jax.ShapeDtypeStruct((n_lat, d), F32),
        grid=(n_lat // TM, d // tn),
        in_specs=[pl.BlockSpec((TM, d_rnn), lambda i, j: (i, 0)),
                  pl.BlockSpec((TM, d_rnn), lambda i, j: (i, 0)),
                  pl.BlockSpec((TM, d_rnn), lambda i, j: (i + n_ctx_mtiles, 0)),
                  pl.BlockSpec((d_rnn, tn), lambda i, j: (0, j)),
                  pl.BlockSpec((TM, tn), lambda i, j: (i + n_ctx_mtiles, j)),
                  _mod_spec(2, tn, lat_tile_to_mod, True)],
        out_specs=pl.BlockSpec((TM, tn), lambda i, j: (i, j)),
        scratch_shapes=[pltpu.VMEM((TM, d_rnn), BF16)],
        compiler_params=pltpu.CompilerParams(
            dimension_semantics=("arbitrary", "arbitrary"),
            vmem_limit_bytes=_vmem_limit(4 * TM * d_rnn * 4 + 2 * TM * d_rnn * 2 + 2 * d_rnn * tn * 2
                                         + 4 * TM * tn * 4 + TM * d_rnn * 2 + TM * d_rnn * 4)),
        name="l1_out_proj",
    )(hf, hb, gg, w_out, h, mods)


def _router_kernel(h_ref, g_ref, sh_ref, sc_ref, rw_ref, rb_ref, u_ref, route_ref):
    u = _norm_mod(h_ref[...], g_ref[...], sh_ref[...], sc_ref[...])
    u_ref[...] = u
    logits = jnp.dot(u.astype(BF16), rw_ref[...], preferred_element_type=F32) + rb_ref[...]
    lane = lax.broadcasted_iota(jnp.int32, logits.shape, 1)
    m1 = jnp.max(logits, axis=-1, keepdims=True)
    i1 = jnp.min(jnp.where(logits == m1, lane, LANES), axis=-1, keepdims=True)
    rest = jnp.where(lane == i1, -jnp.inf, logits)
    m2 = jnp.max(rest, axis=-1, keepdims=True)
    i2 = jnp.min(jnp.where(rest == m2, lane, LANES), axis=-1, keepdims=True)
    e2 = jnp.exp(m2 - m1)
    w1 = 1.0 / (1.0 + e2)
    w2 = e2 * w1
    route = jnp.where(lane == 0, i1.astype(F32),
                      jnp.where(lane == 1, i2.astype(F32),
                                jnp.where(lane == 2, w1, jnp.where(lane == 3, w2, 0.0))))
    route_ref[...] = route


def _router(h, g, mods, rw_pad, rb_pad, lat_tile_to_mod):
    n, d = h.shape

    def mspec(chunk):
        return pl.BlockSpec((None, 1, d), lambda i: (lat_tile_to_mod(i) * N_MOD + chunk, 0, 0))

    return pl.pallas_call(
        _router_kernel,
        out_shape=(jax.ShapeDtypeStruct((n, d), F32), jax.ShapeDtypeStruct((n, LANES), F32)),
        grid=(n // TM,),
        in_specs=[pl.BlockSpec((TM, d), lambda i: (i, 0)),
                  pl.BlockSpec((1, d), lambda i: (0, 0)),
                  mspec(3), mspec(4),
                  pl.BlockSpec((d, LANES), lambda i: (0, 0)),
                  pl.BlockSpec((1, LANES), lambda i: (0, 0))],
        out_specs=(pl.BlockSpec((TM, d), lambda i: (i, 0)),
                   pl.BlockSpec((TM, LANES), lambda i: (i, 0))),
        compiler_params=pltpu.CompilerParams(
            dimension_semantics=("arbitrary",),
            vmem_limit_bytes=_vmem_limit(6 * TM * d * 4)),
        name="router",
    )(h, g.reshape(1, d), mods, mods, rw_pad, rb_pad)


def _row_copy(src, src_row, dst, dst_row, sem):
    return pltpu.make_async_copy(src.at[pl.ds(src_row, 1)], dst.at[pl.ds(dst_row, 1)], sem)


def _moe_kernel(te_ref, tv_ref, tok_ref, tok_next_ref, u_hbm, wg_ref, wu_ref, wd_ref, o_ref,
                xg_scr, x_scr, sems):
    del te_ref
    i = pl.program_id(0)
    j = pl.program_id(1)
    n_tiles = pl.num_programs(0)
    rows = x_scr.shape[0]

    def gather(idx_ref, slot):
        def start(r, carry):
            _row_copy(u_hbm, idx_ref[r], xg_scr.at[slot], r, sems.at[slot]).start()
            return carry
        lax.fori_loop(0, rows, start, 0)

    def gather_wait(slot):
        def wait(r, carry):
            _row_copy(u_hbm, 0, xg_scr.at[slot], 0, sems.at[slot]).wait()
            return carry
        lax.fori_loop(0, rows, wait, 0)

    @pl.when(tv_ref[i] == 1)
    def _():
        @pl.when(j == 0)
        def _():
            slot = i % 2

            @pl.when(i == 0)
            def _():
                gather(tok_ref, 0)

            gather_wait(slot)

            @pl.when((i + 1 < n_tiles) & (tv_ref[jnp.minimum(i + 1, n_tiles - 1)] == 1))
            def _():
                gather(tok_next_ref, 1 - slot)

            x_scr[...] = xg_scr[slot].astype(BF16)

        x = x_scr[...]
        gt = jnp.dot(x, wg_ref[...], preferred_element_type=F32)
        up = jnp.dot(x, wu_ref[...], preferred_element_type=F32)
        act = (gt * jax.nn.sigmoid(gt) * up).astype(BF16)
        y = jnp.dot(act, wd_ref[...], preferred_element_type=F32)

        @pl.when(j == 0)
        def _():
            o_ref[...] = y

        @pl.when(j != 0)
        def _():
            o_ref[...] += y

    @pl.when((tv_ref[i] == 0) & (j == 0))
    def _():
        o_ref[...] = jnp.zeros_like(o_ref)


def _moe(u, slot_tok, tile_e, tile_valid, w_gate, w_up, w_down):
    d = u.shape[1]
    cap = slot_tok.shape[0]
    n_tiles = cap // TM
    f = w_gate.shape[2]
    tf = TF_MOE
    nj = f // tf

    def jj(i, j, tv):
        return jnp.where(tv[i] == 1, j, nj - 1)

    return pl.pallas_call(
        _moe_kernel,
        out_shape=jax.ShapeDtypeStruct((cap, d), F32),
        grid_spec=pltpu.PrefetchScalarGridSpec(
            num_scalar_prefetch=2,
            grid=(n_tiles, nj),
            in_specs=[pl.BlockSpec((TM,), lambda i, j, te, tv: (i,), memory_space=pltpu.SMEM),
                      pl.BlockSpec((TM,), lambda i, j, te, tv: (jnp.minimum(i + 1, n_tiles - 1),),
                                   memory_space=pltpu.SMEM),
                      pl.BlockSpec(memory_space=pl.ANY),
                      pl.BlockSpec((None, d, tf), lambda i, j, te, tv: (te[i], 0, jj(i, j, tv))),
                      pl.BlockSpec((None, d, tf), lambda i, j, te, tv: (te[i], 0, jj(i, j, tv))),
                      pl.BlockSpec((None, tf, d), lambda i, j, te, tv: (te[i], jj(i, j, tv), 0))],
            out_specs=pl.BlockSpec((TM, d), lambda i, j, te, tv: (i, 0)),
            scratch_shapes=[pltpu.VMEM((2, TM, d), F32), pltpu.VMEM((TM, d), BF16),
                            pltpu.SemaphoreType.DMA((2,))]),
        compiler_params=pltpu.CompilerParams(
            dimension_semantics=("arbitrary", "arbitrary"),
            vmem_limit_bytes=_vmem_limit(2 * TM * d * 4 + 2 * TM * d * 4 + 6 * d * tf * 2 + TM * d * 2
                                         + 3 * TM * tf * 4 + TM * d * 4)),
        name="moe_experts",
    )(tile_e, tile_valid, slot_tok, slot_tok, u, w_gate, w_up, w_down)


def _combine_kernel(dest_ref, route_ref, h_ref, gate_ref, fg_ref, yb_hbm, o_ref, y_scr, sem, *, rows):
    def start(r, carry):
        for kk in range(TOP_K):
            _row_copy(yb_hbm, dest_ref[TOP_K * r + kk], y_scr.at[kk], r, sem).start()
        return carry

    lax.fori_loop(0, rows, start, 0)

    def wait(r, carry):
        for kk in range(TOP_K):
            _row_copy(yb_hbm, 0, y_scr.at[kk], 0, sem).wait()
        return carry

    lax.fori_loop(0, rows, wait, 0)

    route = route_ref[...]
    moe = route[:, 2:3] * y_scr[0] + route[:, 3:4] * y_scr[1]
    hl = h_ref[...] + gate_ref[...] * moe
    ms = jnp.mean(hl * hl, axis=-1, keepdims=True)
    o_ref[...] = (hl * lax.rsqrt(ms + NORM_EPS)) * fg_ref[...]


def _combine(yb, dest, route, h, mods, final_g, lat_tile_to_mod):
    n, d = h.shape
    return pl.pallas_call(
        functools.partial(_combine_kernel, rows=TM),
        out_shape=jax.ShapeDtypeStruct((n, d), F32),
        grid=(n // TM,),
        in_specs=[pl.BlockSpec((TOP_K * TM,), lambda i: (i,), memory_space=pltpu.SMEM),
                  pl.BlockSpec((TM, LANES), lambda i: (i, 0)),
                  pl.BlockSpec((TM, d), lambda i: (i, 0)),
                  pl.BlockSpec((None, 1, d), lambda i: (lat_tile_to_mod(i) * N_MOD + 5, 0, 0)),
                  pl.BlockSpec((1, d), lambda i: (0, 0)),
                  pl.BlockSpec(memory_space=pl.ANY)],
        out_specs=pl.BlockSpec((TM, d), lambda i: (i, 0)),
        scratch_shapes=[pltpu.VMEM((TOP_K, TM, d), F32), pltpu.SemaphoreType.DMA],
        compiler_params=pltpu.CompilerParams(
            dimension_semantics=("arbitrary",),
            vmem_limit_bytes=_vmem_limit(TOP_K * TM * d * 4 + 4 * TM * d * 4 + 2 * TM * d * 4)),
        name="moe_combine",
    )(dest, route, h, mods, final_g.reshape(1, d), yb)


def _rope_tables(n_ident, s_len):
    t = jnp.arange(s_len)
    row_id = (t // GRID_W).astype(F32)
    col_id = (t % GRID_W).astype(F32)
    inv_freq = ROPE_BASE ** (-jnp.arange(ROPE_PAIRS, dtype=F32) / ROPE_PAIRS)
    ang_r = row_id[:, None] * inv_freq
    ang_c = col_id[:, None] * inv_freq
    ang = jnp.concatenate([ang_r, ang_r, ang_c, ang_c], axis=-1)
    cos, sin = jnp.cos(ang), jnp.sin(ang)
    first_half = (jnp.arange(HEAD_DIM) % (2 * ROPE_PAIRS)) < ROPE_PAIRS
    sa = jnp.where(first_half, -sin, 0.0)
    sb = jnp.where(first_half, 0.0, sin)
    ones = jnp.ones((n_ident, HEAD_DIM), F32)
    zeros = jnp.zeros((n_ident, HEAD_DIM), F32)
    return (jnp.concatenate([ones, cos]), jnp.concatenate([zeros, sa]), jnp.concatenate([zeros, sb]))


def _routing_plan(route, n_tiles):
    e = route[:, :TOP_K].astype(jnp.int32).reshape(-1)
    onehot = (e[:, None] == jnp.arange(N_EXPERTS, dtype=jnp.int32)[None, :]).astype(jnp.int32)
    csum = jnp.cumsum(onehot, axis=0)
    rank = jnp.sum(csum * onehot, axis=1) - 1
    counts = csum[-1]
    padded = (counts + TM - 1) // TM * TM
    pad_end = jnp.cumsum(padded)
    pad_start = pad_end - padded
    dest = jnp.sum(onehot * pad_start[None, :], axis=1) + rank
    n_valid = pad_end[-1] // TM
    tile_idx = jnp.arange(n_tiles, dtype=jnp.int32)
    tile_valid = (tile_idx < n_valid).astype(jnp.int32)
    tile_row = jnp.minimum(tile_idx, n_valid - 1)
    tile_e = jnp.sum((pad_end[None, :] <= (tile_row * TM)[:, None]).astype(jnp.int32), axis=1)
    tile_e = jnp.minimum(tile_e, N_EXPERTS - 1)
    dest = dest.astype(jnp.int32)
    slot_tok = jnp.zeros((n_tiles * TM,), jnp.int32).at[dest].set(
        jnp.arange(e.shape[0], dtype=jnp.int32) // TOP_K)
    return dest, slot_tok, tile_e.astype(jnp.int32), tile_valid


def _mods(cvecs, w_mod, b_mod):
    d = w_mod.shape[0]
    m = _adaln(cvecs, w_mod, b_mod)[:3]
    return m.reshape(3 * N_MOD, 1, d)


def kernel(x, c, ctx, c_ctx, l0_w_mod, l0_b_mod, l0_norm1_g, l0_w_in, l0_sinks, l0_conv_w, l0_conv_b, l0_w_out, l0_norm2_g, l0_ffn_w_gate, l0_ffn_w_up, l0_ffn_w_down, l1_w_mod, l1_b_mod, l1_norm1_g, l1_w_in, l1_conv_w, l1_conv_b, l1_gate_a_w, l1_gate_a_b, l1_gate_x_w, l1_gate_x_b, l1_lambda, l1_w_out, l1_norm2_g, l1_router_w, l1_router_b, l1_moe_w_gate, l1_moe_w_up, l1_moe_w_down, final_norm_g):
    n_batch, s_len, d = x.shape
    n_ctx = ctx.shape[1]
    assert n_batch == 2 and n_batch * n_ctx == TM and n_ctx == TR
    assert s_len % TM == 0 and s_len % GRID_W == 0 and WINDOW * 2 == TR
    n_ctx_rows = n_batch * n_ctx
    n_lat = n_batch * s_len
    n_ctx_rtiles = n_ctx_rows // TR
    tiles_per_seg = s_len // TR
    mtiles_per_seg = s_len // TM
    d_rnn = l1_w_out.shape[0]
    conv_dim = l0_conv_w.shape[1]
    q_dim = N_Q_HEADS * HEAD_DIM
    kv_dim = N_KV_HEADS * HEAD_DIM

    def tile_to_mod(i):
        return jnp.where(i == 0, n_batch, (i - 1) // mtiles_per_seg)

    def lat_tile_to_mod(i):
        return i // mtiles_per_seg

    def rope_blk(i):
        return jnp.where(i == 0, 0, 1 + (i - 1) % mtiles_per_seg)

    cvecs = jnp.concatenate([c, c_ctx[None, :], jnp.zeros((8 - n_batch - 1, d), F32)], axis=0)
    mods0 = _mods(cvecs, l0_w_mod, l0_b_mod)
    mods1 = _mods(cvecs, l1_w_mod, l1_b_mod)
    h = jnp.concatenate([ctx.reshape(n_ctx_rows, d), x.reshape(n_lat, d)], axis=0)

    rope = _rope_tables(TM, s_len)
    z = _l0_in_proj(h, l0_norm1_g, mods0, l0_w_in.astype(BF16), rope, tile_to_mod, rope_blk)
    x_col = (q_dim + 2 * kv_dim) // CONV_CW
    per = conv_dim // CONV_CW
    conv = _dwconv(z, l0_conv_w, l0_conv_b, SHORT_CONV_OFFSETS, n_ctx_rtiles, tiles_per_seg,
                   (x_col, x_col + per, x_col + 2 * per), BF16)
    attn = _attention(z, l0_sinks, n_batch, n_ctx_rtiles, tiles_per_seg)
    h = _l0_out_proj(attn, conv, l0_w_out.astype(BF16), h, mods0, tile_to_mod)
    h = _ffn(h, l0_norm2_g, mods0, l0_ffn_w_gate.astype(BF16), l0_ffn_w_up.astype(BF16),
             l0_ffn_w_down.astype(BF16), tile_to_mod)

    gg, xr = _l1_in_proj(h, l1_norm1_g, mods1, l1_w_in.astype(BF16), d_rnn, tile_to_mod)
    xc = _dwconv(xr, l1_conv_w, l1_conv_b, RG_CONV_OFFSETS, n_ctx_rtiles, tiles_per_seg, None, F32)
    hf, hb = _rglru(xc, l1_gate_a_w.astype(BF16), l1_gate_x_w.astype(BF16), l1_gate_a_b, l1_gate_x_b,
                    l1_lambda, n_batch, n_ctx_rtiles, tiles_per_seg)
    h_lat = _l1_out_proj(hf, hb, gg, l1_w_out.astype(BF16), h, mods1, n_ctx_rows // TM, lat_tile_to_mod)

    rw_pad = jnp.zeros((d, LANES), BF16).at[:, :N_EXPERTS].set(l1_router_w.astype(BF16))
    rb_pad = jnp.full((1, LANES), NEG_INF, F32).at[0, :N_EXPERTS].set(l1_router_b)
    u, route = _router(h_lat, l1_norm2_g, mods1, rw_pad, rb_pad, lat_tile_to_mod)
    n_tiles = (n_lat * TOP_K) // TM + N_EXPERTS
    dest, slot_tok, tile_e, tile_valid = _routing_plan(route, n_tiles)
    yb = _moe(u, slot_tok, tile_e, tile_valid, l1_moe_w_gate.astype(BF16), l1_moe_w_up.astype(BF16),
              l1_moe_w_down.astype(BF16))
    out = _combine(yb, dest, route, h_lat, mods1, final_norm_g, lat_tile_to_mod)
    return out.reshape(n_batch, s_len, d)
```

```python
import functools

import jax
import jax.numpy as jnp
from jax import lax
from jax.experimental import pallas as pl
from jax.experimental.pallas import tpu as pltpu

F32 = jnp.float32
BF16 = jnp.bfloat16

GRID_W = 64
HEAD_DIM = 128
N_Q_HEADS = 8
N_KV_HEADS = 2
Q_PER_KV = N_Q_HEADS // N_KV_HEADS
WINDOW = 128
ATTN_SCALE = HEAD_DIM ** -0.5
ROPE_BASE = 10000.0
ROPE_PAIRS = HEAD_DIM // 4
SHORT_CONV_OFFSETS = (-1, 0, 1)
RG_CONV_OFFSETS = (-2, -1, 0, 1)
N_RNN_HEADS = 16
RG_C = 8.0
N_EXPERTS = 8
TOP_K = 2
LOG2_E = 1.4426950408889634
NORM_EPS = 1e-6
NEG_INF = -1e30
N_MOD = 6

LANES = 128
TM = 512
TR = 256
HALO = 16
TN_CHUNK = 512
TF_FFN = 512
TF_MOE = 1024
MOE_GATHER_STEPS = 4
CONV_CW = 512
MOD_TN = 1024
VMEM_CAP = 56 * 1024 * 1024


def _vmem_limit(nbytes):
    return int(min(max(nbytes * 5 // 4 + (4 << 20), 32 << 20), VMEM_CAP))


def _sigmoid(x):
    return 0.5 * jnp.tanh(0.5 * x) + 0.5


def _norm_mod(h, g, shift, scale):
    ms = jnp.mean(h * h, axis=-1, keepdims=True)
    y = h * lax.rsqrt(ms + NORM_EPS)
    return (y * g) * (1.0 + scale) + shift


def _adaln_kernel(c_ref, w_ref, b_ref, o_ref):
    c = c_ref[...]
    s = (c * jax.nn.sigmoid(c)).astype(BF16)
    o_ref[...] = jnp.dot(s, w_ref[...].astype(BF16), preferred_element_type=F32) + b_ref[...]


def _adaln(cvecs, w_mod, b_mod):
    d, n = w_mod.shape
    return pl.pallas_call(
        _adaln_kernel,
        out_shape=jax.ShapeDtypeStruct((8, n), F32),
        grid=(n // MOD_TN,),
        in_specs=[pl.BlockSpec((8, d), lambda j: (0, 0)),
                  pl.BlockSpec((d, MOD_TN), lambda j: (0, j)),
                  pl.BlockSpec((1, MOD_TN), lambda j: (0, j))],
        out_specs=pl.BlockSpec((8, MOD_TN), lambda j: (0, j)),
        compiler_params=pltpu.CompilerParams(
            dimension_semantics=("arbitrary",),
            vmem_limit_bytes=_vmem_limit(2 * d * MOD_TN * 4 + d * MOD_TN * 2)),
        name="adaln",
    )(cvecs, w_mod, b_mod.reshape(1, n))


def _mod_spec(chunk, width, tile_to_mod):
    return pl.BlockSpec((None, 1, width), lambda i, j: (tile_to_mod(i) * N_MOD + chunk, 0, 0))


def _mod_spec1(chunk, width, tile_to_mod):
    return pl.BlockSpec((None, 1, width), lambda i: (tile_to_mod(i) * N_MOD + chunk, 0, 0))


def _resident(shape):
    return pl.BlockSpec(shape, lambda i: (0,) * len(shape), pipeline_mode=pl.Buffered(1))


def _l0_in_kernel(ctx_ref, x_ref, g_ref, sh_ref, sc_ref, w_ref, cos_ref, sa_ref, sb_ref, o_ref):
    i = pl.program_id(0)
    h = jnp.where(i == 0, ctx_ref[...], x_ref[...])
    u = _norm_mod(h, g_ref[...], sh_ref[...], sc_ref[...]).astype(BF16)
    cos, sa, sb = cos_ref[...], sa_ref[...], sb_ref[...]
    n_rot = N_Q_HEADS + N_KV_HEADS
    n = o_ref.shape[1]
    for c0 in range(0, n, TN_CHUNK):
        z = jnp.dot(u, w_ref[:, c0:c0 + TN_CHUNK], preferred_element_type=F32)
        if c0 >= n_rot * HEAD_DIM:
            o_ref[:, c0:c0 + TN_CHUNK] = z.astype(BF16)
            continue
        for k in range(TN_CHUNK // HEAD_DIM):
            hh = c0 // HEAD_DIM + k
            t = z[:, k * HEAD_DIM:(k + 1) * HEAD_DIM]
            if hh < n_rot:
                t = (t * cos + pltpu.roll(t, HEAD_DIM - ROPE_PAIRS, 1) * sa
                     + pltpu.roll(t, ROPE_PAIRS, 1) * sb)
            if hh < N_Q_HEADS:
                t = t * ATTN_SCALE
            o_ref[:, hh * HEAD_DIM:(hh + 1) * HEAD_DIM] = t.astype(BF16)


def _l0_in_proj(ctx2, x2, g, mods, w_in, rope, tile_to_mod, rope_blk):
    n_ctx_rows, d = ctx2.shape
    assert n_ctx_rows == TM
    r = n_ctx_rows + x2.shape[0]
    n = w_in.shape[1]
    assert n % TN_CHUNK == 0
    cos, sa, sb = rope
    rope_spec = pl.BlockSpec((TM, HEAD_DIM), lambda i: (rope_blk(i), 0))
    return pl.pallas_call(
        _l0_in_kernel,
        out_shape=jax.ShapeDtypeStruct((r, n), BF16),
        grid=(r // TM,),
        in_specs=[_resident((TM, d)),
                  pl.BlockSpec((TM, d), lambda i: (jnp.maximum(i - 1, 0), 0)),
                  pl.BlockSpec((1, d), lambda i: (0, 0)),
                  _mod_spec1(0, d, tile_to_mod),
                  _mod_spec1(1, d, tile_to_mod),
                  _resident((d, n)),
                  rope_spec, rope_spec, rope_spec],
        out_specs=pl.BlockSpec((TM, n), lambda i: (i, 0)),
        compiler_params=pltpu.CompilerParams(
            dimension_semantics=("arbitrary",),
            vmem_limit_bytes=_vmem_limit(3 * TM * d * 4 + d * n * 2 + 2 * TM * n * 2
                                         + 2 * TM * d * 4 + 4 * TM * TN_CHUNK * 4)),
        name="l0_in_proj",
    )(ctx2, x2, g.reshape(1, d), mods, mods, w_in, cos, sa, sb)


def _dwconv_kernel(*refs, offsets, gated, n_ctx_tiles, tiles_per_seg):
    if gated:
        x_ref, bg_ref, cg_ref, xp_ref, cgp_ref, xn_ref, cgn_ref, w_ref, b_ref, o_ref = refs
    else:
        x_ref, xp_ref, xn_ref, w_ref, b_ref, o_ref = refs
    i = pl.program_id(0)
    li = i - n_ctx_tiles
    is_ctx = i < n_ctx_tiles
    first = is_ctx | (li % tiles_per_seg == 0)
    last = is_ctx | (li % tiles_per_seg == tiles_per_seg - 1)
    x = x_ref[...].astype(F32)
    xp = xp_ref[...].astype(F32)
    xn = xn_ref[...].astype(F32)
    if gated:
        x = x * cg_ref[...].astype(F32)
        xp = xp * cgp_ref[...].astype(F32)
        xn = xn * cgn_ref[...].astype(F32)
    xp = jnp.where(first, 0.0, xp)
    xn = jnp.where(last, 0.0, xn)
    tr, cw = x.shape
    row8 = lax.broadcasted_iota(jnp.int32, (8, cw), 0)
    acc = jnp.broadcast_to(b_ref[...], (tr, cw))
    for k, off in enumerate(offsets):
        wk = w_ref[k:k + 1, :]
        if off == 0:
            y = x
        elif off < 0:
            s = -off
            r = pltpu.roll(x, s, 0)
            rp = pltpu.roll(xp, s, 0)[0:8]
            head = jnp.where(row8 < s, rp, r[0:8])
            y = jnp.concatenate([head, r[8:]], axis=0)
        else:
            r = pltpu.roll(x, tr - off, 0)
            rn = pltpu.roll(xn, HALO - off, 0)[HALO - 8:HALO]
            tail = jnp.where(row8 >= 8 - off, rn, r[tr - 8:])
            y = jnp.concatenate([r[:tr - 8], tail], axis=0)
        acc = acc + wk * y
    if gated:
        acc = acc * bg_ref[...].astype(F32)
    o_ref[...] = acc.astype(o_ref.dtype)


def _dwconv(x, w, b, offsets, n_ctx_tiles, tiles_per_seg, gated_cols, out_dtype):
    r = x.shape[0]
    c = w.shape[1]
    nh = TR // HALO
    n_halo = r // HALO
    gated = gated_cols is not None
    xo, bo, co = gated_cols if gated else (0, 0, 0)

    def main(o):
        return pl.BlockSpec((TR, CONV_CW), lambda i, j: (i, o + j))

    def prev(o):
        return pl.BlockSpec((HALO, CONV_CW), lambda i, j: (jnp.maximum(i * nh - 1, 0), o + j))

    def nxt(o):
        return pl.BlockSpec((HALO, CONV_CW), lambda i, j: (jnp.minimum((i + 1) * nh, n_halo - 1), o + j))

    if gated:
        in_specs = [main(xo), main(bo), main(co), prev(xo), prev(co), nxt(xo), nxt(co)]
        args = [x] * 7
    else:
        in_specs = [main(0), prev(0), nxt(0)]
        args = [x] * 3
    in_specs += [pl.BlockSpec((len(offsets), CONV_CW), lambda i, j: (0, j)),
                 pl.BlockSpec((1, CONV_CW), lambda i, j: (0, j))]
    return pl.pallas_call(
        functools.partial(_dwconv_kernel, offsets=offsets, gated=gated, n_ctx_tiles=n_ctx_tiles,
                          tiles_per_seg=tiles_per_seg),
        out_shape=jax.ShapeDtypeStruct((r, c), out_dtype),
        grid=(r // TR, c // CONV_CW),
        in_specs=in_specs,
        out_specs=pl.BlockSpec((TR, CONV_CW), lambda i, j: (i, j)),
        compiler_params=pltpu.CompilerParams(dimension_semantics=("arbitrary", "arbitrary")),
        name="dwconv_gated" if gated else "dwconv",
    )(*args, w, b.reshape(1, c))


def _attn_kernel(sink_ref, q_ref, kc_ref, vc_ref, kp_ref, ko_ref, kn_ref, vp_ref, vo_ref, vn_ref, o_ref,
                 *, tiles_per_seg):
    t = pl.program_id(1)
    tt = t - 1
    tr = q_ref.shape[0]
    n_ctx = kc_ref.shape[0]
    halo = kp_ref.shape[0]
    n_win = tr + 2 * halo
    qi = lax.broadcasted_iota(jnp.int32, (tr, n_win), 0)
    c = lax.broadcasted_iota(jnp.int32, (tr, n_win), 1)
    ok = (c >= qi) & (c <= qi + 2 * WINDOW)
    ok = ok & ((c >= halo) | (tt > 0)) & ((c < halo + tr) | (tt < tiles_per_seg - 1)) & (t > 0)
    bias = jnp.concatenate([jnp.zeros((tr, n_ctx), F32), jnp.where(ok, 0.0, NEG_INF)], axis=1)
    for hk in range(N_KV_HEADS):
        cs = slice(hk * HEAD_DIM, (hk + 1) * HEAD_DIM)
        k_all = jnp.concatenate([kc_ref[:, cs], kp_ref[:, cs], ko_ref[:, cs], kn_ref[:, cs]], axis=0)
        v_all = jnp.concatenate([vc_ref[:, cs], vp_ref[:, cs], vo_ref[:, cs], vn_ref[:, cs]], axis=0)
        for g in range(Q_PER_KV):
            hq = hk * Q_PER_KV + g
            qs = slice(hq * HEAD_DIM, (hq + 1) * HEAD_DIM)
            s = lax.dot_general(q_ref[:, qs], k_all, (((1,), (1,)), ((), ())),
                                preferred_element_type=F32) + bias
            sink = sink_ref[hq]
            m = jnp.maximum(jnp.max(s, axis=-1, keepdims=True), sink)
            p = jnp.exp(s - m)
            denom = jnp.sum(p, axis=-1, keepdims=True) + jnp.exp(sink - m)
            o = jnp.dot(p.astype(BF16), v_all, preferred_element_type=F32)
            o_ref[:, qs] = (o / denom).astype(o_ref.dtype)


def _attention(z, sinks, n_batch, n_ctx_tiles, tiles_per_seg):
    r = z.shape[0]
    q_dim = N_Q_HEADS * HEAD_DIM
    kv_dim = N_KV_HEADS * HEAD_DIM
    k_col = q_dim // kv_dim
    v_col = k_col + 1
    halo = WINDOW
    per = TR // halo
    n_halo_blk = r // halo

    def qblk(b, t):
        return jnp.where(t == 0, b, n_ctx_tiles + b * tiles_per_seg + t - 1)

    def own(col):
        return pl.BlockSpec((TR, kv_dim), lambda b, t: (qblk(b, t), col))

    def ctx(col):
        return pl.BlockSpec((TR, kv_dim), lambda b, t: (b, col))

    def prev(col):
        return pl.BlockSpec((halo, kv_dim), lambda b, t: (jnp.maximum(qblk(b, t) * per - 1, 0), col))

    def nxt(col):
        return pl.BlockSpec((halo, kv_dim),
                            lambda b, t: (jnp.minimum((qblk(b, t) + 1) * per, n_halo_blk - 1), col))

    return pl.pallas_call(
        functools.partial(_attn_kernel, tiles_per_seg=tiles_per_seg),
        out_shape=jax.ShapeDtypeStruct((r, q_dim), BF16),
        grid=(n_batch, 1 + tiles_per_seg),
        in_specs=[pl.BlockSpec(memory_space=pltpu.SMEM),
                  pl.BlockSpec((TR, q_dim), lambda b, t: (qblk(b, t), 0)),
                  ctx(k_col), ctx(v_col),
                  prev(k_col), own(k_col), nxt(k_col),
                  prev(v_col), own(v_col), nxt(v_col)],
        out_specs=pl.BlockSpec((TR, q_dim), lambda b, t: (qblk(b, t), 0)),
        compiler_params=pltpu.CompilerParams(dimension_semantics=("arbitrary", "arbitrary")),
        name="attention",
    )(sinks, z, z, z, z, z, z, z, z, z)


def _l0_out_kernel(ctx_ref, x_ref, a1_ref, a2_ref, w_ref, gate_ref, o_ref):
    i = pl.program_id(0)
    lhs = jnp.concatenate([a1_ref[...], a2_ref[...]], axis=1)
    d = o_ref.shape[1]
    for c0 in range(0, d, TN_CHUNK):
        cs = slice(c0, c0 + TN_CHUNK)
        y = jnp.dot(lhs, w_ref[:, cs], preferred_element_type=F32)
        h = jnp.where(i == 0, ctx_ref[:, cs], x_ref[:, cs])
        o_ref[:, cs] = h + gate_ref[:, cs] * y


def _l0_out_proj(attn, conv, w_out, ctx2, x2, mods, tile_to_mod):
    n_ctx_rows, d = ctx2.shape
    r = n_ctx_rows + x2.shape[0]
    k1 = attn.shape[1]
    k2 = conv.shape[1]
    assert w_out.shape[0] == k1 + k2 and d % TN_CHUNK == 0
    return pl.pallas_call(
        _l0_out_kernel,
        out_shape=jax.ShapeDtypeStruct((r, d), F32),
        grid=(r // TM,),
        in_specs=[_resident((TM, d)),
                  pl.BlockSpec((TM, d), lambda i: (jnp.maximum(i - 1, 0), 0)),
                  pl.BlockSpec((TM, k1), lambda i: (i, 0)),
                  pl.BlockSpec((TM, k2), lambda i: (i, 0)),
                  _resident((k1 + k2, d)),
                  _mod_spec1(2, d, tile_to_mod)],
        out_specs=pl.BlockSpec((TM, d), lambda i: (i, 0)),
        compiler_params=pltpu.CompilerParams(
            dimension_semantics=("arbitrary",),
            vmem_limit_bytes=_vmem_limit(3 * TM * d * 4 + 3 * TM * (k1 + k2) * 2 + (k1 + k2) * d * 2
                                         + 2 * TM * d * 4 + 4 * TM * TN_CHUNK * 4)),
        name="l0_out_proj",
    )(ctx2, x2, attn, conv, w_out, mods)


def _ffn_kernel(h_ref, g_ref, sh_ref, sc_ref, gate_ref, wg_ref, wu_ref, wd_ref, o_ref, u_scr):
    j = pl.program_id(1)
    nj = pl.num_programs(1)

    @pl.when(j == 0)
    def _():
        u_scr[...] = _norm_mod(h_ref[...], g_ref[...], sh_ref[...], sc_ref[...]).astype(BF16)

    u = u_scr[...]
    gt = jnp.dot(u, wg_ref[...], preferred_element_type=F32)
    up = jnp.dot(u, wu_ref[...], preferred_element_type=F32)
    act = (gt * jax.nn.sigmoid(gt) * up).astype(BF16)
    y = jnp.dot(act, wd_ref[...], preferred_element_type=F32)

    @pl.when(j == 0)
    def _():
        o_ref[...] = y

    @pl.when(j != 0)
    def _():
        o_ref[...] += y

    @pl.when(j == nj - 1)
    def _():
        o_ref[...] = h_ref[...] + gate_ref[...] * o_ref[...]


def _ffn(h, g, mods, w_gate, w_up, w_down, tile_to_mod):
    r, d = h.shape
    f = w_gate.shape[1]
    tf = TF_FFN
    return pl.pallas_call(
        _ffn_kernel,
        out_shape=jax.ShapeDtypeStruct((r, d), F32),
        grid=(r // TM, f // tf),
        in_specs=[pl.BlockSpec((TM, d), lambda i, j: (i, 0)),
                  pl.BlockSpec((1, d), lambda i, j: (0, 0)),
                  _mod_spec(3, d, tile_to_mod),
                  _mod_spec(4, d, tile_to_mod),
                  _mod_spec(5, d, tile_to_mod),
                  pl.BlockSpec((d, tf), lambda i, j: (0, j)),
                  pl.BlockSpec((d, tf), lambda i, j: (0, j)),
                  pl.BlockSpec((tf, d), lambda i, j: (j, 0))],
        out_specs=pl.BlockSpec((TM, d), lambda i, j: (i, 0)),
        scratch_shapes=[pltpu.VMEM((TM, d), BF16)],
        compiler_params=pltpu.CompilerParams(
            dimension_semantics=("arbitrary", "arbitrary"),
            vmem_limit_bytes=_vmem_limit(4 * TM * d * 4 + 6 * d * tf * 2 + TM * d * 2
                                         + 3 * TM * tf * 4 + TM * d * 4)),
        name="ffn",
    )(h, g.reshape(1, d), mods, mods, mods, w_gate, w_up, w_down)


def _l1_in_kernel(h_ref, g_ref, sh_ref, sc_ref, w_ref, gg_ref, xr_ref):
    u = _norm_mod(h_ref[...], g_ref[...], sh_ref[...], sc_ref[...]).astype(BF16)
    d_rnn = gg_ref.shape[1]
    for c0 in range(0, 2 * d_rnn, TN_CHUNK):
        z = jnp.dot(u, w_ref[:, c0:c0 + TN_CHUNK], preferred_element_type=F32)
        if c0 < d_rnn:
            gg_ref[:, c0:c0 + TN_CHUNK] = jax.nn.gelu(z, approximate=True).astype(gg_ref.dtype)
        else:
            xr_ref[:, c0 - d_rnn:c0 - d_rnn + TN_CHUNK] = z


def _l1_in_proj(h, g, mods, w_in, d_rnn, tile_to_mod):
    r, d = h.shape
    assert w_in.shape[1] == 2 * d_rnn and d_rnn % TN_CHUNK == 0
    return pl.pallas_call(
        _l1_in_kernel,
        out_shape=(jax.ShapeDtypeStruct((r, d_rnn), BF16), jax.ShapeDtypeStruct((r, d_rnn), F32)),
        grid=(r // TM,),
        in_specs=[pl.BlockSpec((TM, d), lambda i: (i, 0)),
                  pl.BlockSpec((1, d), lambda i: (0, 0)),
                  _mod_spec1(0, d, tile_to_mod),
                  _mod_spec1(1, d, tile_to_mod),
                  _resident((d, 2 * d_rnn))],
        out_specs=(pl.BlockSpec((TM, d_rnn), lambda i: (i, 0)),
                   pl.BlockSpec((TM, d_rnn), lambda i: (i, 0))),
        compiler_params=pltpu.CompilerParams(
            dimension_semantics=("arbitrary",),
            vmem_limit_bytes=_vmem_limit(2 * TM * d * 4 + d * 2 * d_rnn * 2 + 2 * TM * d_rnn * 6
                                         + 2 * TM * d * 4 + 4 * TM * TN_CHUNK * 4)),
        name="l1_in_proj",
    )(h, g.reshape(1, d), mods, mods, w_in)


def _rglru_kernel(xf_ref, xb_ref, gaw_ref, gxw_ref, gab_ref, gxb_ref, lam_ref, hf_ref, hb_ref,
                  a_scr, b_scr, h_scr):
    c = pl.program_id(1)
    t_len, d_rnn = xf_ref.shape
    hd = d_rnn // N_RNN_HEADS

    @pl.when(c == 0)
    def _():
        h_scr[...] = jnp.zeros_like(h_scr)

    for z, x_ref in ((0, xf_ref), (1, xb_ref)):
        neg_lam = -lam_ref[z]
        sp = jnp.maximum(neg_lam, 0.0) + jnp.log1p(jnp.exp(-jnp.abs(neg_lam)))
        log2_a_per_r = sp * (-RG_C * LOG2_E)
        for hh in range(N_RNN_HEADS):
            sl = slice(hh * hd, (hh + 1) * hd)
            xh = x_ref[:, sl]
            xh16 = xh.astype(BF16)
            ra = jnp.dot(xh16, gaw_ref[z, hh], preferred_element_type=F32) + gab_ref[z][:, sl]
            ri = jnp.dot(xh16, gxw_ref[z, hh], preferred_element_type=F32) + gxb_ref[z][:, sl]
            a = jnp.exp2(_sigmoid(ra) * log2_a_per_r[:, sl])
            om = 1.0 - a * a
            root = jnp.where(om > 0.0, om * lax.rsqrt(om), 0.0)
            a_scr[z, :, sl] = a
            b_scr[z, :, sl] = root * _sigmoid(ri) * xh

    row8 = lax.broadcasted_iota(jnp.int32, (8, d_rnn), 0)
    n_grp = t_len // 8

    def fwd_group(r0, h):
        a = a_scr[0, pl.ds(r0, 8), :]
        b = b_scr[0, pl.ds(r0, 8), :]
        for s in (1, 2, 4):
            a_sh = jnp.where(row8 >= s, pltpu.roll(a, s, 0), 1.0)
            b_sh = jnp.where(row8 >= s, pltpu.roll(b, s, 0), 0.0)
            b = a * b_sh + b
            a = a * a_sh
        out = a * h + b
        return out, jnp.broadcast_to(out[7:8, :], (8, d_rnn))

    def bwd_group(r0, h):
        a = a_scr[1, pl.ds(r0, 8), :]
        b = b_scr[1, pl.ds(r0, 8), :]
        for s in (1, 2, 4):
            a_sh = jnp.where(row8 < 8 - s, pltpu.roll(a, 8 - s, 0), 1.0)
            b_sh = jnp.where(row8 < 8 - s, pltpu.roll(b, 8 - s, 0), 0.0)
            b = a * b_sh + b
            a = a * a_sh
        out = a * h + b
        return out, jnp.broadcast_to(out[0:1, :], (8, d_rnn))

    def fwd_body(g, h):
        r0 = pl.multiple_of(g * 16, 16)
        lo, h = fwd_group(r0, h)
        hi, h = fwd_group(r0 + 8, h)
        hf_ref[pl.ds(r0, 16), :] = jnp.concatenate([lo, hi], axis=0).astype(hf_ref.dtype)
        return h

    def bwd_body(k, h):
        r0 = pl.multiple_of((n_grp // 2 - 1 - k) * 16, 16)
        hi, h = bwd_group(r0 + 8, h)
        lo, h = bwd_group(r0, h)
        hb_ref[pl.ds(r0, 16), :] = jnp.concatenate([lo, hi], axis=0).astype(hb_ref.dtype)
        return h

    h_scr[0] = lax.fori_loop(0, n_grp // 2, fwd_body, h_scr[0])
    h_scr[1] = lax.fori_loop(0, n_grp // 2, bwd_body, h_scr[1])


def _rglru(xc, ga_w, gx_w, ga_b, gx_b, lam, n_batch, n_ctx_tiles, tiles_per_seg):
    d_rnn = xc.shape[1]
    n_lat = n_batch * tiles_per_seg * TR
    hd = d_rnn // N_RNN_HEADS
    nt = tiles_per_seg

    def fblk(b, c):
        return jnp.where(c == 0, b, n_ctx_tiles + b * nt + c - 1)

    def bblk(b, c):
        return jnp.where(c == 0, b, n_ctx_tiles + b * nt + nt - c)

    def full(shape):
        return pl.BlockSpec(shape, lambda b, c: (0,) * len(shape))

    return pl.pallas_call(
        _rglru_kernel,
        out_shape=(jax.ShapeDtypeStruct((n_lat, d_rnn), BF16), jax.ShapeDtypeStruct((n_lat, d_rnn), BF16)),
        grid=(n_batch, 1 + nt),
        in_specs=[pl.BlockSpec((TR, d_rnn), lambda b, c: (fblk(b, c), 0)),
                  pl.BlockSpec((TR, d_rnn), lambda b, c: (bblk(b, c), 0)),
                  full((2, N_RNN_HEADS, hd, hd)), full((2, N_RNN_HEADS, hd, hd)),
                  full((2, 1, d_rnn)), full((2, 1, d_rnn)), full((2, 1, d_rnn))],
        out_specs=(pl.BlockSpec((TR, d_rnn), lambda b, c: (b * nt + jnp.maximum(c - 1, 0), 0)),
                   pl.BlockSpec((TR, d_rnn), lambda b, c: (b * nt + jnp.where(c == 0, nt - 1, nt - c), 0))),
        scratch_shapes=[pltpu.VMEM((2, TR, d_rnn), F32), pltpu.VMEM((2, TR, d_rnn), F32),
                        pltpu.VMEM((2, 8, d_rnn), F32)],
        compiler_params=pltpu.CompilerParams(
            dimension_semantics=("arbitrary", "arbitrary"),
            vmem_limit_bytes=_vmem_limit(12 * TR * d_rnn * 4 + 8 * N_RNN_HEADS * hd * hd * 2)),
        name="rglru",
    )(xc, xc, ga_w, gx_w, ga_b.reshape(2, 1, d_rnn), gx_b.reshape(2, 1, d_rnn), lam.reshape(2, 1, d_rnn))


def _l1_out_kernel(hf_ref, hb_ref, gg_ref, w_ref, h_ref, gate_ref, o_ref):
    rec = hf_ref[...].astype(F32) + hb_ref[...].astype(F32)
    lhs = (rec * gg_ref[...].astype(F32)).astype(BF16)
    d = o_ref.shape[1]
    for c0 in range(0, d, TN_CHUNK):
        cs = slice(c0, c0 + TN_CHUNK)
        y = jnp.dot(lhs, w_ref[:, cs], preferred_element_type=F32)
        o_ref[:, cs] = h_ref[:, cs] + gate_ref[:, cs] * y


def _l1_out_proj(hf, hb, gg, w_out, h, mods, n_ctx_mtiles, lat_tile_to_mod):
    n_lat, d_rnn = hf.shape
    d = h.shape[1]
    assert d % TN_CHUNK == 0
    return pl.pallas_call(
        _l1_out_kernel,
        out_shape=jax.ShapeDtypeStruct((n_lat, d), F32),
        grid=(n_lat // TM,),
        in_specs=[pl.BlockSpec((TM, d_rnn), lambda i: (i, 0)),
                  pl.BlockSpec((TM, d_rnn), lambda i: (i, 0)),
                  pl.BlockSpec((TM, d_rnn), lambda i: (i + n_ctx_mtiles, 0)),
                  _resident((d_rnn, d)),
                  pl.BlockSpec((TM, d), lambda i: (i + n_ctx_mtiles, 0)),
                  _mod_spec1(2, d, lat_tile_to_mod)],
        out_specs=pl.BlockSpec((TM, d), lambda i: (i, 0)),
        compiler_params=pltpu.CompilerParams(
            dimension_semantics=("arbitrary",),
            vmem_limit_bytes=_vmem_limit(6 * TM * d_rnn * 2 + d_rnn * d * 2 + 4 * TM * d * 4
                                         + 2 * TM * d_rnn * 4 + 4 * TM * TN_CHUNK * 4)),
        name="l1_out_proj",
    )(hf, hb, gg, w_out, h, mods)


def _router_kernel(h_ref, g_ref, sh_ref, sc_ref, rw_ref, rb_ref, u_ref, route_ref):
    u = _norm_mod(h_ref[...], g_ref[...], sh_ref[...], sc_ref[...])
    u_ref[...] = u
    logits = jnp.dot(u.astype(BF16), rw_ref[...], preferred_element_type=F32) + rb_ref[...]
    lane = lax.broadcasted_iota(jnp.int32, logits.shape, 1)
    m1 = jnp.max(logits, axis=-1, keepdims=True)
    i1 = jnp.min(jnp.where(logits == m1, lane, LANES), axis=-1, keepdims=True)
    rest = jnp.where(lane == i1, -jnp.inf, logits)
    m2 = jnp.max(rest, axis=-1, keepdims=True)
    i2 = jnp.min(jnp.where(rest == m2, lane, LANES), axis=-1, keepdims=True)
    e2 = jnp.exp(m2 - m1)
    w1 = 1.0 / (1.0 + e2)
    w2 = e2 * w1
    route = jnp.where(lane == 0, i1.astype(F32),
                      jnp.where(lane == 1, i2.astype(F32),
                                jnp.where(lane == 2, w1, jnp.where(lane == 3, w2, 0.0))))
    route_ref[...] = route


def _router(h, g, mods, rw_pad, rb_pad, lat_tile_to_mod):
    n, d = h.shape

    def mspec(chunk):
        return pl.BlockSpec((None, 1, d), lambda i: (lat_tile_to_mod(i) * N_MOD + chunk, 0, 0))

    return pl.pallas_call(
        _router_kernel,
        out_shape=(jax.ShapeDtypeStruct((n, d), F32), jax.ShapeDtypeStruct((n, LANES), F32)),
        grid=(n // TM,),
        in_specs=[pl.BlockSpec((TM, d), lambda i: (i, 0)),
                  pl.BlockSpec((1, d), lambda i: (0, 0)),
                  mspec(3), mspec(4),
                  pl.BlockSpec((d, LANES), lambda i: (0, 0)),
                  pl.BlockSpec((1, LANES), lambda i: (0, 0))],
        out_specs=(pl.BlockSpec((TM, d), lambda i: (i, 0)),
                   pl.BlockSpec((TM, LANES), lambda i: (i, 0))),
        compiler_params=pltpu.CompilerParams(
            dimension_semantics=("arbitrary",),
            vmem_limit_bytes=_vmem_limit(6 * TM * d * 4)),
        name="router",
    )(h, g.reshape(1, d), mods, mods, rw_pad, rb_pad)


def _row_copy(src, src_row, dst, dst_row, sem):
    return pltpu.make_async_copy(src.at[pl.ds(src_row, 1)], dst.at[pl.ds(dst_row, 1)], sem)


def _moe_kernel(te_ref, tv_ref, tok_ref, tok_next_ref, u_hbm, wg_ref, wu_ref, wd_ref, o_ref,
                xg_scr, x_scr, sems):
    del te_ref
    i = pl.program_id(0)
    j = pl.program_id(1)
    n_tiles = pl.num_programs(0)
    rows = x_scr.shape[0]
    per_step = rows // MOE_GATHER_STEPS
    slot = i % 2
    valid = tv_ref[i] == 1
    next_valid = (i + 1 < n_tiles) & (tv_ref[jnp.minimum(i + 1, n_tiles - 1)] == 1)

    @pl.when(valid & (j == 0))
    def _():
        @pl.when(i == 0)
        def _():
            def start(r, carry):
                _row_copy(u_hbm, tok_ref[r], xg_scr.at[0], r, sems.at[0]).start()
                return carry
            lax.fori_loop(0, rows, start, 0)

        def wait(r, carry):
            _row_copy(u_hbm, 0, xg_scr.at[slot], 0, sems.at[slot]).wait()
            return carry
        lax.fori_loop(0, rows, wait, 0)
        x_scr[...] = xg_scr[slot].astype(BF16)

    def step(with_gather):
        if with_gather:
            r0 = j * per_step
            for r in range(per_step):
                _row_copy(u_hbm, tok_next_ref[r0 + r], xg_scr.at[1 - slot], r0 + r, sems.at[1 - slot]).start()
        x = x_scr[...]
        gt = jnp.dot(x, wg_ref[...], preferred_element_type=F32)
        up = jnp.dot(x, wu_ref[...], preferred_element_type=F32)
        act = (gt * jax.nn.sigmoid(gt) * up).astype(BF16)
        y = jnp.dot(act, wd_ref[...], preferred_element_type=F32)

        @pl.when(j == 0)
        def _():
            o_ref[...] = y

        @pl.when(j != 0)
        def _():
            o_ref[...] += y

    gather_now = next_valid & (j < MOE_GATHER_STEPS)

    @pl.when(valid & gather_now)
    def _():
        step(True)

    @pl.when(valid & jnp.logical_not(gather_now))
    def _():
        step(False)

    @pl.when((tv_ref[i] == 0) & (j == 0))
    def _():
        o_ref[...] = jnp.zeros_like(o_ref)


def _moe(u, slot_tok, tile_e, tile_valid, w_gate, w_up, w_down):
    d = u.shape[1]
    cap = slot_tok.shape[0]
    n_tiles = cap // TM
    f = w_gate.shape[2]
    tf = TF_MOE
    nj = f // tf
    assert nj >= MOE_GATHER_STEPS and TM % MOE_GATHER_STEPS == 0

    def jj(i, j, tv):
        return jnp.where(tv[i] == 1, j, nj - 1)

    return pl.pallas_call(
        _moe_kernel,
        out_shape=jax.ShapeDtypeStruct((cap, d), F32),
        grid_spec=pltpu.PrefetchScalarGridSpec(
            num_scalar_prefetch=2,
            grid=(n_tiles, nj),
            in_specs=[pl.BlockSpec((TM,), lambda i, j, te, tv: (i,), memory_space=pltpu.SMEM),
                      pl.BlockSpec((TM,), lambda i, j, te, tv: (jnp.minimum(i + 1, n_tiles - 1),),
                                   memory_space=pltpu.SMEM),
                      pl.BlockSpec(memory_space=pl.ANY),
                      pl.BlockSpec((None, d, tf), lambda i, j, te, tv: (te[i], 0, jj(i, j, tv))),
                      pl.BlockSpec((None, d, tf), lambda i, j, te, tv: (te[i], 0, jj(i, j, tv))),
                      pl.BlockSpec((None, tf, d), lambda i, j, te, tv: (te[i], jj(i, j, tv), 0))],
            out_specs=pl.BlockSpec((TM, d), lambda i, j, te, tv: (i, 0)),
            scratch_shapes=[pltpu.VMEM((2, TM, d), F32), pltpu.VMEM((TM, d), BF16),
                            pltpu.SemaphoreType.DMA((2,))]),
        compiler_params=pltpu.CompilerParams(
            dimension_semantics=("arbitrary", "arbitrary"),
            vmem_limit_bytes=_vmem_limit(2 * TM * d * 4 + 2 * TM * d * 4 + 6 * d * tf * 2 + TM * d * 2
                                         + 3 * TM * tf * 4 + TM * d * 4)),
        name="moe_experts",
    )(tile_e, tile_valid, slot_tok, slot_tok, u, w_gate, w_up, w_down)


def _combine_kernel(dest_ref, route_ref, h_ref, gate_ref, fg_ref, yb_hbm, o_ref, y_scr, sem, *, rows):
    def start(r, carry):
        for kk in range(TOP_K):
            _row_copy(yb_hbm, dest_ref[TOP_K * r + kk], y_scr.at[kk], r, sem).start()
        return carry

    lax.fori_loop(0, rows, start, 0)

    def wait(r, carry):
        for kk in range(TOP_K):
            _row_copy(yb_hbm, 0, y_scr.at[kk], 0, sem).wait()
        return carry

    lax.fori_loop(0, rows, wait, 0)

    route = route_ref[...]
    moe = route[:, 2:3] * y_scr[0] + route[:, 3:4] * y_scr[1]
    hl = h_ref[...] + gate_ref[...] * moe
    ms = jnp.mean(hl * hl, axis=-1, keepdims=True)
    o_ref[...] = (hl * lax.rsqrt(ms + NORM_EPS)) * fg_ref[...]


def _combine(yb, dest, route, h, mods, final_g, lat_tile_to_mod):
    n, d = h.shape
    return pl.pallas_call(
        functools.partial(_combine_kernel, rows=TM),
        out_shape=jax.ShapeDtypeStruct((n, d), F32),
        grid=(n // TM,),
        in_specs=[pl.BlockSpec((TOP_K * TM,), lambda i: (i,), memory_space=pltpu.SMEM),
                  pl.BlockSpec((TM, LANES), lambda i: (i, 0)),
                  pl.BlockSpec((TM, d), lambda i: (i, 0)),
                  pl.BlockSpec((None, 1, d), lambda i: (lat_tile_to_mod(i) * N_MOD + 5, 0, 0)),
                  pl.BlockSpec((1, d), lambda i: (0, 0)),
                  pl.BlockSpec(memory_space=pl.ANY)],
        out_specs=pl.BlockSpec((TM, d), lambda i: (i, 0)),
        scratch_shapes=[pltpu.VMEM((TOP_K, TM, d), F32), pltpu.SemaphoreType.DMA],
        compiler_params=pltpu.CompilerParams(
            dimension_semantics=("arbitrary",),
            vmem_limit_bytes=_vmem_limit(TOP_K * TM * d * 4 + 4 * TM * d * 4 + 2 * TM * d * 4)),
        name="moe_combine",
    )(dest, route, h, mods, final_g.reshape(1, d), yb)


def _rope_tables(n_ident, s_len):
    t = jnp.arange(s_len)
    row_id = (t // GRID_W).astype(F32)
    col_id = (t % GRID_W).astype(F32)
    inv_freq = ROPE_BASE ** (-jnp.arange(ROPE_PAIRS, dtype=F32) / ROPE_PAIRS)
    ang_r = row_id[:, None] * inv_freq
    ang_c = col_id[:, None] * inv_freq
    ang = jnp.concatenate([ang_r, ang_r, ang_c, ang_c], axis=-1)
    cos, sin = jnp.cos(ang), jnp.sin(ang)
    first_half = (jnp.arange(HEAD_DIM) % (2 * ROPE_PAIRS)) < ROPE_PAIRS
    sa = jnp.where(first_half, -sin, 0.0)
    sb = jnp.where(first_half, 0.0, sin)
    ones = jnp.ones((n_ident, HEAD_DIM), F32)
    zeros = jnp.zeros((n_ident, HEAD_DIM), F32)
    return (jnp.concatenate([ones, cos]), jnp.concatenate([zeros, sa]), jnp.concatenate([zeros, sb]))


def _routing_plan(route, n_tiles):
    e = route[:, :TOP_K].astype(jnp.int32).reshape(-1)
    onehot = (e[:, None] == jnp.arange(N_EXPERTS, dtype=jnp.int32)[None, :]).astype(jnp.int32)
    csum = jnp.cumsum(onehot, axis=0)
    rank = jnp.sum(csum * onehot, axis=1) - 1
    counts = csum[-1]
    padded = (counts + TM - 1) // TM * TM
    pad_end = jnp.cumsum(padded)
    pad_start = pad_end - padded
    dest = jnp.sum(onehot * pad_start[None, :], axis=1) + rank
    n_valid = pad_end[-1] // TM
    tile_idx = jnp.arange(n_tiles, dtype=jnp.int32)
    tile_valid = (tile_idx < n_valid).astype(jnp.int32)
    tile_row = jnp.minimum(tile_idx, n_valid - 1)
    tile_e = jnp.sum((pad_end[None, :] <= (tile_row * TM)[:, None]).astype(jnp.int32), axis=1)
    tile_e = jnp.minimum(tile_e, N_EXPERTS - 1)
    dest = dest.astype(jnp.int32)
    slot_tok = jnp.zeros((n_tiles * TM,), jnp.int32).at[dest].set(
        jnp.arange(e.shape[0], dtype=jnp.int32) // TOP_K)
    return dest, slot_tok, tile_e.astype(jnp.int32), tile_valid


def _mods(cvecs, w_mod, b_mod):
    d = w_mod.shape[0]
    m = _adaln(cvecs, w_mod, b_mod)[:3]
    return m.reshape(3 * N_MOD, 1, d)


def kernel(x, c, ctx, c_ctx, l0_w_mod, l0_b_mod, l0_norm1_g, l0_w_in, l0_sinks, l0_conv_w, l0_conv_b, l0_w_out, l0_norm2_g, l0_ffn_w_gate, l0_ffn_w_up, l0_ffn_w_down, l1_w_mod, l1_b_mod, l1_norm1_g, l1_w_in, l1_conv_w, l1_conv_b, l1_gate_a_w, l1_gate_a_b, l1_gate_x_w, l1_gate_x_b, l1_lambda, l1_w_out, l1_norm2_g, l1_router_w, l1_router_b, l1_moe_w_gate, l1_moe_w_up, l1_moe_w_down, final_norm_g):
    n_batch, s_len, d = x.shape
    n_ctx = ctx.shape[1]
    assert n_batch == 2 and n_batch * n_ctx == TM and n_ctx == TR
    assert s_len % TM == 0 and s_len % GRID_W == 0 and WINDOW * 2 == TR
    n_ctx_rows = n_batch * n_ctx
    n_lat = n_batch * s_len
    n_ctx_rtiles = n_ctx_rows // TR
    tiles_per_seg = s_len // TR
    mtiles_per_seg = s_len // TM
    d_rnn = l1_w_out.shape[0]
    conv_dim = l0_conv_w.shape[1]
    q_dim = N_Q_HEADS * HEAD_DIM
    kv_dim = N_KV_HEADS * HEAD_DIM

    def tile_to_mod(i):
        return jnp.where(i == 0, n_batch, (i - 1) // mtiles_per_seg)

    def lat_tile_to_mod(i):
        return i // mtiles_per_seg

    def rope_blk(i):
        return jnp.where(i == 0, 0, 1 + (i - 1) % mtiles_per_seg)

    cvecs = jnp.concatenate([c, c_ctx[None, :], jnp.zeros((8 - n_batch - 1, d), F32)], axis=0)
    mods0 = _mods(cvecs, l0_w_mod, l0_b_mod)
    mods1 = _mods(cvecs, l1_w_mod, l1_b_mod)
    ctx2 = ctx.reshape(n_ctx_rows, d)
    x2 = x.reshape(n_lat, d)

    rope = _rope_tables(TM, s_len)
    z = _l0_in_proj(ctx2, x2, l0_norm1_g, mods0, l0_w_in.astype(BF16), rope, tile_to_mod, rope_blk)
    x_col = (q_dim + 2 * kv_dim) // CONV_CW
    per = conv_dim // CONV_CW
    conv = _dwconv(z, l0_conv_w, l0_conv_b, SHORT_CONV_OFFSETS, n_ctx_rtiles, tiles_per_seg,
                   (x_col, x_col + per, x_col + 2 * per), BF16)
    attn = _attention(z, l0_sinks, n_batch, n_ctx_rtiles, tiles_per_seg)
    h = _l0_out_proj(attn, conv, l0_w_out.astype(BF16), ctx2, x2, mods0, tile_to_mod)
    h = _ffn(h, l0_norm2_g, mods0, l0_ffn_w_gate.astype(BF16), l0_ffn_w_up.astype(BF16),
             l0_ffn_w_down.astype(BF16), tile_to_mod)

    gg, xr = _l1_in_proj(h, l1_norm1_g, mods1, l1_w_in.astype(BF16), d_rnn, tile_to_mod)
    xc = _dwconv(xr, l1_conv_w, l1_conv_b, RG_CONV_OFFSETS, n_ctx_rtiles, tiles_per_seg, None, F32)
    hf, hb = _rglru(xc, l1_gate_a_w.astype(BF16), l1_gate_x_w.astype(BF16), l1_gate_a_b, l1_gate_x_b,
                    l1_lambda, n_batch, n_ctx_rtiles, tiles_per_seg)
    h_lat = _l1_out_proj(hf, hb, gg, l1_w_out.astype(BF16), h, mods1, n_ctx_rows // TM, lat_tile_to_mod)

    rw_pad = jnp.zeros((d, LANES), BF16).at[:, :N_EXPERTS].set(l1_router_w.astype(BF16))
    rb_pad = jnp.full((1, LANES), NEG_INF, F32).at[0, :N_EXPERTS].set(l1_router_b)
    u, route = _router(h_lat, l1_norm2_g, mods1, rw_pad, rb_pad, lat_tile_to_mod)
    n_tiles = (n_lat * TOP_K) // TM + N_EXPERTS
    dest, slot_tok, tile_e, tile_valid = _routing_plan(route, n_tiles)
    yb = _moe(u, slot_tok, tile_e, tile_valid, l1_moe_w_gate.astype(BF16), l1_moe_w_up.astype(BF16),
              l1_moe_w_down.astype(BF16))
    out = _combine(yb, dest, route, h_lat, mods1, final_norm_g, lat_tile_to_mod)
    return out.reshape(n_batch, s_len, d)
```

```python
import functools

import jax
import jax.numpy as jnp
from jax import lax
from jax.experimental import pallas as pl
from jax.experimental.pallas import tpu as pltpu

F32 = jnp.float32
BF16 = jnp.bfloat16

GRID_W = 64
HEAD_DIM = 128
N_Q_HEADS = 8
N_KV_HEADS = 2
Q_PER_KV = N_Q_HEADS // N_KV_HEADS
WINDOW = 128
ATTN_SCALE = HEAD_DIM ** -0.5
ROPE_BASE = 10000.0
ROPE_PAIRS = HEAD_DIM // 4
SHORT_CONV_OFFSETS = (-1, 0, 1)
RG_CONV_OFFSETS = (-2, -1, 0, 1)
N_RNN_HEADS = 16
RG_C = 8.0
N_EXPERTS = 8
TOP_K = 2
LOG2_E = 1.4426950408889634
NORM_EPS = 1e-6
NEG_INF = -1e30
N_MOD = 6

LANES = 128
TM = 512
TM_FFN = 1024
TR = 256
HALO = 16
TN_CHUNK = 512
TF_FFN = 512
TF_MOE = 1024
MOE_GATHER_STEPS = 4
CONV_CW = 512
MOD_TN = 1024
VMEM_CAP = 56 * 1024 * 1024


def _vmem_limit(nbytes):
    return int(min(max(nbytes * 5 // 4 + (4 << 20), 32 << 20), VMEM_CAP))


def _sigmoid(x):
    return 0.5 * jnp.tanh(0.5 * x) + 0.5


def _norm_mod(h, g, shift, scale):
    ms = jnp.mean(h * h, axis=-1, keepdims=True)
    y = h * lax.rsqrt(ms + NORM_EPS)
    return (y * g) * (1.0 + scale) + shift


def _adaln_kernel(c_ref, w_ref, b_ref, o_ref):
    c = c_ref[...]
    s = (c * jax.nn.sigmoid(c)).astype(BF16)
    o_ref[...] = jnp.dot(s, w_ref[...].astype(BF16), preferred_element_type=F32) + b_ref[...]


def _adaln(cvecs, w_mod, b_mod):
    d, n = w_mod.shape
    return pl.pallas_call(
        _adaln_kernel,
        out_shape=jax.ShapeDtypeStruct((8, n), F32),
        grid=(n // MOD_TN,),
        in_specs=[pl.BlockSpec((8, d), lambda j: (0, 0)),
                  pl.BlockSpec((d, MOD_TN), lambda j: (0, j)),
                  pl.BlockSpec((1, MOD_TN), lambda j: (0, j))],
        out_specs=pl.BlockSpec((8, MOD_TN), lambda j: (0, j)),
        compiler_params=pltpu.CompilerParams(
            dimension_semantics=("arbitrary",),
            vmem_limit_bytes=_vmem_limit(2 * d * MOD_TN * 4 + d * MOD_TN * 2)),
        name="adaln",
    )(cvecs, w_mod, b_mod.reshape(1, n))


def _mod_spec(chunk, width, tile_to_mod):
    return pl.BlockSpec((None, 1, width), lambda i, j: (tile_to_mod(i) * N_MOD + chunk, 0, 0))


def _mod_spec1(chunk, width, tile_to_mod):
    return pl.BlockSpec((None, 1, width), lambda i: (tile_to_mod(i) * N_MOD + chunk, 0, 0))


def _resident(shape):
    return pl.BlockSpec(shape, lambda i: (0,) * len(shape), pipeline_mode=pl.Buffered(1))


def _l0_in_kernel(ctx_ref, x_ref, g_ref, sh_ref, sc_ref, w_ref, cos_ref, sa_ref, sb_ref, o_ref):
    i = pl.program_id(0)
    h = jnp.where(i == 0, ctx_ref[...], x_ref[...])
    u = _norm_mod(h, g_ref[...], sh_ref[...], sc_ref[...]).astype(BF16)
    cos, sa, sb = cos_ref[...], sa_ref[...], sb_ref[...]
    n_rot = N_Q_HEADS + N_KV_HEADS
    n = o_ref.shape[1]
    for c0 in range(0, n, TN_CHUNK):
        z = jnp.dot(u, w_ref[:, c0:c0 + TN_CHUNK], preferred_element_type=F32)
        if c0 >= n_rot * HEAD_DIM:
            o_ref[:, c0:c0 + TN_CHUNK] = z.astype(BF16)
            continue
        for k in range(TN_CHUNK // HEAD_DIM):
            hh = c0 // HEAD_DIM + k
            t = z[:, k * HEAD_DIM:(k + 1) * HEAD_DIM]
            if hh < n_rot:
                t = (t * cos + pltpu.roll(t, HEAD_DIM - ROPE_PAIRS, 1) * sa
                     + pltpu.roll(t, ROPE_PAIRS, 1) * sb)
            if hh < N_Q_HEADS:
                t = t * ATTN_SCALE
            o_ref[:, hh * HEAD_DIM:(hh + 1) * HEAD_DIM] = t.astype(BF16)


def _l0_in_proj(ctx2, x2, g, mods, w_in, rope, tile_to_mod, rope_blk):
    n_ctx_rows, d = ctx2.shape
    assert n_ctx_rows == TM
    r = n_ctx_rows + x2.shape[0]
    n = w_in.shape[1]
    assert n % TN_CHUNK == 0
    cos, sa, sb = rope
    rope_spec = pl.BlockSpec((TM, HEAD_DIM), lambda i: (rope_blk(i), 0))
    return pl.pallas_call(
        _l0_in_kernel,
        out_shape=jax.ShapeDtypeStruct((r, n), BF16),
        grid=(r // TM,),
        in_specs=[_resident((TM, d)),
                  pl.BlockSpec((TM, d), lambda i: (jnp.maximum(i - 1, 0), 0)),
                  pl.BlockSpec((1, d), lambda i: (0, 0)),
                  _mod_spec1(0, d, tile_to_mod),
                  _mod_spec1(1, d, tile_to_mod),
                  _resident((d, n)),
                  rope_spec, rope_spec, rope_spec],
        out_specs=pl.BlockSpec((TM, n), lambda i: (i, 0)),
        compiler_params=pltpu.CompilerParams(
            dimension_semantics=("arbitrary",),
            vmem_limit_bytes=_vmem_limit(3 * TM * d * 4 + d * n * 2 + 2 * TM * n * 2
                                         + 2 * TM * d * 4 + 4 * TM * TN_CHUNK * 4)),
        name="l0_in_proj",
    )(ctx2, x2, g.reshape(1, d), mods, mods, w_in, cos, sa, sb)


def _conv_taps(x, xp, xn, first, last, w, b, offsets):
    xp = jnp.where(first, 0.0, xp)
    xn = jnp.where(last, 0.0, xn)
    tr, cw = x.shape
    row8 = lax.broadcasted_iota(jnp.int32, (8, cw), 0)
    acc = jnp.broadcast_to(b, (tr, cw))
    for k, off in enumerate(offsets):
        wk = w[k:k + 1, :]
        if off == 0:
            y = x
        elif off < 0:
            s = -off
            r = pltpu.roll(x, s, 0)
            rp = pltpu.roll(xp, s, 0)[0:8]
            head = jnp.where(row8 < s, rp, r[0:8])
            y = jnp.concatenate([head, r[8:]], axis=0)
        else:
            r = pltpu.roll(x, tr - off, 0)
            rn = pltpu.roll(xn, HALO - off, 0)[HALO - 8:HALO]
            tail = jnp.where(row8 >= 8 - off, rn, r[tr - 8:])
            y = jnp.concatenate([r[:tr - 8], tail], axis=0)
        acc = acc + wk * y
    return acc


def _gated_conv_kernel(x_ref, bg_ref, cg_ref, xp_ref, cgp_ref, xn_ref, cgn_ref, w_ref, b_ref, o_ref,
                       *, n_ctx_tiles, tiles_per_seg):
    i = pl.program_id(0)
    li = i - n_ctx_tiles
    is_ctx = i < n_ctx_tiles
    first = is_ctx | (li % tiles_per_seg == 0)
    last = is_ctx | (li % tiles_per_seg == tiles_per_seg - 1)
    x = x_ref[...].astype(F32) * cg_ref[...].astype(F32)
    xp = xp_ref[...].astype(F32) * cgp_ref[...].astype(F32)
    xn = xn_ref[...].astype(F32) * cgn_ref[...].astype(F32)
    acc = _conv_taps(x, xp, xn, first, last, w_ref[...], b_ref[...], SHORT_CONV_OFFSETS)
    o_ref[...] = (acc * bg_ref[...].astype(F32)).astype(o_ref.dtype)


def _gated_conv(z, w, b, n_ctx_tiles, tiles_per_seg, x_col, bg_col, cg_col):
    r = z.shape[0]
    c = w.shape[1]
    nh = TR // HALO
    n_halo = r // HALO

    def main(o):
        return pl.BlockSpec((TR, CONV_CW), lambda i, j: (i, o + j))

    def prev(o):
        return pl.BlockSpec((HALO, CONV_CW), lambda i, j: (jnp.maximum(i * nh - 1, 0), o + j))

    def nxt(o):
        return pl.BlockSpec((HALO, CONV_CW), lambda i, j: (jnp.minimum((i + 1) * nh, n_halo - 1), o + j))

    return pl.pallas_call(
        functools.partial(_gated_conv_kernel, n_ctx_tiles=n_ctx_tiles, tiles_per_seg=tiles_per_seg),
        out_shape=jax.ShapeDtypeStruct((r, c), BF16),
        grid=(r // TR, c // CONV_CW),
        in_specs=[main(x_col), main(bg_col), main(cg_col), prev(x_col), prev(cg_col), nxt(x_col), nxt(cg_col),
                  pl.BlockSpec((len(SHORT_CONV_OFFSETS), CONV_CW), lambda i, j: (0, j)),
                  pl.BlockSpec((1, CONV_CW), lambda i, j: (0, j))],
        out_specs=pl.BlockSpec((TR, CONV_CW), lambda i, j: (i, j)),
        compiler_params=pltpu.CompilerParams(dimension_semantics=("arbitrary", "arbitrary")),
        name="gated_conv",
    )(z, z, z, z, z, z, z, w, b.reshape(1, c))


def _attn_kernel(sink_ref, q_ref, kc_ref, vc_ref, kp_ref, ko_ref, kn_ref, vp_ref, vo_ref, vn_ref, o_ref,
                 *, tiles_per_seg):
    t = pl.program_id(1)
    tt = t - 1
    tr = q_ref.shape[0]
    n_ctx = kc_ref.shape[0]
    halo = kp_ref.shape[0]
    n_win = tr + 2 * halo
    qi = lax.broadcasted_iota(jnp.int32, (tr, n_win), 0)
    c = lax.broadcasted_iota(jnp.int32, (tr, n_win), 1)
    ok = (c >= qi) & (c <= qi + 2 * WINDOW)
    ok = ok & ((c >= halo) | (tt > 0)) & ((c < halo + tr) | (tt < tiles_per_seg - 1)) & (t > 0)
    bias = jnp.concatenate([jnp.zeros((tr, n_ctx), F32), jnp.where(ok, 0.0, NEG_INF)], axis=1)
    for hk in range(N_KV_HEADS):
        cs = slice(hk * HEAD_DIM, (hk + 1) * HEAD_DIM)
        k_all = jnp.concatenate([kc_ref[:, cs], kp_ref[:, cs], ko_ref[:, cs], kn_ref[:, cs]], axis=0)
        v_all = jnp.concatenate([vc_ref[:, cs], vp_ref[:, cs], vo_ref[:, cs], vn_ref[:, cs]], axis=0)
        for g in range(Q_PER_KV):
            hq = hk * Q_PER_KV + g
            qs = slice(hq * HEAD_DIM, (hq + 1) * HEAD_DIM)
            s = lax.dot_general(q_ref[:, qs], k_all, (((1,), (1,)), ((), ())),
                                preferred_element_type=F32) + bias
            sink = sink_ref[hq]
            m = jnp.maximum(jnp.max(s, axis=-1, keepdims=True), sink)
            p = jnp.exp(s - m)
            denom = jnp.sum(p, axis=-1, keepdims=True) + jnp.exp(sink - m)
            o = jnp.dot(p.astype(BF16), v_all, preferred_element_type=F32)
            o_ref[:, qs] = (o / denom).astype(o_ref.dtype)


def _attention(z, sinks, n_batch, n_ctx_tiles, tiles_per_seg):
    r = z.shape[0]
    q_dim = N_Q_HEADS * HEAD_DIM
    kv_dim = N_KV_HEADS * HEAD_DIM
    k_col = q_dim // kv_dim
    v_col = k_col + 1
    halo = WINDOW
    per = TR // halo
    n_halo_blk = r // halo

    def qblk(b, t):
        return jnp.where(t == 0, b, n_ctx_tiles + b * tiles_per_seg + t - 1)

    def own(col):
        return pl.BlockSpec((TR, kv_dim), lambda b, t: (qblk(b, t), col))

    def ctx(col):
        return pl.BlockSpec((TR, kv_dim), lambda b, t: (b, col))

    def prev(col):
        return pl.BlockSpec((halo, kv_dim), lambda b, t: (jnp.maximum(qblk(b, t) * per - 1, 0), col))

    def nxt(col):
        return pl.BlockSpec((halo, kv_dim),
                            lambda b, t: (jnp.minimum((qblk(b, t) + 1) * per, n_halo_blk - 1), col))

    return pl.pallas_call(
        functools.partial(_attn_kernel, tiles_per_seg=tiles_per_seg),
        out_shape=jax.ShapeDtypeStruct((r, q_dim), BF16),
        grid=(n_batch, 1 + tiles_per_seg),
        in_specs=[pl.BlockSpec(memory_space=pltpu.SMEM),
                  pl.BlockSpec((TR, q_dim), lambda b, t: (qblk(b, t), 0)),
                  ctx(k_col), ctx(v_col),
                  prev(k_col), own(k_col), nxt(k_col),
                  prev(v_col), own(v_col), nxt(v_col)],
        out_specs=pl.BlockSpec((TR, q_dim), lambda b, t: (qblk(b, t), 0)),
        compiler_params=pltpu.CompilerParams(dimension_semantics=("arbitrary", "arbitrary")),
        name="attention",
    )(sinks, z, z, z, z, z, z, z, z, z)


def _l0_out_kernel(ctx_ref, x_ref, a1_ref, a2_ref, w_ref, gate_ref, oc_ref, ol_ref):
    i = pl.program_id(0)
    lhs = jnp.concatenate([a1_ref[...], a2_ref[...]], axis=1)
    d = ol_ref.shape[1]
    for c0 in range(0, d, TN_CHUNK):
        cs = slice(c0, c0 + TN_CHUNK)
        y = jnp.dot(lhs, w_ref[:, cs], preferred_element_type=F32)
        h = jnp.where(i == 0, ctx_ref[:, cs], x_ref[:, cs])
        res = h + gate_ref[:, cs] * y
        oc_ref[:, cs] = res
        ol_ref[:, cs] = res


def _l0_out_proj(attn, conv, w_out, ctx2, x2, mods, tile_to_mod):
    n_ctx_rows, d = ctx2.shape
    n_lat = x2.shape[0]
    r = n_ctx_rows + n_lat
    k1 = attn.shape[1]
    k2 = conv.shape[1]
    assert w_out.shape[0] == k1 + k2 and d % TN_CHUNK == 0
    return pl.pallas_call(
        _l0_out_kernel,
        out_shape=(jax.ShapeDtypeStruct((2, TM, d), F32), jax.ShapeDtypeStruct((n_lat, d), F32)),
        grid=(r // TM,),
        in_specs=[_resident((TM, d)),
                  pl.BlockSpec((TM, d), lambda i: (jnp.maximum(i - 1, 0), 0)),
                  pl.BlockSpec((TM, k1), lambda i: (i, 0)),
                  pl.BlockSpec((TM, k2), lambda i: (i, 0)),
                  _resident((k1 + k2, d)),
                  _mod_spec1(2, d, tile_to_mod)],
        out_specs=(pl.BlockSpec((None, TM, d), lambda i: (jnp.minimum(i, 1), 0, 0)),
                   pl.BlockSpec((TM, d), lambda i: (jnp.maximum(i - 1, 0), 0))),
        compiler_params=pltpu.CompilerParams(
            dimension_semantics=("arbitrary",),
            vmem_limit_bytes=_vmem_limit(3 * TM * d * 4 + 3 * TM * (k1 + k2) * 2 + (k1 + k2) * d * 2
                                         + 4 * TM * d * 4 + 4 * TM * TN_CHUNK * 4)),
        name="l0_out_proj",
    )(ctx2, x2, attn, conv, w_out, mods)


def _ffn_kernel(h_ref, g_ref, sh_ref, sc_ref, gate_ref, wg_ref, wu_ref, wd_ref, o_ref, u_scr):
    j = pl.program_id(1)
    nj = pl.num_programs(1)

    @pl.when(j == 0)
    def _():
        u_scr[...] = _norm_mod(h_ref[...], g_ref[...], sh_ref[...], sc_ref[...]).astype(BF16)

    u = u_scr[...]
    gt = jnp.dot(u, wg_ref[...], preferred_element_type=F32)
    up = jnp.dot(u, wu_ref[...], preferred_element_type=F32)
    act = (gt * jax.nn.sigmoid(gt) * up).astype(BF16)
    y = jnp.dot(act, wd_ref[...], preferred_element_type=F32)

    @pl.when(j == 0)
    def _():
        o_ref[...] = y

    @pl.when(j != 0)
    def _():
        o_ref[...] += y

    @pl.when(j == nj - 1)
    def _():
        o_ref[...] = h_ref[...] + gate_ref[...] * o_ref[...]


def _ffn(h, h_spec, rows, tm, g, mods, w_gate, w_up, w_down, tile_to_mod):
    d = h.shape[-1]
    f = w_gate.shape[1]
    tf = TF_FFN
    return pl.pallas_call(
        _ffn_kernel,
        out_shape=jax.ShapeDtypeStruct((rows, d), F32),
        grid=(rows // tm, f // tf),
        in_specs=[h_spec,
                  pl.BlockSpec((1, d), lambda i, j: (0, 0)),
                  _mod_spec(3, d, tile_to_mod),
                  _mod_spec(4, d, tile_to_mod),
                  _mod_spec(5, d, tile_to_mod),
                  pl.BlockSpec((d, tf), lambda i, j: (0, j)),
                  pl.BlockSpec((d, tf), lambda i, j: (0, j)),
                  pl.BlockSpec((tf, d), lambda i, j: (j, 0))],
        out_specs=pl.BlockSpec((tm, d), lambda i, j: (i, 0)),
        scratch_shapes=[pltpu.VMEM((tm, d), BF16)],
        compiler_params=pltpu.CompilerParams(
            dimension_semantics=("arbitrary", "arbitrary"),
            vmem_limit_bytes=_vmem_limit(3 * tm * d * 4 + 6 * d * tf * 2 + tm * d * 2
                                         + 3 * tm * tf * 4 + tm * tf * 2)),
        name="ffn",
    )(h, g.reshape(1, d), mods, mods, mods, w_gate, w_up, w_down)


def _l1_in_kernel(hc_ref, hl_ref, g_ref, sh_ref, sc_ref, w_ref, gg_ref, xr_ref):
    h = jnp.where(pl.program_id(0) == 0, hc_ref[...], hl_ref[...])
    u = _norm_mod(h, g_ref[...], sh_ref[...], sc_ref[...]).astype(BF16)
    d_rnn = gg_ref.shape[1]
    for c0 in range(0, 2 * d_rnn, TN_CHUNK):
        z = jnp.dot(u, w_ref[:, c0:c0 + TN_CHUNK], preferred_element_type=F32)
        if c0 < d_rnn:
            gg_ref[:, c0:c0 + TN_CHUNK] = jax.nn.gelu(z, approximate=True).astype(gg_ref.dtype)
        else:
            xr_ref[:, c0 - d_rnn:c0 - d_rnn + TN_CHUNK] = z


def _l1_in_proj(h_ctx, h_lat, g, mods, w_in, d_rnn, tile_to_mod):
    n_ctx_rows, d = h_ctx.shape
    assert n_ctx_rows == TM
    r = n_ctx_rows + h_lat.shape[0]
    assert w_in.shape[1] == 2 * d_rnn and d_rnn % TN_CHUNK == 0
    return pl.pallas_call(
        _l1_in_kernel,
        out_shape=(jax.ShapeDtypeStruct((r, d_rnn), BF16), jax.ShapeDtypeStruct((r, d_rnn), F32)),
        grid=(r // TM,),
        in_specs=[_resident((TM, d)),
                  pl.BlockSpec((TM, d), lambda i: (jnp.maximum(i - 1, 0), 0)),
                  pl.BlockSpec((1, d), lambda i: (0, 0)),
                  _mod_spec1(0, d, tile_to_mod),
                  _mod_spec1(1, d, tile_to_mod),
                  _resident((d, 2 * d_rnn))],
        out_specs=(pl.BlockSpec((TM, d_rnn), lambda i: (i, 0)),
                   pl.BlockSpec((TM, d_rnn), lambda i: (i, 0))),
        compiler_params=pltpu.CompilerParams(
            dimension_semantics=("arbitrary",),
            vmem_limit_bytes=_vmem_limit(2 * TM * d * 4 + d * 2 * d_rnn * 2 + 2 * TM * d_rnn * 6
                                         + 2 * TM * d * 4 + 4 * TM * TN_CHUNK * 4)),
        name="l1_in_proj",
    )(h_ctx, h_lat, g.reshape(1, d), mods, mods, w_in)


def _rglru_kernel(xf_ref, xfp_ref, xfn_ref, xb_ref, xbp_ref, xbn_ref, cw_ref, cb_ref,
                  gaw_ref, gxw_ref, gab_ref, gxb_ref, lam_ref, hf_ref, hb_ref,
                  xc_scr, a_scr, b_scr, h_scr, *, tiles_per_seg):
    c = pl.program_id(1)
    t_len, d_rnn = xf_ref.shape
    hd = d_rnn // N_RNN_HEADS

    @pl.when(c == 0)
    def _():
        h_scr[...] = jnp.zeros_like(h_scr)

    is_ctx = c == 0
    chunk = (c - 1, tiles_per_seg - c)
    halos = ((xfp_ref, xfn_ref), (xbp_ref, xbn_ref))

    for z, x_ref in ((0, xf_ref), (1, xb_ref)):
        first = is_ctx | (chunk[z] == 0)
        last = is_ctx | (chunk[z] == tiles_per_seg - 1)
        xc_scr[...] = _conv_taps(x_ref[...], halos[z][0][...], halos[z][1][...], first, last,
                                 cw_ref[...], cb_ref[...], RG_CONV_OFFSETS)
        x_ref = xc_scr
        neg_lam = -lam_ref[z]
        sp = jnp.maximum(neg_lam, 0.0) + jnp.log1p(jnp.exp(-jnp.abs(neg_lam)))
        log2_a_per_r = sp * (-RG_C * LOG2_E)
        for hh in range(N_RNN_HEADS):
            sl = slice(hh * hd, (hh + 1) * hd)
            xh = x_ref[:, sl]
            xh16 = xh.astype(BF16)
            ra = jnp.dot(xh16, gaw_ref[z, hh], preferred_element_type=F32) + gab_ref[z][:, sl]
            ri = jnp.dot(xh16, gxw_ref[z, hh], preferred_element_type=F32) + gxb_ref[z][:, sl]
            a = jnp.exp2(_sigmoid(ra) * log2_a_per_r[:, sl])
            om = 1.0 - a * a
            root = jnp.where(om > 0.0, om * lax.rsqrt(om), 0.0)
            a_scr[z, :, sl] = a
            b_scr[z, :, sl] = root * _sigmoid(ri) * xh

    row8 = lax.broadcasted_iota(jnp.int32, (8, d_rnn), 0)
    n_grp = t_len // 8

    def fwd_group(r0, h):
        a = a_scr[0, pl.ds(r0, 8), :]
        b = b_scr[0, pl.ds(r0, 8), :]
        for s in (1, 2, 4):
            a_sh = jnp.where(row8 >= s, pltpu.roll(a, s, 0), 1.0)
            b_sh = jnp.where(row8 >= s, pltpu.roll(b, s, 0), 0.0)
            b = a * b_sh + b
            a = a * a_sh
        out = a * h + b
        return out, jnp.broadcast_to(out[7:8, :], (8, d_rnn))

    def bwd_group(r0, h):
        a = a_scr[1, pl.ds(r0, 8), :]
        b = b_scr[1, pl.ds(r0, 8), :]
        for s in (1, 2, 4):
            a_sh = jnp.where(row8 < 8 - s, pltpu.roll(a, 8 - s, 0), 1.0)
            b_sh = jnp.where(row8 < 8 - s, pltpu.roll(b, 8 - s, 0), 0.0)
            b = a * b_sh + b
            a = a * a_sh
        out = a * h + b
        return out, jnp.broadcast_to(out[0:1, :], (8, d_rnn))

    def fwd_body(g, h):
        r0 = pl.multiple_of(g * 16, 16)
        lo, h = fwd_group(r0, h)
        hi, h = fwd_group(r0 + 8, h)
        hf_ref[pl.ds(r0, 16), :] = jnp.concatenate([lo, hi], axis=0).astype(hf_ref.dtype)
        return h

    def bwd_body(k, h):
        r0 = pl.multiple_of((n_grp // 2 - 1 - k) * 16, 16)
        hi, h = bwd_group(r0 + 8, h)
        lo, h = bwd_group(r0, h)
        hb_ref[pl.ds(r0, 16), :] = jnp.concatenate([lo, hi], axis=0).astype(hb_ref.dtype)
        return h

    h_scr[0] = lax.fori_loop(0, n_grp // 2, fwd_body, h_scr[0])
    h_scr[1] = lax.fori_loop(0, n_grp // 2, bwd_body, h_scr[1])


def _rglru(xr, conv_w, conv_b, ga_w, gx_w, ga_b, gx_b, lam, n_batch, n_ctx_tiles, tiles_per_seg):
    r, d_rnn = xr.shape
    n_lat = n_batch * tiles_per_seg * TR
    hd = d_rnn // N_RNN_HEADS
    nt = tiles_per_seg
    nh = TR // HALO
    n_halo = r // HALO
    n_taps = conv_w.shape[0]

    def fblk(b, c):
        return jnp.where(c == 0, b, n_ctx_tiles + b * nt + c - 1)

    def bblk(b, c):
        return jnp.where(c == 0, b, n_ctx_tiles + b * nt + nt - c)

    def chunk_specs(blk):
        return [pl.BlockSpec((TR, d_rnn), lambda b, c: (blk(b, c), 0)),
                pl.BlockSpec((HALO, d_rnn), lambda b, c: (jnp.maximum(blk(b, c) * nh - 1, 0), 0)),
                pl.BlockSpec((HALO, d_rnn), lambda b, c: (jnp.minimum((blk(b, c) + 1) * nh, n_halo - 1), 0))]

    def full(shape):
        return pl.BlockSpec(shape, lambda b, c: (0,) * len(shape))

    return pl.pallas_call(
        functools.partial(_rglru_kernel, tiles_per_seg=nt),
        out_shape=(jax.ShapeDtypeStruct((n_lat, d_rnn), BF16), jax.ShapeDtypeStruct((n_lat, d_rnn), BF16)),
        grid=(n_batch, 1 + nt),
        in_specs=chunk_specs(fblk) + chunk_specs(bblk) + [
            full((n_taps, d_rnn)), full((1, d_rnn)),
            full((2, N_RNN_HEADS, hd, hd)), full((2, N_RNN_HEADS, hd, hd)),
            full((2, 1, d_rnn)), full((2, 1, d_rnn)), full((2, 1, d_rnn))],
        out_specs=(pl.BlockSpec((TR, d_rnn), lambda b, c: (b * nt + jnp.maximum(c - 1, 0), 0)),
                   pl.BlockSpec((TR, d_rnn), lambda b, c: (b * nt + jnp.where(c == 0, nt - 1, nt - c), 0))),
        scratch_shapes=[pltpu.VMEM((TR, d_rnn), F32),
                        pltpu.VMEM((2, TR, d_rnn), F32), pltpu.VMEM((2, TR, d_rnn), F32),
                        pltpu.VMEM((2, 8, d_rnn), F32)],
        compiler_params=pltpu.CompilerParams(
            dimension_semantics=("arbitrary", "arbitrary"),
            vmem_limit_bytes=_vmem_limit(14 * TR * d_rnn * 4 + 8 * N_RNN_HEADS * hd * hd * 2)),
        name="rglru",
    )(xr, xr, xr, xr, xr, xr, conv_w, conv_b.reshape(1, d_rnn), ga_w, gx_w,
      ga_b.reshape(2, 1, d_rnn), gx_b.reshape(2, 1, d_rnn), lam.reshape(2, 1, d_rnn))


def _l1_out_kernel(hf_ref, hb_ref, gg_ref, w_ref, h_ref, gate_ref, o_ref):
    rec = hf_ref[...].astype(F32) + hb_ref[...].astype(F32)
    lhs = (rec * gg_ref[...].astype(F32)).astype(BF16)
    d = o_ref.shape[1]
    for c0 in range(0, d, TN_CHUNK):
        cs = slice(c0, c0 + TN_CHUNK)
        y = jnp.dot(lhs, w_ref[:, cs], preferred_element_type=F32)
        o_ref[:, cs] = h_ref[:, cs] + gate_ref[:, cs] * y


def _l1_out_proj(hf, hb, gg, w_out, h, mods, n_ctx_mtiles, lat_tile_to_mod):
    n_lat, d_rnn = hf.shape
    d = h.shape[1]
    assert d % TN_CHUNK == 0
    return pl.pallas_call(
        _l1_out_kernel,
        out_shape=jax.ShapeDtypeStruct((n_lat, d), F32),
        grid=(n_lat // TM,),
        in_specs=[pl.BlockSpec((TM, d_rnn), lambda i: (i, 0)),
                  pl.BlockSpec((TM, d_rnn), lambda i: (i, 0)),
                  pl.BlockSpec((TM, d_rnn), lambda i: (i + n_ctx_mtiles, 0)),
                  _resident((d_rnn, d)),
                  pl.BlockSpec((TM, d), lambda i: (i, 0)),
                  _mod_spec1(2, d, lat_tile_to_mod)],
        out_specs=pl.BlockSpec((TM, d), lambda i: (i, 0)),
        compiler_params=pltpu.CompilerParams(
            dimension_semantics=("arbitrary",),
            vmem_limit_bytes=_vmem_limit(6 * TM * d_rnn * 2 + d_rnn * d * 2 + 4 * TM * d * 4
                                         + 2 * TM * d_rnn * 4 + 4 * TM * TN_CHUNK * 4)),
        name="l1_out_proj",
    )(hf, hb, gg, w_out, h, mods)


def _router_kernel(h_ref, g_ref, sh_ref, sc_ref, rw_ref, rb_ref, u_ref, route_ref):
    u = _norm_mod(h_ref[...], g_ref[...], sh_ref[...], sc_ref[...])
    u_ref[...] = u
    logits = jnp.dot(u.astype(BF16), rw_ref[...], preferred_element_type=F32) + rb_ref[...]
    lane = lax.broadcasted_iota(jnp.int32, logits.shape, 1)
    m1 = jnp.max(logits, axis=-1, keepdims=True)
    i1 = jnp.min(jnp.where(logits == m1, lane, LANES), axis=-1, keepdims=True)
    rest = jnp.where(lane == i1, -jnp.inf, logits)
    m2 = jnp.max(rest, axis=-1, keepdims=True)
    i2 = jnp.min(jnp.where(rest == m2, lane, LANES), axis=-1, keepdims=True)
    e2 = jnp.exp(m2 - m1)
    w1 = 1.0 / (1.0 + e2)
    w2 = e2 * w1
    route = jnp.where(lane == 0, i1.astype(F32),
                      jnp.where(lane == 1, i2.astype(F32),
                                jnp.where(lane == 2, w1, jnp.where(lane == 3, w2, 0.0))))
    route_ref[...] = route


def _router(h, g, mods, rw_pad, rb_pad, lat_tile_to_mod):
    n, d = h.shape

    def mspec(chunk):
        return pl.BlockSpec((None, 1, d), lambda i: (lat_tile_to_mod(i) * N_MOD + chunk, 0, 0))

    return pl.pallas_call(
        _router_kernel,
        out_shape=(jax.ShapeDtypeStruct((n, d), F32), jax.ShapeDtypeStruct((n, LANES), F32)),
        grid=(n // TM,),
        in_specs=[pl.BlockSpec((TM, d), lambda i: (i, 0)),
                  pl.BlockSpec((1, d), lambda i: (0, 0)),
                  mspec(3), mspec(4),
                  pl.BlockSpec((d, LANES), lambda i: (0, 0)),
                  pl.BlockSpec((1, LANES), lambda i: (0, 0))],
        out_specs=(pl.BlockSpec((TM, d), lambda i: (i, 0)),
                   pl.BlockSpec((TM, LANES), lambda i: (i, 0))),
        compiler_params=pltpu.CompilerParams(
            dimension_semantics=("arbitrary",),
            vmem_limit_bytes=_vmem_limit(6 * TM * d * 4)),
        name="router",
    )(h, g.reshape(1, d), mods, mods, rw_pad, rb_pad)


def _row_copy(src, src_row, dst, dst_row, sem):
    return pltpu.make_async_copy(src.at[pl.ds(src_row, 1)], dst.at[pl.ds(dst_row, 1)], sem)


def _moe_kernel(te_ref, tv_ref, tok_ref, tok_next_ref, u_hbm, wg_ref, wu_ref, wd_ref, o_ref,
                xg_scr, x_scr, sems):
    del te_ref
    i = pl.program_id(0)
    j = pl.program_id(1)
    n_tiles = pl.num_programs(0)
    rows = x_scr.shape[0]
    per_step = rows // MOE_GATHER_STEPS
    slot = i % 2
    valid = tv_ref[i] == 1
    next_valid = (i + 1 < n_tiles) & (tv_ref[jnp.minimum(i + 1, n_tiles - 1)] == 1)

    @pl.when(valid & (j == 0))
    def _():
        @pl.when(i == 0)
        def _():
            def start(r, carry):
                _row_copy(u_hbm, tok_ref[r], xg_scr.at[0], r, sems.at[0]).start()
                return carry
            lax.fori_loop(0, rows, start, 0)

        pltpu.make_async_copy(u_hbm.at[pl.ds(0, rows)], xg_scr.at[slot], sems.at[slot]).wait()
        x_scr[...] = xg_scr[slot].astype(BF16)

    def step(with_gather):
        if with_gather:
            r0 = j * per_step
            for r in range(per_step):
                _row_copy(u_hbm, tok_next_ref[r0 + r], xg_scr.at[1 - slot], r0 + r, sems.at[1 - slot]).start()
        x = x_scr[...]
        gt = jnp.dot(x, wg_ref[...], preferred_element_type=F32)
        up = jnp.dot(x, wu_ref[...], preferred_element_type=F32)
        act = (gt * jax.nn.sigmoid(gt) * up).astype(BF16)
        y = jnp.dot(act, wd_ref[...], preferred_element_type=F32)

        @pl.when(j == 0)
        def _():
            o_ref[...] = y

        @pl.when(j != 0)
        def _():
            o_ref[...] += y

    gather_now = next_valid & (j < MOE_GATHER_STEPS)

    @pl.when(valid & gather_now)
    def _():
        step(True)

    @pl.when(valid & jnp.logical_not(gather_now))
    def _():
        step(False)

    @pl.when((tv_ref[i] == 0) & (j == 0))
    def _():
        o_ref[...] = jnp.zeros_like(o_ref)


def _moe(u, slot_tok, tile_e, tile_valid, w_gate, w_up, w_down):
    d = u.shape[1]
    cap = slot_tok.shape[0]
    n_tiles = cap // TM
    f = w_gate.shape[2]
    tf = TF_MOE
    nj = f // tf
    assert nj >= MOE_GATHER_STEPS and TM % MOE_GATHER_STEPS == 0

    def jj(i, j, tv):
        return jnp.where(tv[i] == 1, j, nj - 1)

    return pl.pallas_call(
        _moe_kernel,
        out_shape=jax.ShapeDtypeStruct((cap, d), F32),
        grid_spec=pltpu.PrefetchScalarGridSpec(
            num_scalar_prefetch=2,
            grid=(n_tiles, nj),
            in_specs=[pl.BlockSpec((TM,), lambda i, j, te, tv: (i,), memory_space=pltpu.SMEM),
                      pl.BlockSpec((TM,), lambda i, j, te, tv: (jnp.minimum(i + 1, n_tiles - 1),),
                                   memory_space=pltpu.SMEM),
                      pl.BlockSpec(memory_space=pl.ANY),
                      pl.BlockSpec((None, d, tf), lambda i, j, te, tv: (te[i], 0, jj(i, j, tv))),
                      pl.BlockSpec((None, d, tf), lambda i, j, te, tv: (te[i], 0, jj(i, j, tv))),
                      pl.BlockSpec((None, tf, d), lambda i, j, te, tv: (te[i], jj(i, j, tv), 0))],
            out_specs=pl.BlockSpec((TM, d), lambda i, j, te, tv: (i, 0)),
            scratch_shapes=[pltpu.VMEM((2, TM, d), F32), pltpu.VMEM((TM, d), BF16),
                            pltpu.SemaphoreType.DMA((2,))]),
        compiler_params=pltpu.CompilerParams(
            dimension_semantics=("arbitrary", "arbitrary"),
            vmem_limit_bytes=_vmem_limit(2 * TM * d * 4 + 2 * TM * d * 4 + 6 * d * tf * 2 + TM * d * 2
                                         + 3 * TM * tf * 4 + TM * d * 4)),
        name="moe_experts",
    )(tile_e, tile_valid, slot_tok, slot_tok, u, w_gate, w_up, w_down)


def _combine_kernel(dest_ref, route_ref, h_ref, gate_ref, fg_ref, yb_hbm, o_ref, y_scr, sem, *, rows):
    def start(r, carry):
        for kk in range(TOP_K):
            _row_copy(yb_hbm, dest_ref[TOP_K * r + kk], y_scr.at[kk], r, sem).start(priority=kk)
        return carry

    lax.fori_loop(0, rows, start, 0)

    def wait(r, carry):
        for kk in range(TOP_K):
            _row_copy(yb_hbm, 0, y_scr.at[kk], 0, sem).wait()
        return carry

    lax.fori_loop(0, rows, wait, 0)

    route = route_ref[...]
    moe = route[:, 2:3] * y_scr[0] + route[:, 3:4] * y_scr[1]
    hl = h_ref[...] + gate_ref[...] * moe
    ms = jnp.mean(hl * hl, axis=-1, keepdims=True)
    o_ref[...] = (hl * lax.rsqrt(ms + NORM_EPS)) * fg_ref[...]


def _combine(yb, dest, route, h, mods, final_g, lat_tile_to_mod):
    n, d = h.shape
    return pl.pallas_call(
        functools.partial(_combine_kernel, rows=TM),
        out_shape=jax.ShapeDtypeStruct((n, d), F32),
        grid=(n // TM,),
        in_specs=[pl.BlockSpec((TOP_K * TM,), lambda i: (i,), memory_space=pltpu.SMEM),
                  pl.BlockSpec((TM, LANES), lambda i: (i, 0)),
                  pl.BlockSpec((TM, d), lambda i: (i, 0)),
                  pl.BlockSpec((None, 1, d), lambda i: (lat_tile_to_mod(i) * N_MOD + 5, 0, 0)),
                  pl.BlockSpec((1, d), lambda i: (0, 0)),
                  pl.BlockSpec(memory_space=pl.ANY)],
        out_specs=pl.BlockSpec((TM, d), lambda i: (i, 0)),
        scratch_shapes=[pltpu.VMEM((TOP_K, TM, d), F32), pltpu.SemaphoreType.DMA],
        compiler_params=pltpu.CompilerParams(
            dimension_semantics=("arbitrary",),
            vmem_limit_bytes=_vmem_limit(TOP_K * TM * d * 4 + 4 * TM * d * 4 + 2 * TM * d * 4)),
        name="moe_combine",
    )(dest, route, h, mods, final_g.reshape(1, d), yb)


def _rope_tables(n_ident, s_len):
    t = jnp.arange(s_len)
    row_id = (t // GRID_W).astype(F32)
    col_id = (t % GRID_W).astype(F32)
    inv_freq = ROPE_BASE ** (-jnp.arange(ROPE_PAIRS, dtype=F32) / ROPE_PAIRS)
    ang_r = row_id[:, None] * inv_freq
    ang_c = col_id[:, None] * inv_freq
    ang = jnp.concatenate([ang_r, ang_r, ang_c, ang_c], axis=-1)
    cos, sin = jnp.cos(ang), jnp.sin(ang)
    first_half = (jnp.arange(HEAD_DIM) % (2 * ROPE_PAIRS)) < ROPE_PAIRS
    sa = jnp.where(first_half, -sin, 0.0)
    sb = jnp.where(first_half, 0.0, sin)
    ones = jnp.ones((n_ident, HEAD_DIM), F32)
    zeros = jnp.zeros((n_ident, HEAD_DIM), F32)
    return (jnp.concatenate([ones, cos]), jnp.concatenate([zeros, sa]), jnp.concatenate([zeros, sb]))


def _routing_plan(route, n_tiles):
    e = route[:, :TOP_K].astype(jnp.int32).reshape(-1)
    onehot = (e[:, None] == jnp.arange(N_EXPERTS, dtype=jnp.int32)[None, :]).astype(jnp.int32)
    csum = jnp.cumsum(onehot, axis=0)
    rank = jnp.sum(csum * onehot, axis=1) - 1
    counts = csum[-1]
    padded = (counts + TM - 1) // TM * TM
    pad_end = jnp.cumsum(padded)
    pad_start = pad_end - padded
    dest = jnp.sum(onehot * pad_start[None, :], axis=1) + rank
    n_valid = pad_end[-1] // TM
    tile_idx = jnp.arange(n_tiles, dtype=jnp.int32)
    tile_valid = (tile_idx < n_valid).astype(jnp.int32)
    tile_row = jnp.minimum(tile_idx, n_valid - 1)
    tile_e = jnp.sum((pad_end[None, :] <= (tile_row * TM)[:, None]).astype(jnp.int32), axis=1)
    tile_e = jnp.minimum(tile_e, N_EXPERTS - 1)
    dest = dest.astype(jnp.int32)
    slot_tok = jnp.zeros((n_tiles * TM,), jnp.int32).at[dest].set(
        jnp.arange(e.shape[0], dtype=jnp.int32) // TOP_K)
    return dest, slot_tok, tile_e.astype(jnp.int32), tile_valid


def _mods(cvecs, w_mod, b_mod):
    d = w_mod.shape[0]
    m = _adaln(cvecs, w_mod, b_mod)[:3]
    return m.reshape(3 * N_MOD, 1, d)


def kernel(x, c, ctx, c_ctx, l0_w_mod, l0_b_mod, l0_norm1_g, l0_w_in, l0_sinks, l0_conv_w, l0_conv_b, l0_w_out, l0_norm2_g, l0_ffn_w_gate, l0_ffn_w_up, l0_ffn_w_down, l1_w_mod, l1_b_mod, l1_norm1_g, l1_w_in, l1_conv_w, l1_conv_b, l1_gate_a_w, l1_gate_a_b, l1_gate_x_w, l1_gate_x_b, l1_lambda, l1_w_out, l1_norm2_g, l1_router_w, l1_router_b, l1_moe_w_gate, l1_moe_w_up, l1_moe_w_down, final_norm_g):
    n_batch, s_len, d = x.shape
    n_ctx = ctx.shape[1]
    assert n_batch == 2 and n_batch * n_ctx == TM and n_ctx == TR
    assert s_len % TM_FFN == 0 and s_len % GRID_W == 0 and WINDOW * 2 == TR
    n_ctx_rows = n_batch * n_ctx
    n_lat = n_batch * s_len
    n_ctx_rtiles = n_ctx_rows // TR
    tiles_per_seg = s_len // TR
    mtiles_per_seg = s_len // TM
    d_rnn = l1_w_out.shape[0]
    conv_dim = l0_conv_w.shape[1]
    q_dim = N_Q_HEADS * HEAD_DIM
    kv_dim = N_KV_HEADS * HEAD_DIM

    def tile_to_mod(i):
        return jnp.where(i == 0, n_batch, (i - 1) // mtiles_per_seg)

    def lat_tile_to_mod(i):
        return i // mtiles_per_seg

    def rope_blk(i):
        return jnp.where(i == 0, 0, 1 + (i - 1) % mtiles_per_seg)

    cvecs = jnp.concatenate([c, c_ctx[None, :], jnp.zeros((8 - n_batch - 1, d), F32)], axis=0)
    mods0 = _mods(cvecs, l0_w_mod, l0_b_mod)
    mods1 = _mods(cvecs, l1_w_mod, l1_b_mod)
    ctx2 = ctx.reshape(n_ctx_rows, d)
    x2 = x.reshape(n_lat, d)

    rope = _rope_tables(TM, s_len)
    z = _l0_in_proj(ctx2, x2, l0_norm1_g, mods0, l0_w_in.astype(BF16), rope, tile_to_mod, rope_blk)
    x_col = (q_dim + 2 * kv_dim) // CONV_CW
    per = conv_dim // CONV_CW
    conv = _gated_conv(z, l0_conv_w, l0_conv_b, n_ctx_rtiles, tiles_per_seg,
                       x_col, x_col + per, x_col + 2 * per)
    attn = _attention(z, l0_sinks, n_batch, n_ctx_rtiles, tiles_per_seg)
    hc2, hl = _l0_out_proj(attn, conv, l0_w_out.astype(BF16), ctx2, x2, mods0, tile_to_mod)
    ffn_w = (l0_ffn_w_gate.astype(BF16), l0_ffn_w_up.astype(BF16), l0_ffn_w_down.astype(BF16))
    hc = _ffn(hc2, pl.BlockSpec((None, TM, d), lambda i, j: (0, 0, 0)), n_ctx_rows, TM,
              l0_norm2_g, mods0, *ffn_w, lambda i: n_batch)
    hl = _ffn(hl, pl.BlockSpec((TM_FFN, d), lambda i, j: (i, 0), pipeline_mode=pl.Buffered(1)), n_lat, TM_FFN,
              l0_norm2_g, mods0, *ffn_w, lambda i: i // (s_len // TM_FFN))

    gg, xr = _l1_in_proj(hc, hl, l1_norm1_g, mods1, l1_w_in.astype(BF16), d_rnn, tile_to_mod)
    hf, hb = _rglru(xr, l1_conv_w, l1_conv_b, l1_gate_a_w.astype(BF16), l1_gate_x_w.astype(BF16),
                    l1_gate_a_b, l1_gate_x_b, l1_lambda, n_batch, n_ctx_rtiles, tiles_per_seg)
    h_lat = _l1_out_proj(hf, hb, gg, l1_w_out.astype(BF16), hl, mods1, n_ctx_rows // TM, lat_tile_to_mod)

    rw_pad = jnp.zeros((d, LANES), BF16).at[:, :N_EXPERTS].set(l1_router_w.astype(BF16))
    rb_pad = jnp.full((1, LANES), NEG_INF, F32).at[0, :N_EXPERTS].set(l1_router_b)
    u, route = _router(h_lat, l1_norm2_g, mods1, rw_pad, rb_pad, lat_tile_to_mod)
    n_tiles = (n_lat * TOP_K) // TM + N_EXPERTS
    dest, slot_tok, tile_e, tile_valid = _routing_plan(route, n_tiles)
    yb = _moe(u, slot_tok, tile_e, tile_valid, l1_moe_w_gate.astype(BF16), l1_moe_w_up.astype(BF16),
              l1_moe_w_down.astype(BF16))
    out = _combine(yb, dest, route, h_lat, mods1, final_norm_g, lat_tile_to_mod)
    return out.reshape(n_batch, s_len, d)
```

```python
import functools

import jax
import jax.numpy as jnp
from jax import lax
from jax.experimental import pallas as pl
from jax.experimental.pallas import tpu as pltpu

F32 = jnp.float32
BF16 = jnp.bfloat16

GRID_W = 64
HEAD_DIM = 128
N_Q_HEADS = 8
N_KV_HEADS = 2
Q_PER_KV = N_Q_HEADS // N_KV_HEADS
WINDOW = 128
ATTN_SCALE = HEAD_DIM ** -0.5
ROPE_BASE = 10000.0
ROPE_PAIRS = HEAD_DIM // 4
SHORT_CONV_OFFSETS = (-1, 0, 1)
RG_CONV_OFFSETS = (-2, -1, 0, 1)
N_RNN_HEADS = 16
RG_C = 8.0
N_EXPERTS = 8
TOP_K = 2
LOG2_E = 1.4426950408889634
NORM_EPS = 1e-6
NEG_INF = -1e30
N_MOD = 6

LANES = 128
TM = 512
CAST_BLOCK = 1024
TR = 256
HALO = 16
TN_CHUNK = 512
TF_FFN = 512
TF_MOE = 1024
MOE_GATHER_STEPS = 4
CONV_CW = 512
MOD_TN = 1024
VMEM_CAP = 56 * 1024 * 1024


def _vmem_limit(nbytes):
    return int(min(max(nbytes * 5 // 4 + (4 << 20), 32 << 20), VMEM_CAP))


def _sigmoid(x):
    return 0.5 * jnp.tanh(0.5 * x) + 0.5


def _norm_mod(h, g, shift, scale):
    ms = jnp.mean(h * h, axis=-1, keepdims=True)
    y = h * lax.rsqrt(ms + NORM_EPS)
    return (y * g) * (1.0 + scale) + shift


def _adaln_kernel(c_ref, w_ref, b_ref, o_ref):
    c = c_ref[...]
    s = (c * jax.nn.sigmoid(c)).astype(BF16)
    o_ref[...] = jnp.dot(s, w_ref[...].astype(BF16), preferred_element_type=F32) + b_ref[...]


def _adaln(cvecs, w_mod, b_mod):
    d, n = w_mod.shape
    return pl.pallas_call(
        _adaln_kernel,
        out_shape=jax.ShapeDtypeStruct((8, n), F32),
        grid=(n // MOD_TN,),
        in_specs=[pl.BlockSpec((8, d), lambda j: (0, 0)),
                  pl.BlockSpec((d, MOD_TN), lambda j: (0, j)),
                  pl.BlockSpec((1, MOD_TN), lambda j: (0, j))],
        out_specs=pl.BlockSpec((8, MOD_TN), lambda j: (0, j)),
        compiler_params=pltpu.CompilerParams(
            dimension_semantics=("arbitrary",),
            vmem_limit_bytes=_vmem_limit(2 * d * MOD_TN * 4 + d * MOD_TN * 2)),
        name="adaln",
    )(cvecs, w_mod, b_mod.reshape(1, n))


def _mod_spec(chunk, width, tile_to_mod):
    return pl.BlockSpec((None, 1, width), lambda i, j: (tile_to_mod(i) * N_MOD + chunk, 0, 0))


def _mod_spec1(chunk, width, tile_to_mod):
    return pl.BlockSpec((None, 1, width), lambda i: (tile_to_mod(i) * N_MOD + chunk, 0, 0))


def _resident(shape):
    return pl.BlockSpec(shape, lambda i: (0,) * len(shape), pipeline_mode=pl.Buffered(1))


def _l0_in_kernel(ctx_ref, x_ref, g_ref, sh_ref, sc_ref, w_ref, cos_ref, sa_ref, sb_ref, o_ref):
    i = pl.program_id(0)
    h = jnp.where(i == 0, ctx_ref[...], x_ref[...])
    u = _norm_mod(h, g_ref[...], sh_ref[...], sc_ref[...]).astype(BF16)
    cos, sa, sb = cos_ref[...], sa_ref[...], sb_ref[...]
    n_rot = N_Q_HEADS + N_KV_HEADS
    n = o_ref.shape[1]
    for c0 in range(0, n, TN_CHUNK):
        z = jnp.dot(u, w_ref[:, c0:c0 + TN_CHUNK], preferred_element_type=F32)
        if c0 >= n_rot * HEAD_DIM:
            o_ref[:, c0:c0 + TN_CHUNK] = z.astype(BF16)
            continue
        for k in range(TN_CHUNK // HEAD_DIM):
            hh = c0 // HEAD_DIM + k
            t = z[:, k * HEAD_DIM:(k + 1) * HEAD_DIM]
            if hh < n_rot:
                t = (t * cos + pltpu.roll(t, HEAD_DIM - ROPE_PAIRS, 1) * sa
                     + pltpu.roll(t, ROPE_PAIRS, 1) * sb)
            if hh < N_Q_HEADS:
                t = t * ATTN_SCALE
            o_ref[:, hh * HEAD_DIM:(hh + 1) * HEAD_DIM] = t.astype(BF16)


def _l0_in_proj(ctx2, x2, g, mods, w_in, rope, tile_to_mod, rope_blk):
    n_ctx_rows, d = ctx2.shape
    assert n_ctx_rows == TM
    r = n_ctx_rows + x2.shape[0]
    n = w_in.shape[1]
    assert n % TN_CHUNK == 0
    cos, sa, sb = rope
    rope_spec = pl.BlockSpec((TM, HEAD_DIM), lambda i: (rope_blk(i), 0))
    return pl.pallas_call(
        _l0_in_kernel,
        out_shape=jax.ShapeDtypeStruct((r, n), BF16),
        grid=(r // TM,),
        in_specs=[_resident((TM, d)),
                  pl.BlockSpec((TM, d), lambda i: (jnp.maximum(i - 1, 0), 0)),
                  pl.BlockSpec((1, d), lambda i: (0, 0)),
                  _mod_spec1(0, d, tile_to_mod),
                  _mod_spec1(1, d, tile_to_mod),
                  _resident((d, n)),
                  rope_spec, rope_spec, rope_spec],
        out_specs=pl.BlockSpec((TM, n), lambda i: (i, 0)),
        compiler_params=pltpu.CompilerParams(
            dimension_semantics=("arbitrary",),
            vmem_limit_bytes=_vmem_limit(3 * TM * d * 4 + d * n * 2 + 2 * TM * n * 2
                                         + 2 * TM * d * 4 + 4 * TM * TN_CHUNK * 4)),
        name="l0_in_proj",
    )(ctx2, x2, g.reshape(1, d), mods, mods, w_in, cos, sa, sb)


def _conv_taps(x, xp, xn, first, last, w, b, offsets):
    xp = jnp.where(first, 0.0, xp)
    xn = jnp.where(last, 0.0, xn)
    tr, cw = x.shape
    row8 = lax.broadcasted_iota(jnp.int32, (8, cw), 0)
    acc = jnp.broadcast_to(b, (tr, cw))
    for k, off in enumerate(offsets):
        wk = w[k:k + 1, :]
        if off == 0:
            y = x
        elif off < 0:
            s = -off
            r = pltpu.roll(x, s, 0)
            rp = pltpu.roll(xp, s, 0)[0:8]
            head = jnp.where(row8 < s, rp, r[0:8])
            y = jnp.concatenate([head, r[8:]], axis=0)
        else:
            r = pltpu.roll(x, tr - off, 0)
            rn = pltpu.roll(xn, HALO - off, 0)[HALO - 8:HALO]
            tail = jnp.where(row8 >= 8 - off, rn, r[tr - 8:])
            y = jnp.concatenate([r[:tr - 8], tail], axis=0)
        acc = acc + wk * y
    return acc


def _gated_conv_kernel(x_ref, bg_ref, cg_ref, xp_ref, cgp_ref, xn_ref, cgn_ref, w_ref, b_ref, o_ref,
                       *, n_ctx_tiles, tiles_per_seg):
    i = pl.program_id(0)
    li = i - n_ctx_tiles
    is_ctx = i < n_ctx_tiles
    first = is_ctx | (li % tiles_per_seg == 0)
    last = is_ctx | (li % tiles_per_seg == tiles_per_seg - 1)
    x = x_ref[...].astype(F32) * cg_ref[...].astype(F32)
    xp = xp_ref[...].astype(F32) * cgp_ref[...].astype(F32)
    xn = xn_ref[...].astype(F32) * cgn_ref[...].astype(F32)
    acc = _conv_taps(x, xp, xn, first, last, w_ref[...], b_ref[...], SHORT_CONV_OFFSETS)
    o_ref[...] = (acc * bg_ref[...].astype(F32)).astype(o_ref.dtype)


def _gated_conv(z, w, b, n_ctx_tiles, tiles_per_seg, x_col, bg_col, cg_col):
    r = z.shape[0]
    c = w.shape[1]
    nh = TR // HALO
    n_halo = r // HALO

    def main(o):
        return pl.BlockSpec((TR, CONV_CW), lambda i, j: (i, o + j))

    def prev(o):
        return pl.BlockSpec((HALO, CONV_CW), lambda i, j: (jnp.maximum(i * nh - 1, 0), o + j))

    def nxt(o):
        return pl.BlockSpec((HALO, CONV_CW), lambda i, j: (jnp.minimum((i + 1) * nh, n_halo - 1), o + j))

    return pl.pallas_call(
        functools.partial(_gated_conv_kernel, n_ctx_tiles=n_ctx_tiles, tiles_per_seg=tiles_per_seg),
        out_shape=jax.ShapeDtypeStruct((r, c), BF16),
        grid=(r // TR, c // CONV_CW),
        in_specs=[main(x_col), main(bg_col), main(cg_col), prev(x_col), prev(cg_col), nxt(x_col), nxt(cg_col),
                  pl.BlockSpec((len(SHORT_CONV_OFFSETS), CONV_CW), lambda i, j: (0, j)),
                  pl.BlockSpec((1, CONV_CW), lambda i, j: (0, j))],
        out_specs=pl.BlockSpec((TR, CONV_CW), lambda i, j: (i, j)),
        compiler_params=pltpu.CompilerParams(dimension_semantics=("arbitrary", "arbitrary")),
        name="gated_conv",
    )(z, z, z, z, z, z, z, w, b.reshape(1, c))


def _attn_kernel(sink_ref, q_ref, kc_ref, vc_ref, kp_ref, ko_ref, kn_ref, vp_ref, vo_ref, vn_ref, o_ref,
                 *, tiles_per_seg):
    t = pl.program_id(1)
    tt = t - 1
    tr = q_ref.shape[0]
    n_ctx = kc_ref.shape[0]
    halo = kp_ref.shape[0]
    n_win = tr + 2 * halo
    qi = lax.broadcasted_iota(jnp.int32, (tr, n_win), 0)
    c = lax.broadcasted_iota(jnp.int32, (tr, n_win), 1)
    ok = (c >= qi) & (c <= qi + 2 * WINDOW)
    ok = ok & ((c >= halo) | (tt > 0)) & ((c < halo + tr) | (tt < tiles_per_seg - 1)) & (t > 0)
    bias = jnp.concatenate([jnp.zeros((tr, n_ctx), F32), jnp.where(ok, 0.0, NEG_INF)], axis=1)
    for hk in range(N_KV_HEADS):
        cs = slice(hk * HEAD_DIM, (hk + 1) * HEAD_DIM)
        k_all = jnp.concatenate([kc_ref[:, cs], kp_ref[:, cs], ko_ref[:, cs], kn_ref[:, cs]], axis=0)
        v_all = jnp.concatenate([vc_ref[:, cs], vp_ref[:, cs], vo_ref[:, cs], vn_ref[:, cs]], axis=0)
        for g in range(Q_PER_KV):
            hq = hk * Q_PER_KV + g
            qs = slice(hq * HEAD_DIM, (hq + 1) * HEAD_DIM)
            s = lax.dot_general(q_ref[:, qs], k_all, (((1,), (1,)), ((), ())),
                                preferred_element_type=F32) + bias
            sink = sink_ref[hq]
            m = jnp.maximum(jnp.max(s, axis=-1, keepdims=True), sink)
            p = jnp.exp(s - m)
            denom = jnp.sum(p, axis=-1, keepdims=True) + jnp.exp(sink - m)
            o = jnp.dot(p.astype(BF16), v_all, preferred_element_type=F32)
            o_ref[:, qs] = (o / denom).astype(o_ref.dtype)


def _attention(z, sinks, n_batch, n_ctx_tiles, tiles_per_seg):
    r = z.shape[0]
    q_dim = N_Q_HEADS * HEAD_DIM
    kv_dim = N_KV_HEADS * HEAD_DIM
    k_col = q_dim // kv_dim
    v_col = k_col + 1
    halo = WINDOW
    per = TR // halo
    n_halo_blk = r // halo

    def qblk(b, t):
        return jnp.where(t == 0, b, n_ctx_tiles + b * tiles_per_seg + t - 1)

    def own(col):
        return pl.BlockSpec((TR, kv_dim), lambda b, t: (qblk(b, t), col))

    def ctx(col):
        return pl.BlockSpec((TR, kv_dim), lambda b, t: (b, col))

    def prev(col):
        return pl.BlockSpec((halo, kv_dim), lambda b, t: (jnp.maximum(qblk(b, t) * per - 1, 0), col))

    def nxt(col):
        return pl.BlockSpec((halo, kv_dim),
                            lambda b, t: (jnp.minimum((qblk(b, t) + 1) * per, n_halo_blk - 1), col))

    return pl.pallas_call(
        functools.partial(_attn_kernel, tiles_per_seg=tiles_per_seg),
        out_shape=jax.ShapeDtypeStruct((r, q_dim), BF16),
        grid=(n_batch, 1 + tiles_per_seg),
        in_specs=[pl.BlockSpec(memory_space=pltpu.SMEM),
                  pl.BlockSpec((TR, q_dim), lambda b, t: (qblk(b, t), 0)),
                  ctx(k_col), ctx(v_col),
                  prev(k_col), own(k_col), nxt(k_col),
                  prev(v_col), own(v_col), nxt(v_col)],
        out_specs=pl.BlockSpec((TR, q_dim), lambda b, t: (qblk(b, t), 0)),
        compiler_params=pltpu.CompilerParams(dimension_semantics=("arbitrary", "arbitrary")),
        name="attention",
    )(sinks, z, z, z, z, z, z, z, z, z)


def _l0_out_kernel(ctx_ref, x_ref, a1_ref, a2_ref, w_ref, gate_ref, oc_ref, ol_ref):
    i = pl.program_id(0)
    lhs = jnp.concatenate([a1_ref[...], a2_ref[...]], axis=1)
    d = ol_ref.shape[1]
    for c0 in range(0, d, TN_CHUNK):
        cs = slice(c0, c0 + TN_CHUNK)
        y = jnp.dot(lhs, w_ref[:, cs], preferred_element_type=F32)
        h = jnp.where(i == 0, ctx_ref[:, cs], x_ref[:, cs])
        res = h + gate_ref[:, cs] * y
        oc_ref[:, cs] = res
        ol_ref[:, cs] = res


def _l0_out_proj(attn, conv, w_out, ctx2, x2, mods, tile_to_mod):
    n_ctx_rows, d = ctx2.shape
    n_lat = x2.shape[0]
    r = n_ctx_rows + n_lat
    k1 = attn.shape[1]
    k2 = conv.shape[1]
    assert w_out.shape[0] == k1 + k2 and d % TN_CHUNK == 0
    return pl.pallas_call(
        _l0_out_kernel,
        out_shape=(jax.ShapeDtypeStruct((2, TM, d), F32), jax.ShapeDtypeStruct((n_lat, d), F32)),
        grid=(r // TM,),
        in_specs=[_resident((TM, d)),
                  pl.BlockSpec((TM, d), lambda i: (jnp.maximum(i - 1, 0), 0)),
                  pl.BlockSpec((TM, k1), lambda i: (i, 0)),
                  pl.BlockSpec((TM, k2), lambda i: (i, 0)),
                  _resident((k1 + k2, d)),
                  _mod_spec1(2, d, tile_to_mod)],
        out_specs=(pl.BlockSpec((None, TM, d), lambda i: (jnp.minimum(i, 1), 0, 0)),
                   pl.BlockSpec((TM, d), lambda i: (jnp.maximum(i - 1, 0), 0))),
        compiler_params=pltpu.CompilerParams(
            dimension_semantics=("arbitrary",),
            vmem_limit_bytes=_vmem_limit(3 * TM * d * 4 + 3 * TM * (k1 + k2) * 2 + (k1 + k2) * d * 2
                                         + 4 * TM * d * 4 + 4 * TM * TN_CHUNK * 4)),
        name="l0_out_proj",
    )(ctx2, x2, attn, conv, w_out, mods)


def _ffn_kernel(*refs, cast_blocks):
    n_cast = len(cast_blocks)
    h_ref, g_ref, sh_ref, sc_ref, gate_ref, wg_ref, wu_ref, wd_ref = refs[:8]
    src = refs[8:8 + n_cast]
    o_ref = refs[8 + n_cast]
    dst = refs[9 + n_cast:9 + 2 * n_cast]
    u_scr = refs[9 + 2 * n_cast]
    i = pl.program_id(0)
    j = pl.program_id(1)
    nj = pl.num_programs(1)
    cb = CAST_BLOCK

    if n_cast:
        cin, cout, sem_in, sem_out = refs[10 + 2 * n_cast:]
        n_blocks = sum(ne * nbr * nbc for ne, nbr, nbc in cast_blocks)
        s = i * nj + j
        slot = s % 2

        def for_block(b, fn):
            off = 0
            for k, (ne, nbr, nbc) in enumerate(cast_blocks):
                nk = ne * nbr * nbc

                @pl.when((b >= off) & (b < off + nk))
                def _(k=k, off=off, nbr=nbr, nbc=nbc):
                    q = b - off
                    rc = q % (nbr * nbc)
                    fn(k, q // (nbr * nbc), rc // nbc, rc % nbc)
                off += nk

        def block_of(ref, e, r, c):
            return ref.at[e, pl.ds(r * cb, cb), pl.ds(c * cb, cb)]

        def start_in(b, to_slot):
            for_block(b, lambda k, e, r, c: pltpu.make_async_copy(
                block_of(src[k], e, r, c), cin.at[to_slot], sem_in.at[to_slot]).start())

        @pl.when(s == 0)
        def _():
            start_in(0, 0)

        @pl.when(s < n_blocks)
        def _():
            pltpu.make_async_copy(block_of(src[0], 0, 0, 0), cin.at[slot], sem_in.at[slot]).wait()

        @pl.when(s + 1 < n_blocks)
        def _():
            start_in(s + 1, 1 - slot)

        @pl.when((s >= 2) & (s - 2 < n_blocks))
        def _():
            pltpu.make_async_copy(cout.at[slot], block_of(dst[0], 0, 0, 0), sem_out.at[slot]).wait()

    @pl.when(j == 0)
    def _():
        u_scr[...] = _norm_mod(h_ref[...], g_ref[...], sh_ref[...], sc_ref[...]).astype(BF16)

    if n_cast:
        cout[slot] = cin[slot].astype(BF16)
    u = u_scr[...]
    gt = jnp.dot(u, wg_ref[...], preferred_element_type=F32)
    up = jnp.dot(u, wu_ref[...], preferred_element_type=F32)
    act = (gt * jax.nn.sigmoid(gt) * up).astype(BF16)
    y = jnp.dot(act, wd_ref[...], preferred_element_type=F32)

    @pl.when(j == 0)
    def _():
        o_ref[...] = y

    @pl.when(j != 0)
    def _():
        o_ref[...] += y

    @pl.when(j == nj - 1)
    def _():
        o_ref[...] = h_ref[...] + gate_ref[...] * o_ref[...]

    if n_cast:
        @pl.when(s < n_blocks)
        def _():
            for_block(s, lambda k, e, r, c: pltpu.make_async_copy(
                cout.at[slot], block_of(dst[k], e, r, c), sem_out.at[slot]).start())


def _ffn(h, h_spec, rows, tm, g, mods, w_gate, w_up, w_down, tile_to_mod, riders=()):
    d = h.shape[-1]
    f = w_gate.shape[1]
    tf = TF_FFN
    grid = (rows // tm, f // tf)
    cast_blocks = tuple((w.shape[0], w.shape[1] // CAST_BLOCK, w.shape[2] // CAST_BLOCK) for w in riders)
    for w in riders:
        assert w.shape[1] % CAST_BLOCK == 0 and w.shape[2] % CAST_BLOCK == 0
    n_blocks = sum(ne * nbr * nbc for ne, nbr, nbc in cast_blocks)
    assert not riders or grid[0] * grid[1] >= n_blocks + 2
    any_spec = pl.BlockSpec(memory_space=pl.ANY)
    scratch = [pltpu.VMEM((tm, d), BF16)]
    if riders:
        scratch += [pltpu.VMEM((2, CAST_BLOCK, CAST_BLOCK), F32), pltpu.VMEM((2, CAST_BLOCK, CAST_BLOCK), BF16),
                    pltpu.SemaphoreType.DMA((2,)), pltpu.SemaphoreType.DMA((2,))]
    return pl.pallas_call(
        functools.partial(_ffn_kernel, cast_blocks=cast_blocks),
        out_shape=(jax.ShapeDtypeStruct((rows, d), F32),
                   *[jax.ShapeDtypeStruct(w.shape, BF16) for w in riders]),
        grid=grid,
        in_specs=[h_spec,
                  pl.BlockSpec((1, d), lambda i, j: (0, 0)),
                  _mod_spec(3, d, tile_to_mod),
                  _mod_spec(4, d, tile_to_mod),
                  _mod_spec(5, d, tile_to_mod),
                  pl.BlockSpec((d, tf), lambda i, j: (0, j)),
                  pl.BlockSpec((d, tf), lambda i, j: (0, j)),
                  pl.BlockSpec((tf, d), lambda i, j: (j, 0))] + [any_spec] * len(riders),
        out_specs=(pl.BlockSpec((tm, d), lambda i, j: (i, 0)), *[any_spec] * len(riders)),
        scratch_shapes=scratch,
        compiler_params=pltpu.CompilerParams(
            dimension_semantics=("arbitrary", "arbitrary"),
            vmem_limit_bytes=_vmem_limit(4 * tm * d * 4 + 6 * d * tf * 2 + tm * d * 2
                                         + 3 * tm * tf * 4 + tm * tf * 2
                                         + (12 * CAST_BLOCK * CAST_BLOCK if riders else 0))),
        name="ffn",
    )(h, g.reshape(1, d), mods, mods, mods, w_gate, w_up, w_down, *riders)


def _l1_in_kernel(hc_ref, hl_ref, g_ref, sh_ref, sc_ref, w_ref, gg_ref, xr_ref):
    h = jnp.where(pl.program_id(0) == 0, hc_ref[...], hl_ref[...])
    u = _norm_mod(h, g_ref[...], sh_ref[...], sc_ref[...]).astype(BF16)
    d_rnn = gg_ref.shape[1]
    for c0 in range(0, 2 * d_rnn, TN_CHUNK):
        z = jnp.dot(u, w_ref[:, c0:c0 + TN_CHUNK], preferred_element_type=F32)
        if c0 < d_rnn:
            gg_ref[:, c0:c0 + TN_CHUNK] = jax.nn.gelu(z, approximate=True).astype(gg_ref.dtype)
        else:
            xr_ref[:, c0 - d_rnn:c0 - d_rnn + TN_CHUNK] = z


def _l1_in_proj(h_ctx, h_lat, g, mods, w_in, d_rnn, tile_to_mod):
    n_ctx_rows, d = h_ctx.shape
    assert n_ctx_rows == TM
    r = n_ctx_rows + h_lat.shape[0]
    assert w_in.shape[1] == 2 * d_rnn and d_rnn % TN_CHUNK == 0
    return pl.pallas_call(
        _l1_in_kernel,
        out_shape=(jax.ShapeDtypeStruct((r, d_rnn), BF16), jax.ShapeDtypeStruct((r, d_rnn), F32)),
        grid=(r // TM,),
        in_specs=[_resident((TM, d)),
                  pl.BlockSpec((TM, d), lambda i: (jnp.maximum(i - 1, 0), 0)),
                  pl.BlockSpec((1, d), lambda i: (0, 0)),
                  _mod_spec1(0, d, tile_to_mod),
                  _mod_spec1(1, d, tile_to_mod),
                  _resident((d, 2 * d_rnn))],
        out_specs=(pl.BlockSpec((TM, d_rnn), lambda i: (i, 0)),
                   pl.BlockSpec((TM, d_rnn), lambda i: (i, 0))),
        compiler_params=pltpu.CompilerParams(
            dimension_semantics=("arbitrary",),
            vmem_limit_bytes=_vmem_limit(2 * TM * d * 4 + d * 2 * d_rnn * 2 + 2 * TM * d_rnn * 6
                                         + 2 * TM * d * 4 + 4 * TM * TN_CHUNK * 4)),
        name="l1_in_proj",
    )(h_ctx, h_lat, g.reshape(1, d), mods, mods, w_in)


def _rglru_kernel(xf_ref, xfp_ref, xfn_ref, xb_ref, xbp_ref, xbn_ref, cw_ref, cb_ref,
                  gaw_ref, gxw_ref, gab_ref, gxb_ref, lam_ref, hf_ref, hb_ref,
                  xc_scr, a_scr, b_scr, h_scr, *, tiles_per_seg):
    c = pl.program_id(1)
    t_len, d_rnn = xf_ref.shape
    hd = d_rnn // N_RNN_HEADS

    @pl.when(c == 0)
    def _():
        h_scr[...] = jnp.zeros_like(h_scr)

    is_ctx = c == 0
    chunk = (c - 1, tiles_per_seg - c)
    halos = ((xfp_ref, xfn_ref), (xbp_ref, xbn_ref))

    for z, x_ref in ((0, xf_ref), (1, xb_ref)):
        first = is_ctx | (chunk[z] == 0)
        last = is_ctx | (chunk[z] == tiles_per_seg - 1)
        xc_scr[...] = _conv_taps(x_ref[...], halos[z][0][...], halos[z][1][...], first, last,
                                 cw_ref[...], cb_ref[...], RG_CONV_OFFSETS)
        x_ref = xc_scr
        neg_lam = -lam_ref[z]
        sp = jnp.maximum(neg_lam, 0.0) + jnp.log1p(jnp.exp(-jnp.abs(neg_lam)))
        log2_a_per_r = sp * (-RG_C * LOG2_E)
        for hh in range(N_RNN_HEADS):
            sl = slice(hh * hd, (hh + 1) * hd)
            xh = x_ref[:, sl]
            xh16 = xh.astype(BF16)
            ra = jnp.dot(xh16, gaw_ref[z, hh], preferred_element_type=F32) + gab_ref[z][:, sl]
            ri = jnp.dot(xh16, gxw_ref[z, hh], preferred_element_type=F32) + gxb_ref[z][:, sl]
            a = jnp.exp2(_sigmoid(ra) * log2_a_per_r[:, sl])
            om = 1.0 - a * a
            root = jnp.where(om > 0.0, om * lax.rsqrt(om), 0.0)
            a_scr[z, :, sl] = a
            b_scr[z, :, sl] = root * _sigmoid(ri) * xh

    row8 = lax.broadcasted_iota(jnp.int32, (8, d_rnn), 0)
    n_grp = t_len // 8

    def fwd_group(r0, h):
        a = a_scr[0, pl.ds(r0, 8), :]
        b = b_scr[0, pl.ds(r0, 8), :]
        for s in (1, 2, 4):
            a_sh = jnp.where(row8 >= s, pltpu.roll(a, s, 0), 1.0)
            b_sh = jnp.where(row8 >= s, pltpu.roll(b, s, 0), 0.0)
            b = a * b_sh + b
            a = a * a_sh
        out = a * h + b
        return out, jnp.broadcast_to(out[7:8, :], (8, d_rnn))

    def bwd_group(r0, h):
        a = a_scr[1, pl.ds(r0, 8), :]
        b = b_scr[1, pl.ds(r0, 8), :]
        for s in (1, 2, 4):
            a_sh = jnp.where(row8 < 8 - s, pltpu.roll(a, 8 - s, 0), 1.0)
            b_sh = jnp.where(row8 < 8 - s, pltpu.roll(b, 8 - s, 0), 0.0)
            b = a * b_sh + b
            a = a * a_sh
        out = a * h + b
        return out, jnp.broadcast_to(out[0:1, :], (8, d_rnn))

    def fwd_body(g, h):
        r0 = pl.multiple_of(g * 16, 16)
        lo, h = fwd_group(r0, h)
        hi, h = fwd_group(r0 + 8, h)
        hf_ref[pl.ds(r0, 16), :] = jnp.concatenate([lo, hi], axis=0).astype(hf_ref.dtype)
        return h

    def bwd_body(k, h):
        r0 = pl.multiple_of((n_grp // 2 - 1 - k) * 16, 16)
        hi, h = bwd_group(r0 + 8, h)
        lo, h = bwd_group(r0, h)
        hb_ref[pl.ds(r0, 16), :] = jnp.concatenate([lo, hi], axis=0).astype(hb_ref.dtype)
        return h

    h_scr[0] = lax.fori_loop(0, n_grp // 2, fwd_body, h_scr[0])
    h_scr[1] = lax.fori_loop(0, n_grp // 2, bwd_body, h_scr[1])


def _rglru(xr, conv_w, conv_b, ga_w, gx_w, ga_b, gx_b, lam, n_batch, n_ctx_tiles, tiles_per_seg):
    r, d_rnn = xr.shape
    n_lat = n_batch * tiles_per_seg * TR
    hd = d_rnn // N_RNN_HEADS
    nt = tiles_per_seg
    nh = TR // HALO
    n_halo = r // HALO
    n_taps = conv_w.shape[0]

    def fblk(b, c):
        return jnp.where(c == 0, b, n_ctx_tiles + b * nt + c - 1)

    def bblk(b, c):
        return jnp.where(c == 0, b, n_ctx_tiles + b * nt + nt - c)

    def chunk_specs(blk):
        return [pl.BlockSpec((TR, d_rnn), lambda b, c: (blk(b, c), 0)),
                pl.BlockSpec((HALO, d_rnn), lambda b, c: (jnp.maximum(blk(b, c) * nh - 1, 0), 0)),
                pl.BlockSpec((HALO, d_rnn), lambda b, c: (jnp.minimum((blk(b, c) + 1) * nh, n_halo - 1), 0))]

    def full(shape):
        return pl.BlockSpec(shape, lambda b, c: (0,) * len(shape))

    return pl.pallas_call(
        functools.partial(_rglru_kernel, tiles_per_seg=nt),
        out_shape=(jax.ShapeDtypeStruct((n_lat, d_rnn), BF16), jax.ShapeDtypeStruct((n_lat, d_rnn), BF16)),
        grid=(n_batch, 1 + nt),
        in_specs=chunk_specs(fblk) + chunk_specs(bblk) + [
            full((n_taps, d_rnn)), full((1, d_rnn)),
            full((2, N_RNN_HEADS, hd, hd)), full((2, N_RNN_HEADS, hd, hd)),
            full((2, 1, d_rnn)), full((2, 1, d_rnn)), full((2, 1, d_rnn))],
        out_specs=(pl.BlockSpec((TR, d_rnn), lambda b, c: (b * nt + jnp.maximum(c - 1, 0), 0)),
                   pl.BlockSpec((TR, d_rnn), lambda b, c: (b * nt + jnp.where(c == 0, nt - 1, nt - c), 0))),
        scratch_shapes=[pltpu.VMEM((TR, d_rnn), F32),
                        pltpu.VMEM((2, TR, d_rnn), F32), pltpu.VMEM((2, TR, d_rnn), F32),
                        pltpu.VMEM((2, 8, d_rnn), F32)],
        compiler_params=pltpu.CompilerParams(
            dimension_semantics=("arbitrary", "arbitrary"),
            vmem_limit_bytes=_vmem_limit(14 * TR * d_rnn * 4 + 8 * N_RNN_HEADS * hd * hd * 2)),
        name="rglru",
    )(xr, xr, xr, xr, xr, xr, conv_w, conv_b.reshape(1, d_rnn), ga_w, gx_w,
      ga_b.reshape(2, 1, d_rnn), gx_b.reshape(2, 1, d_rnn), lam.reshape(2, 1, d_rnn))


def _l1_out_kernel(hf_ref, hb_ref, gg_ref, w_ref, h_ref, gate_ref, o_ref):
    rec = hf_ref[...].astype(F32) + hb_ref[...].astype(F32)
    lhs = (rec * gg_ref[...].astype(F32)).astype(BF16)
    d = o_ref.shape[1]
    for c0 in range(0, d, TN_CHUNK):
        cs = slice(c0, c0 + TN_CHUNK)
        y = jnp.dot(lhs, w_ref[:, cs], preferred_element_type=F32)
        o_ref[:, cs] = h_ref[:, cs] + gate_ref[:, cs] * y


def _l1_out_proj(hf, hb, gg, w_out, h, mods, n_ctx_mtiles, lat_tile_to_mod):
    n_lat, d_rnn = hf.shape
    d = h.shape[1]
    assert d % TN_CHUNK == 0
    return pl.pallas_call(
        _l1_out_kernel,
        out_shape=jax.ShapeDtypeStruct((n_lat, d), F32),
        grid=(n_lat // TM,),
        in_specs=[pl.BlockSpec((TM, d_rnn), lambda i: (i, 0)),
                  pl.BlockSpec((TM, d_rnn), lambda i: (i, 0)),
                  pl.BlockSpec((TM, d_rnn), lambda i: (i + n_ctx_mtiles, 0)),
                  _resident((d_rnn, d)),
                  pl.BlockSpec((TM, d), lambda i: (i, 0)),
                  _mod_spec1(2, d, lat_tile_to_mod)],
        out_specs=pl.BlockSpec((TM, d), lambda i: (i, 0)),
        compiler_params=pltpu.CompilerParams(
            dimension_semantics=("arbitrary",),
            vmem_limit_bytes=_vmem_limit(6 * TM * d_rnn * 2 + d_rnn * d * 2 + 4 * TM * d * 4
                                         + 2 * TM * d_rnn * 4 + 4 * TM * TN_CHUNK * 4)),
        name="l1_out_proj",
    )(hf, hb, gg, w_out, h, mods)


def _router_kernel(h_ref, g_ref, sh_ref, sc_ref, rw_ref, rb_ref, u_ref, route_ref):
    u = _norm_mod(h_ref[...], g_ref[...], sh_ref[...], sc_ref[...])
    u_ref[...] = u
    logits = jnp.dot(u.astype(BF16), rw_ref[...], preferred_element_type=F32) + rb_ref[...]
    lane = lax.broadcasted_iota(jnp.int32, logits.shape, 1)
    m1 = jnp.max(logits, axis=-1, keepdims=True)
    i1 = jnp.min(jnp.where(logits == m1, lane, LANES), axis=-1, keepdims=True)
    rest = jnp.where(lane == i1, -jnp.inf, logits)
    m2 = jnp.max(rest, axis=-1, keepdims=True)
    i2 = jnp.min(jnp.where(rest == m2, lane, LANES), axis=-1, keepdims=True)
    e2 = jnp.exp(m2 - m1)
    w1 = 1.0 / (1.0 + e2)
    w2 = e2 * w1
    route = jnp.where(lane == 0, i1.astype(F32),
                      jnp.where(lane == 1, i2.astype(F32),
                                jnp.where(lane == 2, w1, jnp.where(lane == 3, w2, 0.0))))
    route_ref[...] = route


def _router(h, g, mods, rw_pad, rb_pad, lat_tile_to_mod):
    n, d = h.shape

    def mspec(chunk):
        return pl.BlockSpec((None, 1, d), lambda i: (lat_tile_to_mod(i) * N_MOD + chunk, 0, 0))

    return pl.pallas_call(
        _router_kernel,
        out_shape=(jax.ShapeDtypeStruct((n, d), F32), jax.ShapeDtypeStruct((n, LANES), F32)),
        grid=(n // TM,),
        in_specs=[pl.BlockSpec((TM, d), lambda i: (i, 0)),
                  pl.BlockSpec((1, d), lambda i: (0, 0)),
                  mspec(3), mspec(4),
                  pl.BlockSpec((d, LANES), lambda i: (0, 0)),
                  pl.BlockSpec((1, LANES), lambda i: (0, 0))],
        out_specs=(pl.BlockSpec((TM, d), lambda i: (i, 0)),
                   pl.BlockSpec((TM, LANES), lambda i: (i, 0))),
        compiler_params=pltpu.CompilerParams(
            dimension_semantics=("arbitrary",),
            vmem_limit_bytes=_vmem_limit(6 * TM * d * 4)),
        name="router",
    )(h, g.reshape(1, d), mods, mods, rw_pad, rb_pad)


def _row_copy(src, src_row, dst, dst_row, sem):
    return pltpu.make_async_copy(src.at[pl.ds(src_row, 1)], dst.at[pl.ds(dst_row, 1)], sem)


def _moe_kernel(te_ref, tv_ref, tok_ref, tok_next_ref, u_hbm, wg_ref, wu_ref, wd_ref, o_ref,
                xg_scr, x_scr, sems, *, gather_steps):
    del te_ref
    i = pl.program_id(0)
    j = pl.program_id(1)
    n_tiles = pl.num_programs(0)
    rows = x_scr.shape[0]
    per_step = rows // gather_steps
    slot = i % 2
    valid = tv_ref[i] == 1
    next_valid = (i + 1 < n_tiles) & (tv_ref[jnp.minimum(i + 1, n_tiles - 1)] == 1)

    @pl.when(valid & (j == 0))
    def _():
        @pl.when(i == 0)
        def _():
            def start(r, carry):
                _row_copy(u_hbm, tok_ref[r], xg_scr.at[0], r, sems.at[0]).start()
                return carry
            lax.fori_loop(0, rows, start, 0)

        pltpu.make_async_copy(u_hbm.at[pl.ds(0, rows)], xg_scr.at[slot], sems.at[slot]).wait()
        x_scr[...] = xg_scr[slot].astype(BF16)

    def step(with_gather):
        if with_gather:
            r0 = j * per_step
            for r in range(per_step):
                _row_copy(u_hbm, tok_next_ref[r0 + r], xg_scr.at[1 - slot], r0 + r, sems.at[1 - slot]).start()
        x = x_scr[...]
        gt = jnp.dot(x, wg_ref[...], preferred_element_type=F32)
        up = jnp.dot(x, wu_ref[...], preferred_element_type=F32)
        act = (gt * jax.nn.sigmoid(gt) * up).astype(BF16)
        y = jnp.dot(act, wd_ref[...], preferred_element_type=F32)

        @pl.when(j == 0)
        def _():
            o_ref[...] = y

        @pl.when(j != 0)
        def _():
            o_ref[...] += y

    gather_now = next_valid & (j < gather_steps)

    @pl.when(valid & gather_now)
    def _():
        step(True)

    @pl.when(valid & jnp.logical_not(gather_now))
    def _():
        step(False)

    @pl.when((tv_ref[i] == 0) & (j == 0))
    def _():
        o_ref[...] = jnp.zeros_like(o_ref)


def _moe(u, slot_tok, tile_e, tile_valid, w_gate, w_up, w_down):
    d = u.shape[1]
    cap = slot_tok.shape[0]
    n_tiles = cap // TM
    f = w_gate.shape[2]
    tf = TF_MOE
    nj = f // tf
    gather_steps = min(MOE_GATHER_STEPS, nj)
    assert TM % gather_steps == 0

    def jj(i, j, tv):
        return jnp.where(tv[i] == 1, j, nj - 1)

    return pl.pallas_call(
        functools.partial(_moe_kernel, gather_steps=gather_steps),
        out_shape=jax.ShapeDtypeStruct((cap, d), F32),
        grid_spec=pltpu.PrefetchScalarGridSpec(
            num_scalar_prefetch=2,
            grid=(n_tiles, nj),
            in_specs=[pl.BlockSpec((TM,), lambda i, j, te, tv: (i,), memory_space=pltpu.SMEM),
                      pl.BlockSpec((TM,), lambda i, j, te, tv: (jnp.minimum(i + 1, n_tiles - 1),),
                                   memory_space=pltpu.SMEM),
                      pl.BlockSpec(memory_space=pl.ANY),
                      pl.BlockSpec((None, d, tf), lambda i, j, te, tv: (te[i], 0, jj(i, j, tv))),
                      pl.BlockSpec((None, d, tf), lambda i, j, te, tv: (te[i], 0, jj(i, j, tv))),
                      pl.BlockSpec((None, tf, d), lambda i, j, te, tv: (te[i], jj(i, j, tv), 0))],
            out_specs=pl.BlockSpec((TM, d), lambda i, j, te, tv: (i, 0)),
            scratch_shapes=[pltpu.VMEM((2, TM, d), F32), pltpu.VMEM((TM, d), BF16),
                            pltpu.SemaphoreType.DMA((2,))]),
        compiler_params=pltpu.CompilerParams(
            dimension_semantics=("arbitrary", "arbitrary"),
            vmem_limit_bytes=_vmem_limit(2 * TM * d * 4 + 2 * TM * d * 4 + 6 * d * tf * 2 + TM * d * 2
                                         + 3 * TM * tf * 4 + TM * d * 4)),
        name="moe_experts",
    )(tile_e, tile_valid, slot_tok, slot_tok, u, w_gate, w_up, w_down)


def _combine_kernel(dest_ref, route_ref, h_ref, gate_ref, fg_ref, yb_hbm, o_ref, y_scr, sem, *, rows):
    def start(r, carry):
        for kk in range(TOP_K):
            _row_copy(yb_hbm, dest_ref[TOP_K * r + kk], y_scr.at[kk], r, sem).start(priority=kk)
        return carry

    lax.fori_loop(0, rows, start, 0)

    def wait(r, carry):
        for kk in range(TOP_K):
            _row_copy(yb_hbm, 0, y_scr.at[kk], 0, sem).wait()
        return carry

    lax.fori_loop(0, rows, wait, 0)

    route = route_ref[...]
    moe = route[:, 2:3] * y_scr[0] + route[:, 3:4] * y_scr[1]
    hl = h_ref[...] + gate_ref[...] * moe
    ms = jnp.mean(hl * hl, axis=-1, keepdims=True)
    o_ref[...] = (hl * lax.rsqrt(ms + NORM_EPS)) * fg_ref[...]


def _combine(yb, dest, route, h, mods, final_g, lat_tile_to_mod):
    n, d = h.shape
    return pl.pallas_call(
        functools.partial(_combine_kernel, rows=TM),
        out_shape=jax.ShapeDtypeStruct((n, d), F32),
        grid=(n // TM,),
        in_specs=[pl.BlockSpec((TOP_K * TM,), lambda i: (i,), memory_space=pltpu.SMEM),
                  pl.BlockSpec((TM, LANES), lambda i: (i, 0)),
                  pl.BlockSpec((TM, d), lambda i: (i, 0)),
                  pl.BlockSpec((None, 1, d), lambda i: (lat_tile_to_mod(i) * N_MOD + 5, 0, 0)),
                  pl.BlockSpec((1, d), lambda i: (0, 0)),
                  pl.BlockSpec(memory_space=pl.ANY)],
        out_specs=pl.BlockSpec((TM, d), lambda i: (i, 0)),
        scratch_shapes=[pltpu.VMEM((TOP_K, TM, d), F32), pltpu.SemaphoreType.DMA],
        compiler_params=pltpu.CompilerParams(
            dimension_semantics=("arbitrary",),
            vmem_limit_bytes=_vmem_limit(TOP_K * TM * d * 4 + 4 * TM * d * 4 + 2 * TM * d * 4)),
        name="moe_combine",
    )(dest, route, h, mods, final_g.reshape(1, d), yb)


def _rope_tables(n_ident, s_len):
    t = jnp.arange(s_len)
    row_id = (t // GRID_W).astype(F32)
    col_id = (t % GRID_W).astype(F32)
    inv_freq = ROPE_BASE ** (-jnp.arange(ROPE_PAIRS, dtype=F32) / ROPE_PAIRS)
    ang_r = row_id[:, None] * inv_freq
    ang_c = col_id[:, None] * inv_freq
    ang = jnp.concatenate([ang_r, ang_r, ang_c, ang_c], axis=-1)
    cos, sin = jnp.cos(ang), jnp.sin(ang)
    first_half = (jnp.arange(HEAD_DIM) % (2 * ROPE_PAIRS)) < ROPE_PAIRS
    sa = jnp.where(first_half, -sin, 0.0)
    sb = jnp.where(first_half, 0.0, sin)
    ones = jnp.ones((n_ident, HEAD_DIM), F32)
    zeros = jnp.zeros((n_ident, HEAD_DIM), F32)
    return (jnp.concatenate([ones, cos]), jnp.concatenate([zeros, sa]), jnp.concatenate([zeros, sb]))


def _routing_plan(route, n_tiles):
    e = route[:, :TOP_K].astype(jnp.int32).reshape(-1)
    onehot = (e[:, None] == jnp.arange(N_EXPERTS, dtype=jnp.int32)[None, :]).astype(jnp.int32)
    csum = jnp.cumsum(onehot, axis=0)
    rank = jnp.sum(csum * onehot, axis=1) - 1
    counts = csum[-1]
    padded = (counts + TM - 1) // TM * TM
    pad_end = jnp.cumsum(padded)
    pad_start = pad_end - padded
    dest = jnp.sum(onehot * pad_start[None, :], axis=1) + rank
    n_valid = pad_end[-1] // TM
    tile_idx = jnp.arange(n_tiles, dtype=jnp.int32)
    tile_valid = (tile_idx < n_valid).astype(jnp.int32)
    tile_row = jnp.minimum(tile_idx, n_valid - 1)
    tile_e = jnp.sum((pad_end[None, :] <= (tile_row * TM)[:, None]).astype(jnp.int32), axis=1)
    tile_e = jnp.minimum(tile_e, N_EXPERTS - 1)
    dest = dest.astype(jnp.int32)
    slot_tok = jnp.zeros((n_tiles * TM,), jnp.int32).at[dest].set(
        jnp.arange(e.shape[0], dtype=jnp.int32) // TOP_K)
    return dest, slot_tok, tile_e.astype(jnp.int32), tile_valid


def _mods(cvecs, w_mod, b_mod):
    d = w_mod.shape[0]
    m = _adaln(cvecs, w_mod, b_mod)[:3]
    return m.reshape(3 * N_MOD, 1, d)


def kernel(x, c, ctx, c_ctx, l0_w_mod, l0_b_mod, l0_norm1_g, l0_w_in, l0_sinks, l0_conv_w, l0_conv_b, l0_w_out, l0_norm2_g, l0_ffn_w_gate, l0_ffn_w_up, l0_ffn_w_down, l1_w_mod, l1_b_mod, l1_norm1_g, l1_w_in, l1_conv_w, l1_conv_b, l1_gate_a_w, l1_gate_a_b, l1_gate_x_w, l1_gate_x_b, l1_lambda, l1_w_out, l1_norm2_g, l1_router_w, l1_router_b, l1_moe_w_gate, l1_moe_w_up, l1_moe_w_down, final_norm_g):
    n_batch, s_len, d = x.shape
    n_ctx = ctx.shape[1]
    assert n_batch == 2 and n_batch * n_ctx == TM and n_ctx == TR
    assert s_len % TM == 0 and s_len % GRID_W == 0 and WINDOW * 2 == TR
    n_ctx_rows = n_batch * n_ctx
    n_lat = n_batch * s_len
    n_ctx_rtiles = n_ctx_rows // TR
    tiles_per_seg = s_len // TR
    mtiles_per_seg = s_len // TM
    d_rnn = l1_w_out.shape[0]
    conv_dim = l0_conv_w.shape[1]
    q_dim = N_Q_HEADS * HEAD_DIM
    kv_dim = N_KV_HEADS * HEAD_DIM

    def tile_to_mod(i):
        return jnp.where(i == 0, n_batch, (i - 1) // mtiles_per_seg)

    def lat_tile_to_mod(i):
        return i // mtiles_per_seg

    def rope_blk(i):
        return jnp.where(i == 0, 0, 1 + (i - 1) % mtiles_per_seg)

    cvecs = jnp.concatenate([c, c_ctx[None, :], jnp.zeros((8 - n_batch - 1, d), F32)], axis=0)
    mods0 = _mods(cvecs, l0_w_mod, l0_b_mod)
    mods1 = _mods(cvecs, l1_w_mod, l1_b_mod)
    ctx2 = ctx.reshape(n_ctx_rows, d)
    x2 = x.reshape(n_lat, d)

    rope = _rope_tables(TM, s_len)
    z = _l0_in_proj(ctx2, x2, l0_norm1_g, mods0, l0_w_in.astype(BF16), rope, tile_to_mod, rope_blk)
    x_col = (q_dim + 2 * kv_dim) // CONV_CW
    per = conv_dim // CONV_CW
    conv = _gated_conv(z, l0_conv_w, l0_conv_b, n_ctx_rtiles, tiles_per_seg,
                       x_col, x_col + per, x_col + 2 * per)
    attn = _attention(z, l0_sinks, n_batch, n_ctx_rtiles, tiles_per_seg)
    hc2, hl = _l0_out_proj(attn, conv, l0_w_out.astype(BF16), ctx2, x2, mods0, tile_to_mod)
    ffn_w = (l0_ffn_w_gate.astype(BF16), l0_ffn_w_up.astype(BF16), l0_ffn_w_down.astype(BF16))
    (hc,) = _ffn(hc2, pl.BlockSpec((None, TM, d), lambda i, j: (0, 0, 0)), n_ctx_rows, TM,
                 l0_norm2_g, mods0, *ffn_w, lambda i: n_batch)
    hl, moe_wg, moe_wu, moe_wd = _ffn(
        hl, pl.BlockSpec((TM, d), lambda i, j: (i, 0)), n_lat, TM, l0_norm2_g, mods0, *ffn_w,
        lat_tile_to_mod, riders=(l1_moe_w_gate, l1_moe_w_up, l1_moe_w_down))

    gg, xr = _l1_in_proj(hc, hl, l1_norm1_g, mods1, l1_w_in.astype(BF16), d_rnn, tile_to_mod)
    hf, hb = _rglru(xr, l1_conv_w, l1_conv_b, l1_gate_a_w.astype(BF16), l1_gate_x_w.astype(BF16),
                    l1_gate_a_b, l1_gate_x_b, l1_lambda, n_batch, n_ctx_rtiles, tiles_per_seg)
    h_lat = _l1_out_proj(hf, hb, gg, l1_w_out.astype(BF16), hl, mods1, n_ctx_rows // TM, lat_tile_to_mod)

    rw_pad = jnp.zeros((d, LANES), BF16).at[:, :N_EXPERTS].set(l1_router_w.astype(BF16))
    rb_pad = jnp.full((1, LANES), NEG_INF, F32).at[0, :N_EXPERTS].set(l1_router_b)
    u, route = _router(h_lat, l1_norm2_g, mods1, rw_pad, rb_pad, lat_tile_to_mod)
    n_tiles = (n_lat * TOP_K) // TM + N_EXPERTS
    dest, slot_tok, tile_e, tile_valid = _routing_plan(route, n_tiles)
    yb = _moe(u, slot_tok, tile_e, tile_valid, moe_wg, moe_wu, moe_wd)
    out = _combine(yb, dest, route, h_lat, mods1, final_norm_g, lat_tile_to_mod)
    return out.reshape(n_batch, s_len, d)
```

```python
import functools

import jax
import jax.numpy as jnp
from jax import lax
from jax.experimental import pallas as pl
from jax.experimental.pallas import tpu as pltpu

F32 = jnp.float32
BF16 = jnp.bfloat16

GRID_W = 64
HEAD_DIM = 128
N_Q_HEADS = 8
N_KV_HEADS = 2
Q_PER_KV = N_Q_HEADS // N_KV_HEADS
WINDOW = 128
ATTN_SCALE = HEAD_DIM ** -0.5
ROPE_BASE = 10000.0
ROPE_PAIRS = HEAD_DIM // 4
SHORT_CONV_OFFSETS = (-1, 0, 1)
RG_CONV_OFFSETS = (-2, -1, 0, 1)
N_RNN_HEADS = 16
RG_C = 8.0
N_EXPERTS = 8
TOP_K = 2
LOG2_E = 1.4426950408889634
NORM_EPS = 1e-6
NEG_INF = -1e30
N_MOD = 6

LANES = 128
TM = 512
CAST_BLOCK = 1024
TR = 256
HALO = 16
TN_CHUNK = 512
TF_FFN = 512
TF_MOE = 1024
MOE_GATHER_STEPS = 4
CONV_CW = 512
MOD_TN = 1024
VMEM_CAP = 56 * 1024 * 1024


def _vmem_limit(nbytes):
    return int(min(max(nbytes * 5 // 4 + (4 << 20), 32 << 20), VMEM_CAP))


def _norm_mod(h, g, shift, scale):
    ms = jnp.mean(h * h, axis=-1, keepdims=True)
    y = h * lax.rsqrt(ms + NORM_EPS)
    return (y * g) * (1.0 + scale) + shift


def _adaln_kernel(c_ref, w_ref, b_ref, o_ref):
    c = c_ref[...]
    s = (c * jax.nn.sigmoid(c)).astype(BF16)
    o_ref[...] = jnp.dot(s, w_ref[...].astype(BF16), preferred_element_type=F32) + b_ref[...]


def _adaln(cvecs, w_mod, b_mod):
    d, n = w_mod.shape
    return pl.pallas_call(
        _adaln_kernel,
        out_shape=jax.ShapeDtypeStruct((8, n), F32),
        grid=(n // MOD_TN,),
        in_specs=[pl.BlockSpec((8, d), lambda j: (0, 0)),
                  pl.BlockSpec((d, MOD_TN), lambda j: (0, j)),
                  pl.BlockSpec((1, MOD_TN), lambda j: (0, j))],
        out_specs=pl.BlockSpec((8, MOD_TN), lambda j: (0, j)),
        compiler_params=pltpu.CompilerParams(
            dimension_semantics=("arbitrary",),
            vmem_limit_bytes=_vmem_limit(2 * d * MOD_TN * 4 + d * MOD_TN * 2)),
        name="adaln",
    )(cvecs, w_mod, b_mod.reshape(1, n))


def _mod_spec(chunk, width, tile_to_mod):
    return pl.BlockSpec((None, 1, width), lambda i, j: (tile_to_mod(i) * N_MOD + chunk, 0, 0))


def _mod_spec1(chunk, width, tile_to_mod):
    return pl.BlockSpec((None, 1, width), lambda i: (tile_to_mod(i) * N_MOD + chunk, 0, 0))


def _resident(shape):
    return pl.BlockSpec(shape, lambda i: (0,) * len(shape), pipeline_mode=pl.Buffered(1))


def _l0_in_kernel(ctx_ref, x_ref, g_ref, sh_ref, sc_ref, w_ref, cos_ref, sa_ref, sb_ref, o_ref):
    i = pl.program_id(0)
    h = jnp.where(i == 0, ctx_ref[...], x_ref[...])
    u = _norm_mod(h, g_ref[...], sh_ref[...], sc_ref[...]).astype(BF16)
    cos, sa, sb = cos_ref[...], sa_ref[...], sb_ref[...]
    n_rot = N_Q_HEADS + N_KV_HEADS
    n = o_ref.shape[1]
    for c0 in range(0, n, TN_CHUNK):
        z = jnp.dot(u, w_ref[:, c0:c0 + TN_CHUNK], preferred_element_type=F32)
        if c0 >= n_rot * HEAD_DIM:
            o_ref[:, c0:c0 + TN_CHUNK] = z.astype(BF16)
            continue
        for k in range(TN_CHUNK // HEAD_DIM):
            hh = c0 // HEAD_DIM + k
            t = z[:, k * HEAD_DIM:(k + 1) * HEAD_DIM]
            if hh < n_rot:
                t = (t * cos + pltpu.roll(t, HEAD_DIM - ROPE_PAIRS, 1) * sa
                     + pltpu.roll(t, ROPE_PAIRS, 1) * sb)
            if hh < N_Q_HEADS:
                t = t * ATTN_SCALE
            o_ref[:, hh * HEAD_DIM:(hh + 1) * HEAD_DIM] = t.astype(BF16)


def _l0_in_proj(ctx2, x2, g, mods, w_in, rope, tile_to_mod, rope_blk):
    n_ctx_rows, d = ctx2.shape
    assert n_ctx_rows == TM
    r = n_ctx_rows + x2.shape[0]
    n = w_in.shape[1]
    assert n % TN_CHUNK == 0
    cos, sa, sb = rope
    rope_spec = pl.BlockSpec((TM, HEAD_DIM), lambda i: (rope_blk(i), 0))
    return pl.pallas_call(
        _l0_in_kernel,
        out_shape=jax.ShapeDtypeStruct((r, n), BF16),
        grid=(r // TM,),
        in_specs=[_resident((TM, d)),
                  pl.BlockSpec((TM, d), lambda i: (jnp.maximum(i - 1, 0), 0)),
                  pl.BlockSpec((1, d), lambda i: (0, 0)),
                  _mod_spec1(0, d, tile_to_mod),
                  _mod_spec1(1, d, tile_to_mod),
                  _resident((d, n)),
                  rope_spec, rope_spec, rope_spec],
        out_specs=pl.BlockSpec((TM, n), lambda i: (i, 0)),
        compiler_params=pltpu.CompilerParams(
            dimension_semantics=("arbitrary",),
            vmem_limit_bytes=_vmem_limit(3 * TM * d * 4 + d * n * 2 + 2 * TM * n * 2
                                         + 2 * TM * d * 4 + 4 * TM * TN_CHUNK * 4)),
        name="l0_in_proj",
    )(ctx2, x2, g.reshape(1, d), mods, mods, w_in, cos, sa, sb)


def _conv_taps(x, xp, xn, first, last, w, b, offsets):
    xp = jnp.where(first, 0.0, xp)
    xn = jnp.where(last, 0.0, xn)
    tr, cw = x.shape
    row8 = lax.broadcasted_iota(jnp.int32, (8, cw), 0)
    acc = jnp.broadcast_to(b, (tr, cw))
    for k, off in enumerate(offsets):
        wk = w[k:k + 1, :]
        if off == 0:
            y = x
        elif off < 0:
            s = -off
            r = pltpu.roll(x, s, 0)
            rp = pltpu.roll(xp, s, 0)[0:8]
            head = jnp.where(row8 < s, rp, r[0:8])
            y = jnp.concatenate([head, r[8:]], axis=0)
        else:
            r = pltpu.roll(x, tr - off, 0)
            rn = pltpu.roll(xn, HALO - off, 0)[HALO - 8:HALO]
            tail = jnp.where(row8 >= 8 - off, rn, r[tr - 8:])
            y = jnp.concatenate([r[:tr - 8], tail], axis=0)
        acc = acc + wk * y
    return acc


def _gated_conv_kernel(x_ref, bg_ref, cg_ref, xp_ref, cgp_ref, xn_ref, cgn_ref, w_ref, b_ref, o_ref,
                       *, n_ctx_tiles, tiles_per_seg):
    i = pl.program_id(0)
    li = i - n_ctx_tiles
    is_ctx = i < n_ctx_tiles
    first = is_ctx | (li % tiles_per_seg == 0)
    last = is_ctx | (li % tiles_per_seg == tiles_per_seg - 1)
    x = x_ref[...].astype(F32) * cg_ref[...].astype(F32)
    xp = xp_ref[...].astype(F32) * cgp_ref[...].astype(F32)
    xn = xn_ref[...].astype(F32) * cgn_ref[...].astype(F32)
    acc = _conv_taps(x, xp, xn, first, last, w_ref[...], b_ref[...], SHORT_CONV_OFFSETS)
    o_ref[...] = (acc * bg_ref[...].astype(F32)).astype(o_ref.dtype)


def _gated_conv(z, w, b, n_ctx_tiles, tiles_per_seg, x_col, bg_col, cg_col):
    r = z.shape[0]
    c = w.shape[1]
    nh = TR // HALO
    n_halo = r // HALO

    def main(o):
        return pl.BlockSpec((TR, CONV_CW), lambda i, j: (i, o + j))

    def prev(o):
        return pl.BlockSpec((HALO, CONV_CW), lambda i, j: (jnp.maximum(i * nh - 1, 0), o + j))

    def nxt(o):
        return pl.BlockSpec((HALO, CONV_CW), lambda i, j: (jnp.minimum((i + 1) * nh, n_halo - 1), o + j))

    return pl.pallas_call(
        functools.partial(_gated_conv_kernel, n_ctx_tiles=n_ctx_tiles, tiles_per_seg=tiles_per_seg),
        out_shape=jax.ShapeDtypeStruct((r, c), BF16),
        grid=(r // TR, c // CONV_CW),
        in_specs=[main(x_col), main(bg_col), main(cg_col), prev(x_col), prev(cg_col), nxt(x_col), nxt(cg_col),
                  pl.BlockSpec((len(SHORT_CONV_OFFSETS), CONV_CW), lambda i, j: (0, j)),
                  pl.BlockSpec((1, CONV_CW), lambda i, j: (0, j))],
        out_specs=pl.BlockSpec((TR, CONV_CW), lambda i, j: (i, j)),
        compiler_params=pltpu.CompilerParams(dimension_semantics=("arbitrary", "arbitrary")),
        name="gated_conv",
    )(z, z, z, z, z, z, z, w, b.reshape(1, c))


def _attn_kernel(sink_ref, q_ref, kc_ref, vc_ref, kp_ref, ko_ref, kn_ref, vp_ref, vo_ref, vn_ref, o_ref,
                 *, tiles_per_seg):
    t = pl.program_id(1)
    tt = t - 1
    tr = q_ref.shape[0]
    n_ctx = kc_ref.shape[0]
    halo = kp_ref.shape[0]
    n_win = tr + 2 * halo
    qi = lax.broadcasted_iota(jnp.int32, (tr, n_win), 0)
    c = lax.broadcasted_iota(jnp.int32, (tr, n_win), 1)
    ok = (c >= qi) & (c <= qi + 2 * WINDOW)
    ok = ok & ((c >= halo) | (tt > 0)) & ((c < halo + tr) | (tt < tiles_per_seg - 1)) & (t > 0)
    bias = jnp.concatenate([jnp.zeros((tr, n_ctx), F32), jnp.where(ok, 0.0, NEG_INF)], axis=1)
    for hk in range(N_KV_HEADS):
        cs = slice(hk * HEAD_DIM, (hk + 1) * HEAD_DIM)
        k_all = jnp.concatenate([kc_ref[:, cs], kp_ref[:, cs], ko_ref[:, cs], kn_ref[:, cs]], axis=0)
        v_all = jnp.concatenate([vc_ref[:, cs], vp_ref[:, cs], vo_ref[:, cs], vn_ref[:, cs]], axis=0)
        for g in range(Q_PER_KV):
            hq = hk * Q_PER_KV + g
            qs = slice(hq * HEAD_DIM, (hq + 1) * HEAD_DIM)
            s = lax.dot_general(q_ref[:, qs], k_all, (((1,), (1,)), ((), ())),
                                preferred_element_type=F32) + bias
            sink = sink_ref[hq]
            m = jnp.maximum(jnp.max(s, axis=-1, keepdims=True), sink)
            p = jnp.exp(s - m)
            denom = jnp.sum(p, axis=-1, keepdims=True) + jnp.exp(sink - m)
            o = jnp.dot(p.astype(BF16), v_all, preferred_element_type=F32)
            o_ref[:, qs] = (o / denom).astype(o_ref.dtype)


def _attention(z, sinks, n_batch, n_ctx_tiles, tiles_per_seg):
    r = z.shape[0]
    q_dim = N_Q_HEADS * HEAD_DIM
    kv_dim = N_KV_HEADS * HEAD_DIM
    k_col = q_dim // kv_dim
    v_col = k_col + 1
    halo = WINDOW
    per = TR // halo
    n_halo_blk = r // halo

    def qblk(b, t):
        return jnp.where(t == 0, b, n_ctx_tiles + b * tiles_per_seg + t - 1)

    def own(col):
        return pl.BlockSpec((TR, kv_dim), lambda b, t: (qblk(b, t), col))

    def ctx(col):
        return pl.BlockSpec((TR, kv_dim), lambda b, t: (b, col))

    def prev(col):
        return pl.BlockSpec((halo, kv_dim), lambda b, t: (jnp.maximum(qblk(b, t) * per - 1, 0), col))

    def nxt(col):
        return pl.BlockSpec((halo, kv_dim),
                            lambda b, t: (jnp.minimum((qblk(b, t) + 1) * per, n_halo_blk - 1), col))

    return pl.pallas_call(
        functools.partial(_attn_kernel, tiles_per_seg=tiles_per_seg),
        out_shape=jax.ShapeDtypeStruct((r, q_dim), BF16),
        grid=(n_batch, 1 + tiles_per_seg),
        in_specs=[pl.BlockSpec(memory_space=pltpu.SMEM),
                  pl.BlockSpec((TR, q_dim), lambda b, t: (qblk(b, t), 0)),
                  ctx(k_col), ctx(v_col),
                  prev(k_col), own(k_col), nxt(k_col),
                  prev(v_col), own(v_col), nxt(v_col)],
        out_specs=pl.BlockSpec((TR, q_dim), lambda b, t: (qblk(b, t), 0)),
        compiler_params=pltpu.CompilerParams(dimension_semantics=("arbitrary", "arbitrary")),
        name="attention",
    )(sinks, z, z, z, z, z, z, z, z, z)


def _l0_out_kernel(ctx_ref, x_ref, a1_ref, a2_ref, w_ref, gate_ref, oc_ref, ol_ref):
    i = pl.program_id(0)
    lhs = jnp.concatenate([a1_ref[...], a2_ref[...]], axis=1)
    d = ol_ref.shape[1]
    for c0 in range(0, d, TN_CHUNK):
        cs = slice(c0, c0 + TN_CHUNK)
        y = jnp.dot(lhs, w_ref[:, cs], preferred_element_type=F32)
        h = jnp.where(i == 0, ctx_ref[:, cs], x_ref[:, cs])
        res = h + gate_ref[:, cs] * y
        oc_ref[:, cs] = res
        ol_ref[:, cs] = res


def _l0_out_proj(attn, conv, w_out, ctx2, x2, mods, tile_to_mod):
    n_ctx_rows, d = ctx2.shape
    n_lat = x2.shape[0]
    r = n_ctx_rows + n_lat
    k1 = attn.shape[1]
    k2 = conv.shape[1]
    assert w_out.shape[0] == k1 + k2 and d % TN_CHUNK == 0
    return pl.pallas_call(
        _l0_out_kernel,
        out_shape=(jax.ShapeDtypeStruct((2, TM, d), F32), jax.ShapeDtypeStruct((n_lat, d), F32)),
        grid=(r // TM,),
        in_specs=[_resident((TM, d)),
                  pl.BlockSpec((TM, d), lambda i: (jnp.maximum(i - 1, 0), 0)),
                  pl.BlockSpec((TM, k1), lambda i: (i, 0)),
                  pl.BlockSpec((TM, k2), lambda i: (i, 0)),
                  _resident((k1 + k2, d)),
                  _mod_spec1(2, d, tile_to_mod)],
        out_specs=(pl.BlockSpec((None, TM, d), lambda i: (jnp.minimum(i, 1), 0, 0)),
                   pl.BlockSpec((TM, d), lambda i: (jnp.maximum(i - 1, 0), 0))),
        compiler_params=pltpu.CompilerParams(
            dimension_semantics=("arbitrary",),
            vmem_limit_bytes=_vmem_limit(3 * TM * d * 4 + 3 * TM * (k1 + k2) * 2 + (k1 + k2) * d * 2
                                         + 4 * TM * d * 4 + 4 * TM * TN_CHUNK * 4)),
        name="l0_out_proj",
    )(ctx2, x2, attn, conv, w_out, mods)


def _ffn_kernel(*refs, cast_blocks):
    n_cast = len(cast_blocks)
    h_ref, g_ref, sh_ref, sc_ref, gate_ref, wg_ref, wu_ref, wd_ref = refs[:8]
    src = refs[8:8 + n_cast]
    o_ref = refs[8 + n_cast]
    dst = refs[9 + n_cast:9 + 2 * n_cast]
    u_scr = refs[9 + 2 * n_cast]
    i = pl.program_id(0)
    j = pl.program_id(1)
    nj = pl.num_programs(1)
    cb = CAST_BLOCK

    if n_cast:
        cin, cout, sem_in, sem_out = refs[10 + 2 * n_cast:]
        n_blocks = sum(ne * nbr * nbc for ne, nbr, nbc in cast_blocks)
        s = i * nj + j
        slot = s % 2

        def for_block(b, fn):
            off = 0
            for k, (ne, nbr, nbc) in enumerate(cast_blocks):
                nk = ne * nbr * nbc

                @pl.when((b >= off) & (b < off + nk))
                def _(k=k, off=off, nbr=nbr, nbc=nbc):
                    q = b - off
                    rc = q % (nbr * nbc)
                    fn(k, q // (nbr * nbc), rc // nbc, rc % nbc)
                off += nk

        def block_of(ref, e, r, c):
            return ref.at[e, pl.ds(r * cb, cb), pl.ds(c * cb, cb)]

        def start_in(b, to_slot):
            for_block(b, lambda k, e, r, c: pltpu.make_async_copy(
                block_of(src[k], e, r, c), cin.at[to_slot], sem_in.at[to_slot]).start(priority=1))

        @pl.when(s == 0)
        def _():
            start_in(0, 0)

        @pl.when(s < n_blocks)
        def _():
            pltpu.make_async_copy(block_of(src[0], 0, 0, 0), cin.at[slot], sem_in.at[slot]).wait()

        @pl.when(s + 1 < n_blocks)
        def _():
            start_in(s + 1, 1 - slot)

        @pl.when((s >= 2) & (s - 2 < n_blocks))
        def _():
            pltpu.make_async_copy(cout.at[slot], block_of(dst[0], 0, 0, 0), sem_out.at[slot]).wait()

    @pl.when(j == 0)
    def _():
        u_scr[...] = _norm_mod(h_ref[...], g_ref[...], sh_ref[...], sc_ref[...]).astype(BF16)

    if n_cast:
        cout[slot] = cin[slot].astype(BF16)
    u = u_scr[...]
    gt = jnp.dot(u, wg_ref[...], preferred_element_type=F32)
    up = jnp.dot(u, wu_ref[...], preferred_element_type=F32)
    act = (gt * jax.nn.sigmoid(gt) * up).astype(BF16)
    y = jnp.dot(act, wd_ref[...], preferred_element_type=F32)

    @pl.when(j == 0)
    def _():
        o_ref[...] = y

    @pl.when(j != 0)
    def _():
        o_ref[...] += y

    @pl.when(j == nj - 1)
    def _():
        o_ref[...] = h_ref[...] + gate_ref[...] * o_ref[...]

    if n_cast:
        @pl.when(s < n_blocks)
        def _():
            for_block(s, lambda k, e, r, c: pltpu.make_async_copy(
                cout.at[slot], block_of(dst[k], e, r, c), sem_out.at[slot]).start(priority=1))


def _ffn(h, h_spec, rows, tm, g, mods, w_gate, w_up, w_down, tile_to_mod, riders=()):
    d = h.shape[-1]
    f = w_gate.shape[1]
    tf = TF_FFN
    grid = (rows // tm, f // tf)
    cast_blocks = tuple((w.shape[0], w.shape[1] // CAST_BLOCK, w.shape[2] // CAST_BLOCK) for w in riders)
    for w in riders:
        assert w.shape[1] % CAST_BLOCK == 0 and w.shape[2] % CAST_BLOCK == 0
    n_blocks = sum(ne * nbr * nbc for ne, nbr, nbc in cast_blocks)
    assert not riders or grid[0] * grid[1] >= n_blocks + 2
    any_spec = pl.BlockSpec(memory_space=pl.ANY)
    scratch = [pltpu.VMEM((tm, d), BF16)]
    if riders:
        scratch += [pltpu.VMEM((2, CAST_BLOCK, CAST_BLOCK), F32), pltpu.VMEM((2, CAST_BLOCK, CAST_BLOCK), BF16),
                    pltpu.SemaphoreType.DMA((2,)), pltpu.SemaphoreType.DMA((2,))]
    return pl.pallas_call(
        functools.partial(_ffn_kernel, cast_blocks=cast_blocks),
        out_shape=(jax.ShapeDtypeStruct((rows, d), F32),
                   *[jax.ShapeDtypeStruct(w.shape, BF16) for w in riders]),
        grid=grid,
        in_specs=[h_spec,
                  pl.BlockSpec((1, d), lambda i, j: (0, 0)),
                  _mod_spec(3, d, tile_to_mod),
                  _mod_spec(4, d, tile_to_mod),
                  _mod_spec(5, d, tile_to_mod),
                  pl.BlockSpec((d, tf), lambda i, j: (0, j)),
                  pl.BlockSpec((d, tf), lambda i, j: (0, j)),
                  pl.BlockSpec((tf, d), lambda i, j: (j, 0))] + [any_spec] * len(riders),
        out_specs=(pl.BlockSpec((tm, d), lambda i, j: (i, 0)), *[any_spec] * len(riders)),
        scratch_shapes=scratch,
        compiler_params=pltpu.CompilerParams(
            dimension_semantics=("arbitrary", "arbitrary"),
            vmem_limit_bytes=_vmem_limit(4 * tm * d * 4 + 6 * d * tf * 2 + tm * d * 2
                                         + 3 * tm * tf * 4 + tm * tf * 2
                                         + (12 * CAST_BLOCK * CAST_BLOCK if riders else 0))),
        name="ffn",
    )(h, g.reshape(1, d), mods, mods, mods, w_gate, w_up, w_down, *riders)


def _l1_in_kernel(hc_ref, hl_ref, g_ref, sh_ref, sc_ref, w_ref, gg_ref, xr_ref):
    h = jnp.where(pl.program_id(0) == 0, hc_ref[...], hl_ref[...])
    u = _norm_mod(h, g_ref[...], sh_ref[...], sc_ref[...]).astype(BF16)
    d_rnn = gg_ref.shape[1]
    for c0 in range(0, 2 * d_rnn, TN_CHUNK):
        z = jnp.dot(u, w_ref[:, c0:c0 + TN_CHUNK], preferred_element_type=F32)
        if c0 < d_rnn:
            gg_ref[:, c0:c0 + TN_CHUNK] = jax.nn.gelu(z, approximate=True).astype(gg_ref.dtype)
        else:
            xr_ref[:, c0 - d_rnn:c0 - d_rnn + TN_CHUNK] = z


def _l1_in_proj(h_ctx, h_lat, g, mods, w_in, d_rnn, tile_to_mod):
    n_ctx_rows, d = h_ctx.shape
    assert n_ctx_rows == TM
    r = n_ctx_rows + h_lat.shape[0]
    assert w_in.shape[1] == 2 * d_rnn and d_rnn % TN_CHUNK == 0
    return pl.pallas_call(
        _l1_in_kernel,
        out_shape=(jax.ShapeDtypeStruct((r, d_rnn), BF16), jax.ShapeDtypeStruct((r, d_rnn), F32)),
        grid=(r // TM,),
        in_specs=[_resident((TM, d)),
                  pl.BlockSpec((TM, d), lambda i: (jnp.maximum(i - 1, 0), 0)),
                  pl.BlockSpec((1, d), lambda i: (0, 0)),
                  _mod_spec1(0, d, tile_to_mod),
                  _mod_spec1(1, d, tile_to_mod),
                  _resident((d, 2 * d_rnn))],
        out_specs=(pl.BlockSpec((TM, d_rnn), lambda i: (i, 0)),
                   pl.BlockSpec((TM, d_rnn), lambda i: (i, 0))),
        compiler_params=pltpu.CompilerParams(
            dimension_semantics=("arbitrary",),
            vmem_limit_bytes=_vmem_limit(2 * TM * d * 4 + d * 2 * d_rnn * 2 + 2 * TM * d_rnn * 6
                                         + 2 * TM * d * 4 + 4 * TM * TN_CHUNK * 4)),
        name="l1_in_proj",
    )(h_ctx, h_lat, g.reshape(1, d), mods, mods, w_in)


def _rglru_kernel(xf_ref, xfp_ref, xfn_ref, xb_ref, xbp_ref, xbn_ref, cw_ref, cb_ref,
                  gaw_ref, gxw_ref, gab_ref, gxb_ref, lam_ref, hf_ref, hb_ref,
                  xc_scr, a_scr, b_scr, h_scr, *, tiles_per_seg):
    c = pl.program_id(1)
    t_len, d_rnn = xf_ref.shape
    hd = d_rnn // N_RNN_HEADS

    @pl.when(c == 0)
    def _():
        h_scr[...] = jnp.zeros_like(h_scr)

    is_ctx = c == 0
    chunk = (c - 1, tiles_per_seg - c)
    halos = ((xfp_ref, xfn_ref), (xbp_ref, xbn_ref))

    for z, x_ref in ((0, xf_ref), (1, xb_ref)):
        first = is_ctx | (chunk[z] == 0)
        last = is_ctx | (chunk[z] == tiles_per_seg - 1)
        xc_scr[...] = _conv_taps(x_ref[...], halos[z][0][...], halos[z][1][...], first, last,
                                 cw_ref[...], cb_ref[...], RG_CONV_OFFSETS)
        x_ref = xc_scr
        neg_lam = -lam_ref[z]
        sp = jnp.maximum(neg_lam, 0.0) + jnp.log1p(jnp.exp(-jnp.abs(neg_lam)))
        half_k = sp * (-0.5 * RG_C * LOG2_E)
        for hh in range(N_RNN_HEADS):
            sl = slice(hh * hd, (hh + 1) * hd)
            xh = x_ref[:, sl]
            xh16 = xh.astype(BF16)
            ta = jnp.tanh(jnp.dot(xh16, gaw_ref[z, hh], preferred_element_type=F32) + gab_ref[z][:, sl])
            ti = jnp.tanh(jnp.dot(xh16, gxw_ref[z, hh], preferred_element_type=F32) + gxb_ref[z][:, sl])
            hk = half_k[:, sl]
            a = jnp.exp2(ta * hk + hk)
            om = 1.0 - a * a
            root = jnp.where(om > 0.0, om * lax.rsqrt(om), 0.0)
            rx = (0.5 * root) * xh
            a_scr[z, :, sl] = a
            b_scr[z, :, sl] = rx * ti + rx

    row8 = lax.broadcasted_iota(jnp.int32, (8, d_rnn), 0)
    n_grp = t_len // 8

    def fwd_group(r0, h):
        a = a_scr[0, pl.ds(r0, 8), :]
        b = b_scr[0, pl.ds(r0, 8), :]
        for s in (1, 2, 4):
            a_sh = jnp.where(row8 >= s, pltpu.roll(a, s, 0), 1.0)
            b_sh = jnp.where(row8 >= s, pltpu.roll(b, s, 0), 0.0)
            b = a * b_sh + b
            a = a * a_sh
        out = a * h + b
        return out, jnp.broadcast_to(out[7:8, :], (8, d_rnn))

    def bwd_group(r0, h):
        a = a_scr[1, pl.ds(r0, 8), :]
        b = b_scr[1, pl.ds(r0, 8), :]
        for s in (1, 2, 4):
            a_sh = jnp.where(row8 < 8 - s, pltpu.roll(a, 8 - s, 0), 1.0)
            b_sh = jnp.where(row8 < 8 - s, pltpu.roll(b, 8 - s, 0), 0.0)
            b = a * b_sh + b
            a = a * a_sh
        out = a * h + b
        return out, jnp.broadcast_to(out[0:1, :], (8, d_rnn))

    def fwd_body(g, h):
        r0 = pl.multiple_of(g * 16, 16)
        lo, h = fwd_group(r0, h)
        hi, h = fwd_group(r0 + 8, h)
        hf_ref[pl.ds(r0, 16), :] = jnp.concatenate([lo, hi], axis=0).astype(hf_ref.dtype)
        return h

    def bwd_body(k, h):
        r0 = pl.multiple_of((n_grp // 2 - 1 - k) * 16, 16)
        hi, h = bwd_group(r0 + 8, h)
        lo, h = bwd_group(r0, h)
        hb_ref[pl.ds(r0, 16), :] = jnp.concatenate([lo, hi], axis=0).astype(hb_ref.dtype)
        return h

    h_scr[0] = lax.fori_loop(0, n_grp // 2, fwd_body, h_scr[0])
    h_scr[1] = lax.fori_loop(0, n_grp // 2, bwd_body, h_scr[1])


def _rglru(xr, conv_w, conv_b, ga_w, gx_w, ga_b, gx_b, lam, n_batch, n_ctx_tiles, tiles_per_seg):
    r, d_rnn = xr.shape
    n_lat = n_batch * tiles_per_seg * TR
    hd = d_rnn // N_RNN_HEADS
    nt = tiles_per_seg
    nh = TR // HALO
    n_halo = r // HALO
    n_taps = conv_w.shape[0]

    def fblk(b, c):
        return jnp.where(c == 0, b, n_ctx_tiles + b * nt + c - 1)

    def bblk(b, c):
        return jnp.where(c == 0, b, n_ctx_tiles + b * nt + nt - c)

    def chunk_specs(blk):
        return [pl.BlockSpec((TR, d_rnn), lambda b, c: (blk(b, c), 0)),
                pl.BlockSpec((HALO, d_rnn), lambda b, c: (jnp.maximum(blk(b, c) * nh - 1, 0), 0)),
                pl.BlockSpec((HALO, d_rnn), lambda b, c: (jnp.minimum((blk(b, c) + 1) * nh, n_halo - 1), 0))]

    def full(shape):
        return pl.BlockSpec(shape, lambda b, c: (0,) * len(shape))

    return pl.pallas_call(
        functools.partial(_rglru_kernel, tiles_per_seg=nt),
        out_shape=(jax.ShapeDtypeStruct((n_lat, d_rnn), BF16), jax.ShapeDtypeStruct((n_lat, d_rnn), BF16)),
        grid=(n_batch, 1 + nt),
        in_specs=chunk_specs(fblk) + chunk_specs(bblk) + [
            full((n_taps, d_rnn)), full((1, d_rnn)),
            full((2, N_RNN_HEADS, hd, hd)), full((2, N_RNN_HEADS, hd, hd)),
            full((2, 1, d_rnn)), full((2, 1, d_rnn)), full((2, 1, d_rnn))],
        out_specs=(pl.BlockSpec((TR, d_rnn), lambda b, c: (b * nt + jnp.maximum(c - 1, 0), 0)),
                   pl.BlockSpec((TR, d_rnn), lambda b, c: (b * nt + jnp.where(c == 0, nt - 1, nt - c), 0))),
        scratch_shapes=[pltpu.VMEM((TR, d_rnn), F32),
                        pltpu.VMEM((2, TR, d_rnn), F32), pltpu.VMEM((2, TR, d_rnn), F32),
                        pltpu.VMEM((2, 8, d_rnn), F32)],
        compiler_params=pltpu.CompilerParams(
            dimension_semantics=("arbitrary", "arbitrary"),
            vmem_limit_bytes=_vmem_limit(14 * TR * d_rnn * 4 + 8 * N_RNN_HEADS * hd * hd * 2)),
        name="rglru",
    )(xr, xr, xr, xr, xr, xr, conv_w, conv_b.reshape(1, d_rnn), ga_w, gx_w,
      ga_b.reshape(2, 1, d_rnn), gx_b.reshape(2, 1, d_rnn), lam.reshape(2, 1, d_rnn))


def _l1_out_kernel(hf_ref, hb_ref, gg_ref, w_ref, h_ref, gate_ref, o_ref):
    rec = hf_ref[...].astype(F32) + hb_ref[...].astype(F32)
    lhs = (rec * gg_ref[...].astype(F32)).astype(BF16)
    d = o_ref.shape[1]
    for c0 in range(0, d, TN_CHUNK):
        cs = slice(c0, c0 + TN_CHUNK)
        y = jnp.dot(lhs, w_ref[:, cs], preferred_element_type=F32)
        o_ref[:, cs] = h_ref[:, cs] + gate_ref[:, cs] * y


def _l1_out_proj(hf, hb, gg, w_out, h, mods, n_ctx_mtiles, lat_tile_to_mod):
    n_lat, d_rnn = hf.shape
    d = h.shape[1]
    assert d % TN_CHUNK == 0
    return pl.pallas_call(
        _l1_out_kernel,
        out_shape=jax.ShapeDtypeStruct((n_lat, d), F32),
        grid=(n_lat // TM,),
        in_specs=[pl.BlockSpec((TM, d_rnn), lambda i: (i, 0)),
                  pl.BlockSpec((TM, d_rnn), lambda i: (i, 0)),
                  pl.BlockSpec((TM, d_rnn), lambda i: (i + n_ctx_mtiles, 0)),
                  _resident((d_rnn, d)),
                  pl.BlockSpec((TM, d), lambda i: (i, 0)),
                  _mod_spec1(2, d, lat_tile_to_mod)],
        out_specs=pl.BlockSpec((TM, d), lambda i: (i, 0)),
        compiler_params=pltpu.CompilerParams(
            dimension_semantics=("arbitrary",),
            vmem_limit_bytes=_vmem_limit(6 * TM * d_rnn * 2 + d_rnn * d * 2 + 4 * TM * d * 4
                                         + 2 * TM * d_rnn * 4 + 4 * TM * TN_CHUNK * 4)),
        name="l1_out_proj",
    )(hf, hb, gg, w_out, h, mods)


def _router_kernel(h_ref, g_ref, sh_ref, sc_ref, rw_ref, rb_ref, u_ref, route_ref):
    u = _norm_mod(h_ref[...], g_ref[...], sh_ref[...], sc_ref[...])
    u_ref[...] = u
    logits = jnp.dot(u.astype(BF16), rw_ref[...], preferred_element_type=F32) + rb_ref[...]
    lane = lax.broadcasted_iota(jnp.int32, logits.shape, 1)
    m1 = jnp.max(logits, axis=-1, keepdims=True)
    i1 = jnp.min(jnp.where(logits == m1, lane, LANES), axis=-1, keepdims=True)
    rest = jnp.where(lane == i1, -jnp.inf, logits)
    m2 = jnp.max(rest, axis=-1, keepdims=True)
    i2 = jnp.min(jnp.where(rest == m2, lane, LANES), axis=-1, keepdims=True)
    e2 = jnp.exp(m2 - m1)
    w1 = 1.0 / (1.0 + e2)
    w2 = e2 * w1
    route = jnp.where(lane == 0, i1.astype(F32),
                      jnp.where(lane == 1, i2.astype(F32),
                                jnp.where(lane == 2, w1, jnp.where(lane == 3, w2, 0.0))))
    route_ref[...] = route


def _router(h, g, mods, rw_pad, rb_pad, lat_tile_to_mod):
    n, d = h.shape

    def mspec(chunk):
        return pl.BlockSpec((None, 1, d), lambda i: (lat_tile_to_mod(i) * N_MOD + chunk, 0, 0))

    return pl.pallas_call(
        _router_kernel,
        out_shape=(jax.ShapeDtypeStruct((n, d), F32), jax.ShapeDtypeStruct((n, LANES), F32)),
        grid=(n // TM,),
        in_specs=[pl.BlockSpec((TM, d), lambda i: (i, 0)),
                  pl.BlockSpec((1, d), lambda i: (0, 0)),
                  mspec(3), mspec(4),
                  pl.BlockSpec((d, LANES), lambda i: (0, 0)),
                  pl.BlockSpec((1, LANES), lambda i: (0, 0))],
        out_specs=(pl.BlockSpec((TM, d), lambda i: (i, 0)),
                   pl.BlockSpec((TM, LANES), lambda i: (i, 0))),
        compiler_params=pltpu.CompilerParams(
            dimension_semantics=("arbitrary",),
            vmem_limit_bytes=_vmem_limit(6 * TM * d * 4)),
        name="router",
    )(h, g.reshape(1, d), mods, mods, rw_pad, rb_pad)


def _row_copy(src, src_row, dst, dst_row, sem):
    return pltpu.make_async_copy(src.at[pl.ds(src_row, 1)], dst.at[pl.ds(dst_row, 1)], sem)


def _moe_kernel(te_ref, tv_ref, tok_ref, tok_next_ref, u_hbm, wg_ref, wu_ref, wd_ref, o_ref,
                xg_scr, x_scr, sems, *, gather_steps):
    del te_ref
    i = pl.program_id(0)
    j = pl.program_id(1)
    n_tiles = pl.num_programs(0)
    rows = x_scr.shape[0]
    per_step = rows // gather_steps
    slot = i % 2
    valid = tv_ref[i] == 1
    next_valid = (i + 1 < n_tiles) & (tv_ref[jnp.minimum(i + 1, n_tiles - 1)] == 1)

    @pl.when(valid & (j == 0))
    def _():
        @pl.when(i == 0)
        def _():
            def start(r, carry):
                _row_copy(u_hbm, tok_ref[r], xg_scr.at[0], r, sems.at[0]).start()
                return carry
            lax.fori_loop(0, rows, start, 0)

        pltpu.make_async_copy(u_hbm.at[pl.ds(0, rows)], xg_scr.at[slot], sems.at[slot]).wait()
        x_scr[...] = xg_scr[slot].astype(BF16)

    def step(with_gather):
        if with_gather:
            r0 = j * per_step
            for r in range(per_step):
                _row_copy(u_hbm, tok_next_ref[r0 + r], xg_scr.at[1 - slot], r0 + r, sems.at[1 - slot]).start()
        x = x_scr[...]
        gt = jnp.dot(x, wg_ref[...], preferred_element_type=F32)
        up = jnp.dot(x, wu_ref[...], preferred_element_type=F32)
        act = (gt * jax.nn.sigmoid(gt) * up).astype(BF16)
        y = jnp.dot(act, wd_ref[...], preferred_element_type=F32)

        @pl.when(j == 0)
        def _():
            o_ref[...] = y

        @pl.when(j != 0)
        def _():
            o_ref[...] += y

    gather_now = next_valid & (j < gather_steps)

    @pl.when(valid & gather_now)
    def _():
        step(True)

    @pl.when(valid & jnp.logical_not(gather_now))
    def _():
        step(False)

    @pl.when((tv_ref[i] == 0) & (j == 0))
    def _():
        o_ref[...] = jnp.zeros_like(o_ref)


def _moe(u, slot_tok, tile_e, tile_valid, w_gate, w_up, w_down):
    d = u.shape[1]
    cap = slot_tok.shape[0]
    n_tiles = cap // TM
    f = w_gate.shape[2]
    tf = TF_MOE
    nj = f // tf
    gather_steps = min(MOE_GATHER_STEPS, nj)
    assert TM % gather_steps == 0

    def jj(i, j, tv):
        return jnp.where(tv[i] == 1, j, nj - 1)

    return pl.pallas_call(
        functools.partial(_moe_kernel, gather_steps=gather_steps),
        out_shape=jax.ShapeDtypeStruct((cap, d), F32),
        grid_spec=pltpu.PrefetchScalarGridSpec(
            num_scalar_prefetch=2,
            grid=(n_tiles, nj),
            in_specs=[pl.BlockSpec((TM,), lambda i, j, te, tv: (i,), memory_space=pltpu.SMEM),
                      pl.BlockSpec((TM,), lambda i, j, te, tv: (jnp.minimum(i + 1, n_tiles - 1),),
                                   memory_space=pltpu.SMEM),
                      pl.BlockSpec(memory_space=pl.ANY),
                      pl.BlockSpec((None, d, tf), lambda i, j, te, tv: (te[i], 0, jj(i, j, tv))),
                      pl.BlockSpec((None, d, tf), lambda i, j, te, tv: (te[i], 0, jj(i, j, tv))),
                      pl.BlockSpec((None, tf, d), lambda i, j, te, tv: (te[i], jj(i, j, tv), 0))],
            out_specs=pl.BlockSpec((TM, d), lambda i, j, te, tv: (i, 0)),
            scratch_shapes=[pltpu.VMEM((2, TM, d), F32), pltpu.VMEM((TM, d), BF16),
                            pltpu.SemaphoreType.DMA((2,))]),
        compiler_params=pltpu.CompilerParams(
            dimension_semantics=("arbitrary", "arbitrary"),
            vmem_limit_bytes=_vmem_limit(2 * TM * d * 4 + 2 * TM * d * 4 + 6 * d * tf * 2 + TM * d * 2
                                         + 3 * TM * tf * 4 + TM * d * 4)),
        name="moe_experts",
    )(tile_e, tile_valid, slot_tok, slot_tok, u, w_gate, w_up, w_down)


def _combine_kernel(dest_ref, route_ref, h_ref, gate_ref, fg_ref, yb_hbm, o_ref, y_scr, sem, *, rows):
    def start(r, carry):
        for kk in range(TOP_K):
            _row_copy(yb_hbm, dest_ref[TOP_K * r + kk], y_scr.at[kk], r, sem).start()
        return carry

    lax.fori_loop(0, rows, start, 0)
    for kk in range(TOP_K):
        pltpu.make_async_copy(yb_hbm.at[pl.ds(0, rows)], y_scr.at[kk], sem).wait()

    route = route_ref[...]
    moe = route[:, 2:3] * y_scr[0] + route[:, 3:4] * y_scr[1]
    hl = h_ref[...] + gate_ref[...] * moe
    ms = jnp.mean(hl * hl, axis=-1, keepdims=True)
    o_ref[...] = (hl * lax.rsqrt(ms + NORM_EPS)) * fg_ref[...]


def _combine(yb, dest, route, h, mods, final_g, lat_tile_to_mod):
    n, d = h.shape
    return pl.pallas_call(
        functools.partial(_combine_kernel, rows=TM),
        out_shape=jax.ShapeDtypeStruct((n, d), F32),
        grid=(n // TM,),
        in_specs=[pl.BlockSpec((TOP_K * TM,), lambda i: (i,), memory_space=pltpu.SMEM),
                  pl.BlockSpec((TM, LANES), lambda i: (i, 0)),
                  pl.BlockSpec((TM, d), lambda i: (i, 0)),
                  pl.BlockSpec((None, 1, d), lambda i: (lat_tile_to_mod(i) * N_MOD + 5, 0, 0)),
                  pl.BlockSpec((1, d), lambda i: (0, 0)),
                  pl.BlockSpec(memory_space=pl.ANY)],
        out_specs=pl.BlockSpec((TM, d), lambda i: (i, 0)),
        scratch_shapes=[pltpu.VMEM((TOP_K, TM, d), F32), pltpu.SemaphoreType.DMA],
        compiler_params=pltpu.CompilerParams(
            dimension_semantics=("arbitrary",),
            vmem_limit_bytes=_vmem_limit(TOP_K * TM * d * 4 + 4 * TM * d * 4 + 2 * TM * d * 4)),
        name="moe_combine",
    )(dest, route, h, mods, final_g.reshape(1, d), yb)


def _rope_tables(n_ident, s_len):
    t = jnp.arange(s_len)
    row_id = (t // GRID_W).astype(F32)
    col_id = (t % GRID_W).astype(F32)
    inv_freq = ROPE_BASE ** (-jnp.arange(ROPE_PAIRS, dtype=F32) / ROPE_PAIRS)
    ang_r = row_id[:, None] * inv_freq
    ang_c = col_id[:, None] * inv_freq
    ang = jnp.concatenate([ang_r, ang_r, ang_c, ang_c], axis=-1)
    cos, sin = jnp.cos(ang), jnp.sin(ang)
    first_half = (jnp.arange(HEAD_DIM) % (2 * ROPE_PAIRS)) < ROPE_PAIRS
    sa = jnp.where(first_half, -sin, 0.0)
    sb = jnp.where(first_half, 0.0, sin)
    ones = jnp.ones((n_ident, HEAD_DIM), F32)
    zeros = jnp.zeros((n_ident, HEAD_DIM), F32)
    return (jnp.concatenate([ones, cos]), jnp.concatenate([zeros, sa]), jnp.concatenate([zeros, sb]))


def _routing_plan(route, n_tiles):
    e = route[:, :TOP_K].astype(jnp.int32).reshape(-1)
    onehot = (e[:, None] == jnp.arange(N_EXPERTS, dtype=jnp.int32)[None, :]).astype(jnp.int32)
    csum = jnp.cumsum(onehot, axis=0)
    rank = jnp.sum(csum * onehot, axis=1) - 1
    counts = csum[-1]
    padded = (counts + TM - 1) // TM * TM
    pad_end = jnp.cumsum(padded)
    pad_start = pad_end - padded
    dest = jnp.sum(onehot * pad_start[None, :], axis=1) + rank
    n_valid = pad_end[-1] // TM
    tile_idx = jnp.arange(n_tiles, dtype=jnp.int32)
    tile_valid = (tile_idx < n_valid).astype(jnp.int32)
    tile_row = jnp.minimum(tile_idx, n_valid - 1)
    tile_e = jnp.sum((pad_end[None, :] <= (tile_row * TM)[:, None]).astype(jnp.int32), axis=1)
    tile_e = jnp.minimum(tile_e, N_EXPERTS - 1)
    dest = dest.astype(jnp.int32)
    slot_tok = jnp.zeros((n_tiles * TM,), jnp.int32).at[dest].set(
        jnp.arange(e.shape[0], dtype=jnp.int32) // TOP_K)
    return dest, slot_tok, tile_e.astype(jnp.int32), tile_valid


def _mods(cvecs, w_mod, b_mod):
    d = w_mod.shape[0]
    m = _adaln(cvecs, w_mod, b_mod)[:3]
    return m.reshape(3 * N_MOD, 1, d)


def kernel(x, c, ctx, c_ctx, l0_w_mod, l0_b_mod, l0_norm1_g, l0_w_in, l0_sinks, l0_conv_w, l0_conv_b, l0_w_out, l0_norm2_g, l0_ffn_w_gate, l0_ffn_w_up, l0_ffn_w_down, l1_w_mod, l1_b_mod, l1_norm1_g, l1_w_in, l1_conv_w, l1_conv_b, l1_gate_a_w, l1_gate_a_b, l1_gate_x_w, l1_gate_x_b, l1_lambda, l1_w_out, l1_norm2_g, l1_router_w, l1_router_b, l1_moe_w_gate, l1_moe_w_up, l1_moe_w_down, final_norm_g):
    n_batch, s_len, d = x.shape
    n_ctx = ctx.shape[1]
    assert n_batch == 2 and n_batch * n_ctx == TM and n_ctx == TR
    assert s_len % TM == 0 and s_len % GRID_W == 0 and WINDOW * 2 == TR
    n_ctx_rows = n_batch * n_ctx
    n_lat = n_batch * s_len
    n_ctx_rtiles = n_ctx_rows // TR
    tiles_per_seg = s_len // TR
    mtiles_per_seg = s_len // TM
    d_rnn = l1_w_out.shape[0]
    conv_dim = l0_conv_w.shape[1]
    q_dim = N_Q_HEADS * HEAD_DIM
    kv_dim = N_KV_HEADS * HEAD_DIM

    def tile_to_mod(i):
        return jnp.where(i == 0, n_batch, (i - 1) // mtiles_per_seg)

    def lat_tile_to_mod(i):
        return i // mtiles_per_seg

    def rope_blk(i):
        return jnp.where(i == 0, 0, 1 + (i - 1) % mtiles_per_seg)

    cvecs = jnp.concatenate([c, c_ctx[None, :], jnp.zeros((8 - n_batch - 1, d), F32)], axis=0)
    mods0 = _mods(cvecs, l0_w_mod, l0_b_mod)
    mods1 = _mods(cvecs, l1_w_mod, l1_b_mod)
    ctx2 = ctx.reshape(n_ctx_rows, d)
    x2 = x.reshape(n_lat, d)

    rope = _rope_tables(TM, s_len)
    z = _l0_in_proj(ctx2, x2, l0_norm1_g, mods0, l0_w_in.astype(BF16), rope, tile_to_mod, rope_blk)
    x_col = (q_dim + 2 * kv_dim) // CONV_CW
    per = conv_dim // CONV_CW
    conv = _gated_conv(z, l0_conv_w, l0_conv_b, n_ctx_rtiles, tiles_per_seg,
                       x_col, x_col + per, x_col + 2 * per)
    attn = _attention(z, l0_sinks, n_batch, n_ctx_rtiles, tiles_per_seg)
    hc2, hl = _l0_out_proj(attn, conv, l0_w_out.astype(BF16), ctx2, x2, mods0, tile_to_mod)
    ffn_w = (l0_ffn_w_gate.astype(BF16), l0_ffn_w_up.astype(BF16), l0_ffn_w_down.astype(BF16))
    (hc,) = _ffn(hc2, pl.BlockSpec((None, TM, d), lambda i, j: (0, 0, 0)), n_ctx_rows, TM,
                 l0_norm2_g, mods0, *ffn_w, lambda i: n_batch)
    hl, moe_wg, moe_wu, moe_wd = _ffn(
        hl, pl.BlockSpec((TM, d), lambda i, j: (i, 0)), n_lat, TM, l0_norm2_g, mods0, *ffn_w,
        lat_tile_to_mod, riders=(l1_moe_w_gate, l1_moe_w_up, l1_moe_w_down))

    gg, xr = _l1_in_proj(hc, hl, l1_norm1_g, mods1, l1_w_in.astype(BF16), d_rnn, tile_to_mod)
    hf, hb = _rglru(xr, l1_conv_w, l1_conv_b, (0.5 * l1_gate_a_w).astype(BF16), (0.5 * l1_gate_x_w).astype(BF16),
                    0.5 * l1_gate_a_b, 0.5 * l1_gate_x_b, l1_lambda, n_batch, n_ctx_rtiles, tiles_per_seg)
    h_lat = _l1_out_proj(hf, hb, gg, l1_w_out.astype(BF16), hl, mods1, n_ctx_rows // TM, lat_tile_to_mod)

    rw_pad = jnp.zeros((d, LANES), BF16).at[:, :N_EXPERTS].set(l1_router_w.astype(BF16))
    rb_pad = jnp.full((1, LANES), NEG_INF, F32).at[0, :N_EXPERTS].set(l1_router_b)
    u, route = _router(h_lat, l1_norm2_g, mods1, rw_pad, rb_pad, lat_tile_to_mod)
    n_tiles = (n_lat * TOP_K) // TM + N_EXPERTS
    dest, slot_tok, tile_e, tile_valid = _routing_plan(route, n_tiles)
    yb = _moe(u, slot_tok, tile_e, tile_valid, moe_wg, moe_wu, moe_wd)
    out = _combine(yb, dest, route, h_lat, mods1, final_norm_g, lat_tile_to_mod)
    return out.reshape(n_batch, s_len, d)
```

```python
import functools

import jax
import jax.numpy as jnp
from jax import lax
from jax.experimental import pallas as pl
from jax.experimental.pallas import tpu as pltpu

F32 = jnp.float32
BF16 = jnp.bfloat16

GRID_W = 64
HEAD_DIM = 128
N_Q_HEADS = 8
N_KV_HEADS = 2
Q_PER_KV = N_Q_HEADS // N_KV_HEADS
WINDOW = 128
ATTN_SCALE = HEAD_DIM ** -0.5
ROPE_BASE = 10000.0
ROPE_PAIRS = HEAD_DIM // 4
SHORT_CONV_OFFSETS = (-1, 0, 1)
RG_CONV_OFFSETS = (-2, -1, 0, 1)
N_RNN_HEADS = 16
RG_C = 8.0
N_EXPERTS = 8
TOP_K = 2
LOG2_E = 1.4426950408889634
NORM_EPS = 1e-6
NEG_INF = -1e30
N_MOD = 6

LANES = 128
TM = 512
FFN_RIDER_BLOCK = (1024, 1024)
ATTN_RIDER_BLOCK = (1024, 2048)
TR = 256
HALO = 16
TN_CHUNK = 512
TF_FFN = 512
TF_MOE = 1024
MOE_GATHER_STEPS = 4
CONV_CW = 512
MOD_TN = 1024
VMEM_CAP = 56 * 1024 * 1024


def _vmem_limit(nbytes):
    return int(min(max(nbytes * 5 // 4 + (4 << 20), 32 << 20), VMEM_CAP))


def _norm_mod(h, g, shift, scale):
    ms = jnp.mean(h * h, axis=-1, keepdims=True)
    y = h * lax.rsqrt(ms + NORM_EPS)
    return (y * g) * (1.0 + scale) + shift


class _Riders:
    def __init__(self, src, dst, scratch, riders_blocks, blk):
        self.src, self.dst = src, dst
        self.cin, self.cout, self.sem_in, self.sem_out = scratch
        self.blocks = riders_blocks
        self.blk = blk
        self.n_blocks = sum(ne * nbr * nbc for ne, nbr, nbc in riders_blocks)

    @staticmethod
    def plan(riders, blk):
        for w in riders:
            assert w.shape[1] % blk[0] == 0 and w.shape[2] % blk[1] == 0
        blocks = tuple((w.shape[0], w.shape[1] // blk[0], w.shape[2] // blk[1]) for w in riders)
        n_blocks = sum(ne * nbr * nbc for ne, nbr, nbc in blocks)
        scratch = [pltpu.VMEM((2,) + blk, F32), pltpu.VMEM((2,) + blk, BF16),
                   pltpu.SemaphoreType.DMA((2,)), pltpu.SemaphoreType.DMA((2,))]
        return blocks, n_blocks, scratch

    def _for_block(self, b, fn):
        off = 0
        for k, (ne, nbr, nbc) in enumerate(self.blocks):
            nk = ne * nbr * nbc

            @pl.when((b >= off) & (b < off + nk))
            def _(k=k, off=off, nbr=nbr, nbc=nbc):
                q = b - off
                rc = q % (nbr * nbc)
                fn(k, q // (nbr * nbc), rc // nbc, rc % nbc)
            off += nk

    def _block_of(self, ref, e, r, c):
        br, bc = self.blk
        return ref.at[e, pl.ds(r * br, br), pl.ds(c * bc, bc)]

    def _start_in(self, b, to_slot):
        self._for_block(b, lambda k, e, r, c: pltpu.make_async_copy(
            self._block_of(self.src[k], e, r, c), self.cin.at[to_slot], self.sem_in.at[to_slot]
        ).start(priority=1))

    def before(self, s):
        self.s, self.slot = s, s % 2
        slot, n_blocks = self.slot, self.n_blocks

        @pl.when(s == 0)
        def _():
            self._start_in(0, 0)

        @pl.when(s < n_blocks)
        def _():
            pltpu.make_async_copy(self._block_of(self.src[0], 0, 0, 0), self.cin.at[slot],
                                  self.sem_in.at[slot]).wait()

        @pl.when(s + 1 < n_blocks)
        def _():
            self._start_in(s + 1, 1 - slot)

        @pl.when((s >= 2) & (s - 2 < n_blocks))
        def _():
            pltpu.make_async_copy(self.cout.at[slot], self._block_of(self.dst[0], 0, 0, 0),
                                  self.sem_out.at[slot]).wait()

    def convert(self):
        self.cout[self.slot] = self.cin[self.slot].astype(BF16)

    def after(self):
        s, slot = self.s, self.slot

        @pl.when(s < self.n_blocks)
        def _():
            self._for_block(s, lambda k, e, r, c: pltpu.make_async_copy(
                self.cout.at[slot], self._block_of(self.dst[k], e, r, c), self.sem_out.at[slot]
            ).start(priority=1))


def _adaln_kernel(c_ref, w_ref, b_ref, o_ref):
    c = c_ref[...]
    s = (c * jax.nn.sigmoid(c)).astype(BF16)
    o_ref[...] = jnp.dot(s, w_ref[...].astype(BF16), preferred_element_type=F32) + b_ref[...]


def _adaln(cvecs, w_mod, b_mod):
    d, n = w_mod.shape
    return pl.pallas_call(
        _adaln_kernel,
        out_shape=jax.ShapeDtypeStruct((8, n), F32),
        grid=(n // MOD_TN,),
        in_specs=[pl.BlockSpec((8, d), lambda j: (0, 0)),
                  pl.BlockSpec((d, MOD_TN), lambda j: (0, j)),
                  pl.BlockSpec((1, MOD_TN), lambda j: (0, j))],
        out_specs=pl.BlockSpec((8, MOD_TN), lambda j: (0, j)),
        compiler_params=pltpu.CompilerParams(
            dimension_semantics=("arbitrary",),
            vmem_limit_bytes=_vmem_limit(2 * d * MOD_TN * 4 + d * MOD_TN * 2)),
        name="adaln",
    )(cvecs, w_mod, b_mod.reshape(1, n))


def _mod_spec(chunk, width, tile_to_mod):
    return pl.BlockSpec((None, 1, width), lambda i, j: (tile_to_mod(i) * N_MOD + chunk, 0, 0))


def _mod_spec1(chunk, width, tile_to_mod):
    return pl.BlockSpec((None, 1, width), lambda i: (tile_to_mod(i) * N_MOD + chunk, 0, 0))


def _resident(shape):
    return pl.BlockSpec(shape, lambda i: (0,) * len(shape), pipeline_mode=pl.Buffered(1))


def _l0_in_kernel(ctx_ref, x_ref, g_ref, sh_ref, sc_ref, w_ref, cos_ref, sa_ref, sb_ref, o_ref):
    i = pl.program_id(0)
    h = jnp.where(i == 0, ctx_ref[...], x_ref[...])
    u = _norm_mod(h, g_ref[...], sh_ref[...], sc_ref[...]).astype(BF16)
    cos, sa, sb = cos_ref[...], sa_ref[...], sb_ref[...]
    n_rot = N_Q_HEADS + N_KV_HEADS
    n = o_ref.shape[1]
    for c0 in range(0, n, TN_CHUNK):
        z = jnp.dot(u, w_ref[:, c0:c0 + TN_CHUNK], preferred_element_type=F32)
        if c0 >= n_rot * HEAD_DIM:
            o_ref[:, c0:c0 + TN_CHUNK] = z.astype(BF16)
            continue
        for k in range(TN_CHUNK // HEAD_DIM):
            hh = c0 // HEAD_DIM + k
            t = z[:, k * HEAD_DIM:(k + 1) * HEAD_DIM]
            if hh < n_rot:
                t = (t * cos + pltpu.roll(t, HEAD_DIM - ROPE_PAIRS, 1) * sa
                     + pltpu.roll(t, ROPE_PAIRS, 1) * sb)
            if hh < N_Q_HEADS:
                t = t * ATTN_SCALE
            o_ref[:, hh * HEAD_DIM:(hh + 1) * HEAD_DIM] = t.astype(BF16)


def _l0_in_proj(ctx2, x2, g, mods, w_in, rope, tile_to_mod, rope_blk):
    n_ctx_rows, d = ctx2.shape
    assert n_ctx_rows == TM
    r = n_ctx_rows + x2.shape[0]
    n = w_in.shape[1]
    assert n % TN_CHUNK == 0
    cos, sa, sb = rope
    rope_spec = pl.BlockSpec((TM, HEAD_DIM), lambda i: (rope_blk(i), 0))
    return pl.pallas_call(
        _l0_in_kernel,
        out_shape=jax.ShapeDtypeStruct((r, n), BF16),
        grid=(r // TM,),
        in_specs=[_resident((TM, d)),
                  pl.BlockSpec((TM, d), lambda i: (jnp.maximum(i - 1, 0), 0)),
                  pl.BlockSpec((1, d), lambda i: (0, 0)),
                  _mod_spec1(0, d, tile_to_mod),
                  _mod_spec1(1, d, tile_to_mod),
                  _resident((d, n)),
                  rope_spec, rope_spec, rope_spec],
        out_specs=pl.BlockSpec((TM, n), lambda i: (i, 0)),
        compiler_params=pltpu.CompilerParams(
            dimension_semantics=("arbitrary",),
            vmem_limit_bytes=_vmem_limit(3 * TM * d * 4 + d * n * 2 + 2 * TM * n * 2
                                         + 2 * TM * d * 4 + 4 * TM * TN_CHUNK * 4)),
        name="l0_in_proj",
    )(ctx2, x2, g.reshape(1, d), mods, mods, w_in, cos, sa, sb)


def _conv_taps(x, xp, xn, first, last, w, b, offsets):
    xp = jnp.where(first, 0.0, xp)
    xn = jnp.where(last, 0.0, xn)
    tr, cw = x.shape
    row8 = lax.broadcasted_iota(jnp.int32, (8, cw), 0)
    acc = jnp.broadcast_to(b, (tr, cw))
    for k, off in enumerate(offsets):
        wk = w[k:k + 1, :]
        if off == 0:
            y = x
        elif off < 0:
            s = -off
            r = pltpu.roll(x, s, 0)
            rp = pltpu.roll(xp, s, 0)[0:8]
            head = jnp.where(row8 < s, rp, r[0:8])
            y = jnp.concatenate([head, r[8:]], axis=0)
        else:
            r = pltpu.roll(x, tr - off, 0)
            rn = pltpu.roll(xn, HALO - off, 0)[HALO - 8:HALO]
            tail = jnp.where(row8 >= 8 - off, rn, r[tr - 8:])
            y = jnp.concatenate([r[:tr - 8], tail], axis=0)
        acc = acc + wk * y
    return acc


def _gated_conv_kernel(x_ref, bg_ref, cg_ref, xp_ref, cgp_ref, xn_ref, cgn_ref, w_ref, b_ref, o_ref,
                       *, n_ctx_tiles, tiles_per_seg):
    i = pl.program_id(0)
    li = i - n_ctx_tiles
    is_ctx = i < n_ctx_tiles
    first = is_ctx | (li % tiles_per_seg == 0)
    last = is_ctx | (li % tiles_per_seg == tiles_per_seg - 1)
    x = x_ref[...].astype(F32) * cg_ref[...].astype(F32)
    xp = xp_ref[...].astype(F32) * cgp_ref[...].astype(F32)
    xn = xn_ref[...].astype(F32) * cgn_ref[...].astype(F32)
    acc = _conv_taps(x, xp, xn, first, last, w_ref[...], b_ref[...], SHORT_CONV_OFFSETS)
    o_ref[...] = (acc * bg_ref[...].astype(F32)).astype(o_ref.dtype)


def _gated_conv(z, w, b, n_ctx_tiles, tiles_per_seg, x_col, bg_col, cg_col):
    r = z.shape[0]
    c = w.shape[1]
    nh = TR // HALO
    n_halo = r // HALO

    def main(o):
        return pl.BlockSpec((TR, CONV_CW), lambda i, j: (i, o + j))

    def prev(o):
        return pl.BlockSpec((HALO, CONV_CW), lambda i, j: (jnp.maximum(i * nh - 1, 0), o + j))

    def nxt(o):
        return pl.BlockSpec((HALO, CONV_CW), lambda i, j: (jnp.minimum((i + 1) * nh, n_halo - 1), o + j))

    return pl.pallas_call(
        functools.partial(_gated_conv_kernel, n_ctx_tiles=n_ctx_tiles, tiles_per_seg=tiles_per_seg),
        out_shape=jax.ShapeDtypeStruct((r, c), BF16),
        grid=(r // TR, c // CONV_CW),
        in_specs=[main(x_col), main(bg_col), main(cg_col), prev(x_col), prev(cg_col), nxt(x_col), nxt(cg_col),
                  pl.BlockSpec((len(SHORT_CONV_OFFSETS), CONV_CW), lambda i, j: (0, j)),
                  pl.BlockSpec((1, CONV_CW), lambda i, j: (0, j))],
        out_specs=pl.BlockSpec((TR, CONV_CW), lambda i, j: (i, j)),
        compiler_params=pltpu.CompilerParams(dimension_semantics=("arbitrary", "arbitrary")),
        name="gated_conv",
    )(z, z, z, z, z, z, z, w, b.reshape(1, c))


def _attn_kernel(*refs, tiles_per_seg, rider_blocks):
    n_cast = len(rider_blocks)
    sink_ref, q_ref, kc_ref, vc_ref, kp_ref, ko_ref, kn_ref, vp_ref, vo_ref, vn_ref = refs[:10]
    o_ref = refs[10 + n_cast]
    t = pl.program_id(1)
    riders = None
    if n_cast:
        riders = _Riders(refs[10:10 + n_cast], refs[11 + n_cast:11 + 2 * n_cast], refs[11 + 2 * n_cast:],
                         rider_blocks, ATTN_RIDER_BLOCK)
        riders.before(pl.program_id(0) * pl.num_programs(1) + t)
        riders.convert()
    tt = t - 1
    tr = q_ref.shape[0]
    n_ctx = kc_ref.shape[0]
    halo = kp_ref.shape[0]
    n_win = tr + 2 * halo
    qi = lax.broadcasted_iota(jnp.int32, (tr, n_win), 0)
    c = lax.broadcasted_iota(jnp.int32, (tr, n_win), 1)
    ok = (c >= qi) & (c <= qi + 2 * WINDOW)
    ok = ok & ((c >= halo) | (tt > 0)) & ((c < halo + tr) | (tt < tiles_per_seg - 1)) & (t > 0)
    bias = jnp.concatenate([jnp.zeros((tr, n_ctx), F32), jnp.where(ok, 0.0, NEG_INF)], axis=1)
    for hk in range(N_KV_HEADS):
        cs = slice(hk * HEAD_DIM, (hk + 1) * HEAD_DIM)
        k_all = jnp.concatenate([kc_ref[:, cs], kp_ref[:, cs], ko_ref[:, cs], kn_ref[:, cs]], axis=0)
        v_all = jnp.concatenate([vc_ref[:, cs], vp_ref[:, cs], vo_ref[:, cs], vn_ref[:, cs]], axis=0)
        for g in range(Q_PER_KV):
            hq = hk * Q_PER_KV + g
            qs = slice(hq * HEAD_DIM, (hq + 1) * HEAD_DIM)
            s = lax.dot_general(q_ref[:, qs], k_all, (((1,), (1,)), ((), ())),
                                preferred_element_type=F32) + bias
            sink = sink_ref[hq]
            m = jnp.maximum(jnp.max(s, axis=-1, keepdims=True), sink)
            p = jnp.exp(s - m)
            denom = jnp.sum(p, axis=-1, keepdims=True) + jnp.exp(sink - m)
            o = jnp.dot(p.astype(BF16), v_all, preferred_element_type=F32)
            o_ref[:, qs] = (o / denom).astype(o_ref.dtype)
    if riders:
        riders.after()


def _attention(z, sinks, n_batch, n_ctx_tiles, tiles_per_seg, riders=()):
    r = z.shape[0]
    grid = (n_batch, 1 + tiles_per_seg)
    rider_blocks, n_blocks, rider_scratch = _Riders.plan(riders, ATTN_RIDER_BLOCK)
    assert not riders or grid[0] * grid[1] >= n_blocks + 2
    any_spec = pl.BlockSpec(memory_space=pl.ANY)
    q_dim = N_Q_HEADS * HEAD_DIM
    kv_dim = N_KV_HEADS * HEAD_DIM
    k_col = q_dim // kv_dim
    v_col = k_col + 1
    halo = WINDOW
    per = TR // halo
    n_halo_blk = r // halo

    def qblk(b, t):
        return jnp.where(t == 0, b, n_ctx_tiles + b * tiles_per_seg + t - 1)

    def own(col):
        return pl.BlockSpec((TR, kv_dim), lambda b, t: (qblk(b, t), col))

    def ctx(col):
        return pl.BlockSpec((TR, kv_dim), lambda b, t: (b, col))

    def prev(col):
        return pl.BlockSpec((halo, kv_dim), lambda b, t: (jnp.maximum(qblk(b, t) * per - 1, 0), col))

    def nxt(col):
        return pl.BlockSpec((halo, kv_dim),
                            lambda b, t: (jnp.minimum((qblk(b, t) + 1) * per, n_halo_blk - 1), col))

    return pl.pallas_call(
        functools.partial(_attn_kernel, tiles_per_seg=tiles_per_seg, rider_blocks=rider_blocks),
        out_shape=(jax.ShapeDtypeStruct((r, q_dim), BF16),
                   *[jax.ShapeDtypeStruct(w.shape, BF16) for w in riders]),
        grid=grid,
        in_specs=[pl.BlockSpec(memory_space=pltpu.SMEM),
                  pl.BlockSpec((TR, q_dim), lambda b, t: (qblk(b, t), 0)),
                  ctx(k_col), ctx(v_col),
                  prev(k_col), own(k_col), nxt(k_col),
                  prev(v_col), own(v_col), nxt(v_col)] + [any_spec] * len(riders),
        out_specs=(pl.BlockSpec((TR, q_dim), lambda b, t: (qblk(b, t), 0)), *[any_spec] * len(riders)),
        scratch_shapes=rider_scratch if riders else [],
        compiler_params=pltpu.CompilerParams(
            dimension_semantics=("arbitrary", "arbitrary"),
            vmem_limit_bytes=_vmem_limit(12 * TR * q_dim * 2 + 8 * TR * (TR + 3 * halo) * 4
                                         + (12 * ATTN_RIDER_BLOCK[0] * ATTN_RIDER_BLOCK[1] if riders else 0))),
        name="attention",
    )(sinks, z, z, z, z, z, z, z, z, z, *riders)


def _l0_out_kernel(ctx_ref, x_ref, a1_ref, a2_ref, w_ref, gate_ref, oc_ref, ol_ref):
    i = pl.program_id(0)
    lhs = jnp.concatenate([a1_ref[...], a2_ref[...]], axis=1)
    d = ol_ref.shape[1]
    for c0 in range(0, d, TN_CHUNK):
        cs = slice(c0, c0 + TN_CHUNK)
        y = jnp.dot(lhs, w_ref[:, cs], preferred_element_type=F32)
        h = jnp.where(i == 0, ctx_ref[:, cs], x_ref[:, cs])
        res = h + gate_ref[:, cs] * y
        oc_ref[:, cs] = res
        ol_ref[:, cs] = res


def _l0_out_proj(attn, conv, w_out, ctx2, x2, mods, tile_to_mod):
    n_ctx_rows, d = ctx2.shape
    n_lat = x2.shape[0]
    r = n_ctx_rows + n_lat
    k1 = attn.shape[1]
    k2 = conv.shape[1]
    assert w_out.shape[0] == k1 + k2 and d % TN_CHUNK == 0
    return pl.pallas_call(
        _l0_out_kernel,
        out_shape=(jax.ShapeDtypeStruct((2, TM, d), F32), jax.ShapeDtypeStruct((n_lat, d), F32)),
        grid=(r // TM,),
        in_specs=[_resident((TM, d)),
                  pl.BlockSpec((TM, d), lambda i: (jnp.maximum(i - 1, 0), 0)),
                  pl.BlockSpec((TM, k1), lambda i: (i, 0)),
                  pl.BlockSpec((TM, k2), lambda i: (i, 0)),
                  _resident((k1 + k2, d)),
                  _mod_spec1(2, d, tile_to_mod)],
        out_specs=(pl.BlockSpec((None, TM, d), lambda i: (jnp.minimum(i, 1), 0, 0)),
                   pl.BlockSpec((TM, d), lambda i: (jnp.maximum(i - 1, 0), 0))),
        compiler_params=pltpu.CompilerParams(
            dimension_semantics=("arbitrary",),
            vmem_limit_bytes=_vmem_limit(3 * TM * d * 4 + 3 * TM * (k1 + k2) * 2 + (k1 + k2) * d * 2
                                         + 4 * TM * d * 4 + 4 * TM * TN_CHUNK * 4)),
        name="l0_out_proj",
    )(ctx2, x2, attn, conv, w_out, mods)


def _ffn_kernel(*refs, rider_blocks):
    n_cast = len(rider_blocks)
    h_ref, g_ref, sh_ref, sc_ref, gate_ref, wg_ref, wu_ref, wd_ref = refs[:8]
    o_ref = refs[8 + n_cast]
    u_scr = refs[9 + 2 * n_cast]
    i = pl.program_id(0)
    j = pl.program_id(1)
    nj = pl.num_programs(1)
    riders = None
    if n_cast:
        riders = _Riders(refs[8:8 + n_cast], refs[9 + n_cast:9 + 2 * n_cast], refs[10 + 2 * n_cast:],
                         rider_blocks, FFN_RIDER_BLOCK)
        riders.before(i * nj + j)

    @pl.when(j == 0)
    def _():
        u_scr[...] = _norm_mod(h_ref[...], g_ref[...], sh_ref[...], sc_ref[...]).astype(BF16)

    if riders:
        riders.convert()
    u = u_scr[...]
    gt = jnp.dot(u, wg_ref[...], preferred_element_type=F32)
    up = jnp.dot(u, wu_ref[...], preferred_element_type=F32)
    act = (gt * jax.nn.sigmoid(gt) * up).astype(BF16)
    y = jnp.dot(act, wd_ref[...], preferred_element_type=F32)

    @pl.when(j == 0)
    def _():
        o_ref[...] = y

    @pl.when(j != 0)
    def _():
        o_ref[...] += y

    @pl.when(j == nj - 1)
    def _():
        o_ref[...] = h_ref[...] + gate_ref[...] * o_ref[...]

    if riders:
        riders.after()


def _ffn(h, h_spec, rows, tm, g, mods, w_gate, w_up, w_down, tile_to_mod, riders=()):
    d = h.shape[-1]
    f = w_gate.shape[1]
    tf = TF_FFN
    grid = (rows // tm, f // tf)
    rider_blocks, n_blocks, rider_scratch = _Riders.plan(riders, FFN_RIDER_BLOCK)
    assert not riders or grid[0] * grid[1] >= n_blocks + 2
    any_spec = pl.BlockSpec(memory_space=pl.ANY)
    scratch = [pltpu.VMEM((tm, d), BF16)] + (rider_scratch if riders else [])
    return pl.pallas_call(
        functools.partial(_ffn_kernel, rider_blocks=rider_blocks),
        out_shape=(jax.ShapeDtypeStruct((rows, d), F32),
                   *[jax.ShapeDtypeStruct(w.shape, BF16) for w in riders]),
        grid=grid,
        in_specs=[h_spec,
                  pl.BlockSpec((1, d), lambda i, j: (0, 0)),
                  _mod_spec(3, d, tile_to_mod),
                  _mod_spec(4, d, tile_to_mod),
                  _mod_spec(5, d, tile_to_mod),
                  pl.BlockSpec((d, tf), lambda i, j: (0, j)),
                  pl.BlockSpec((d, tf), lambda i, j: (0, j)),
                  pl.BlockSpec((tf, d), lambda i, j: (j, 0))] + [any_spec] * len(riders),
        out_specs=(pl.BlockSpec((tm, d), lambda i, j: (i, 0)), *[any_spec] * len(riders)),
        scratch_shapes=scratch,
        compiler_params=pltpu.CompilerParams(
            dimension_semantics=("arbitrary", "arbitrary"),
            vmem_limit_bytes=_vmem_limit(4 * tm * d * 4 + 6 * d * tf * 2 + tm * d * 2
                                         + 3 * tm * tf * 4 + tm * tf * 2
                                         + (12 * FFN_RIDER_BLOCK[0] * FFN_RIDER_BLOCK[1] if riders else 0))),
        name="ffn",
    )(h, g.reshape(1, d), mods, mods, mods, w_gate, w_up, w_down, *riders)


def _l1_in_kernel(hc_ref, hl_ref, g_ref, sh_ref, sc_ref, w_ref, gg_ref, xr_ref):
    h = jnp.where(pl.program_id(0) == 0, hc_ref[...], hl_ref[...])
    u = _norm_mod(h, g_ref[...], sh_ref[...], sc_ref[...]).astype(BF16)
    d_rnn = gg_ref.shape[1]
    for c0 in range(0, 2 * d_rnn, TN_CHUNK):
        z = jnp.dot(u, w_ref[:, c0:c0 + TN_CHUNK], preferred_element_type=F32)
        if c0 < d_rnn:
            gg_ref[:, c0:c0 + TN_CHUNK] = jax.nn.gelu(z, approximate=True).astype(gg_ref.dtype)
        else:
            xr_ref[:, c0 - d_rnn:c0 - d_rnn + TN_CHUNK] = z


def _l1_in_proj(h_ctx, h_lat, g, mods, w_in, d_rnn, tile_to_mod):
    n_ctx_rows, d = h_ctx.shape
    assert n_ctx_rows == TM
    r = n_ctx_rows + h_lat.shape[0]
    assert w_in.shape[1] == 2 * d_rnn and d_rnn % TN_CHUNK == 0
    return pl.pallas_call(
        _l1_in_kernel,
        out_shape=(jax.ShapeDtypeStruct((r, d_rnn), BF16), jax.ShapeDtypeStruct((r, d_rnn), F32)),
        grid=(r // TM,),
        in_specs=[_resident((TM, d)),
                  pl.BlockSpec((TM, d), lambda i: (jnp.maximum(i - 1, 0), 0)),
                  pl.BlockSpec((1, d), lambda i: (0, 0)),
                  _mod_spec1(0, d, tile_to_mod),
                  _mod_spec1(1, d, tile_to_mod),
                  _resident((d, 2 * d_rnn))],
        out_specs=(pl.BlockSpec((TM, d_rnn), lambda i: (i, 0)),
                   pl.BlockSpec((TM, d_rnn), lambda i: (i, 0))),
        compiler_params=pltpu.CompilerParams(
            dimension_semantics=("arbitrary",),
            vmem_limit_bytes=_vmem_limit(2 * TM * d * 4 + d * 2 * d_rnn * 2 + 2 * TM * d_rnn * 6
                                         + 2 * TM * d * 4 + 4 * TM * TN_CHUNK * 4)),
        name="l1_in_proj",
    )(h_ctx, h_lat, g.reshape(1, d), mods, mods, w_in)


def _rglru_kernel(xf_ref, xfp_ref, xfn_ref, xb_ref, xbp_ref, xbn_ref, cw_ref, cb_ref,
                  gaw_ref, gxw_ref, gab_ref, gxb_ref, lam_ref, hf_ref, hb_ref,
                  xc_scr, a_scr, b_scr, h_scr, *, tiles_per_seg):
    c = pl.program_id(1)
    t_len, d_rnn = xf_ref.shape
    hd = d_rnn // N_RNN_HEADS

    @pl.when(c == 0)
    def _():
        h_scr[...] = jnp.zeros_like(h_scr)

    is_ctx = c == 0
    chunk = (c - 1, tiles_per_seg - c)
    halos = ((xfp_ref, xfn_ref), (xbp_ref, xbn_ref))

    for z, x_ref in ((0, xf_ref), (1, xb_ref)):
        first = is_ctx | (chunk[z] == 0)
        last = is_ctx | (chunk[z] == tiles_per_seg - 1)
        xc_scr[...] = _conv_taps(x_ref[...], halos[z][0][...], halos[z][1][...], first, last,
                                 cw_ref[...], cb_ref[...], RG_CONV_OFFSETS)
        x_ref = xc_scr
        neg_lam = -lam_ref[z]
        sp = jnp.maximum(neg_lam, 0.0) + jnp.log1p(jnp.exp(-jnp.abs(neg_lam)))
        half_k = sp * (-0.5 * RG_C * LOG2_E)
        for hh in range(N_RNN_HEADS):
            sl = slice(hh * hd, (hh + 1) * hd)
            xh = x_ref[:, sl]
            xh16 = xh.astype(BF16)
            ta = jnp.tanh(jnp.dot(xh16, gaw_ref[z, hh], preferred_element_type=F32) + gab_ref[z][:, sl])
            ti = jnp.tanh(jnp.dot(xh16, gxw_ref[z, hh], preferred_element_type=F32) + gxb_ref[z][:, sl])
            hk = half_k[:, sl]
            a = jnp.exp2(ta * hk + hk)
            om = 1.0 - a * a
            root = jnp.where(om > 0.0, om * lax.rsqrt(om), 0.0)
            rx = (0.5 * root) * xh
            a_scr[z, :, sl] = a
            b_scr[z, :, sl] = rx * ti + rx

    row8 = lax.broadcasted_iota(jnp.int32, (8, d_rnn), 0)
    n_grp = t_len // 8

    def fwd_group(r0, h):
        a = a_scr[0, pl.ds(r0, 8), :]
        b = b_scr[0, pl.ds(r0, 8), :]
        for s in (1, 2, 4):
            a_sh = jnp.where(row8 >= s, pltpu.roll(a, s, 0), 1.0)
            b_sh = jnp.where(row8 >= s, pltpu.roll(b, s, 0), 0.0)
            b = a * b_sh + b
            a = a * a_sh
        out = a * h + b
        return out, jnp.broadcast_to(out[7:8, :], (8, d_rnn))

    def bwd_group(r0, h):
        a = a_scr[1, pl.ds(r0, 8), :]
        b = b_scr[1, pl.ds(r0, 8), :]
        for s in (1, 2, 4):
            a_sh = jnp.where(row8 < 8 - s, pltpu.roll(a, 8 - s, 0), 1.0)
            b_sh = jnp.where(row8 < 8 - s, pltpu.roll(b, 8 - s, 0), 0.0)
            b = a * b_sh + b
            a = a * a_sh
        out = a * h + b
        return out, jnp.broadcast_to(out[0:1, :], (8, d_rnn))

    def fwd_body(g, h):
        r0 = pl.multiple_of(g * 16, 16)
        lo, h = fwd_group(r0, h)
        hi, h = fwd_group(r0 + 8, h)
        hf_ref[pl.ds(r0, 16), :] = jnp.concatenate([lo, hi], axis=0).astype(hf_ref.dtype)
        return h

    def bwd_body(k, h):
        r0 = pl.multiple_of((n_grp // 2 - 1 - k) * 16, 16)
        hi, h = bwd_group(r0 + 8, h)
        lo, h = bwd_group(r0, h)
        hb_ref[pl.ds(r0, 16), :] = jnp.concatenate([lo, hi], axis=0).astype(hb_ref.dtype)
        return h

    h_scr[0] = lax.fori_loop(0, n_grp // 2, fwd_body, h_scr[0])
    h_scr[1] = lax.fori_loop(0, n_grp // 2, bwd_body, h_scr[1])


def _rglru(xr, conv_w, conv_b, ga_w, gx_w, ga_b, gx_b, lam, n_batch, n_ctx_tiles, tiles_per_seg):
    r, d_rnn = xr.shape
    n_lat = n_batch * tiles_per_seg * TR
    hd = d_rnn // N_RNN_HEADS
    nt = tiles_per_seg
    nh = TR // HALO
    n_halo = r // HALO
    n_taps = conv_w.shape[0]

    def fblk(b, c):
        return jnp.where(c == 0, b, n_ctx_tiles + b * nt + c - 1)

    def bblk(b, c):
        return jnp.where(c == 0, b, n_ctx_tiles + b * nt + nt - c)

    def chunk_specs(blk):
        return [pl.BlockSpec((TR, d_rnn), lambda b, c: (blk(b, c), 0)),
                pl.BlockSpec((HALO, d_rnn), lambda b, c: (jnp.maximum(blk(b, c) * nh - 1, 0), 0)),
                pl.BlockSpec((HALO, d_rnn), lambda b, c: (jnp.minimum((blk(b, c) + 1) * nh, n_halo - 1), 0))]

    def full(shape):
        return pl.BlockSpec(shape, lambda b, c: (0,) * len(shape))

    return pl.pallas_call(
        functools.partial(_rglru_kernel, tiles_per_seg=nt),
        out_shape=(jax.ShapeDtypeStruct((n_lat, d_rnn), BF16), jax.ShapeDtypeStruct((n_lat, d_rnn), BF16)),
        grid=(n_batch, 1 + nt),
        in_specs=chunk_specs(fblk) + chunk_specs(bblk) + [
            full((n_taps, d_rnn)), full((1, d_rnn)),
            full((2, N_RNN_HEADS, hd, hd)), full((2, N_RNN_HEADS, hd, hd)),
            full((2, 1, d_rnn)), full((2, 1, d_rnn)), full((2, 1, d_rnn))],
        out_specs=(pl.BlockSpec((TR, d_rnn), lambda b, c: (b * nt + jnp.maximum(c - 1, 0), 0)),
                   pl.BlockSpec((TR, d_rnn), lambda b, c: (b * nt + jnp.where(c == 0, nt - 1, nt - c), 0))),
        scratch_shapes=[pltpu.VMEM((TR, d_rnn), F32),
                        pltpu.VMEM((2, TR, d_rnn), F32), pltpu.VMEM((2, TR, d_rnn), F32),
                        pltpu.VMEM((2, 8, d_rnn), F32)],
        compiler_params=pltpu.CompilerParams(
            dimension_semantics=("arbitrary", "arbitrary"),
            vmem_limit_bytes=_vmem_limit(14 * TR * d_rnn * 4 + 8 * N_RNN_HEADS * hd * hd * 2)),
        name="rglru",
    )(xr, xr, xr, xr, xr, xr, conv_w, conv_b.reshape(1, d_rnn), ga_w, gx_w,
      ga_b.reshape(2, 1, d_rnn), gx_b.reshape(2, 1, d_rnn), lam.reshape(2, 1, d_rnn))


def _l1_out_kernel(hf_ref, hb_ref, gg_ref, w_ref, h_ref, gate_ref, o_ref):
    rec = hf_ref[...].astype(F32) + hb_ref[...].astype(F32)
    lhs = (rec * gg_ref[...].astype(F32)).astype(BF16)
    d = o_ref.shape[1]
    for c0 in range(0, d, TN_CHUNK):
        cs = slice(c0, c0 + TN_CHUNK)
        y = jnp.dot(lhs, w_ref[:, cs], preferred_element_type=F32)
        o_ref[:, cs] = h_ref[:, cs] + gate_ref[:, cs] * y


def _l1_out_proj(hf, hb, gg, w_out, h, mods, n_ctx_mtiles, lat_tile_to_mod):
    n_lat, d_rnn = hf.shape
    d = h.shape[1]
    assert d % TN_CHUNK == 0
    return pl.pallas_call(
        _l1_out_kernel,
        out_shape=jax.ShapeDtypeStruct((n_lat, d), F32),
        grid=(n_lat // TM,),
        in_specs=[pl.BlockSpec((TM, d_rnn), lambda i: (i, 0)),
                  pl.BlockSpec((TM, d_rnn), lambda i: (i, 0)),
                  pl.BlockSpec((TM, d_rnn), lambda i: (i + n_ctx_mtiles, 0)),
                  _resident((d_rnn, d)),
                  pl.BlockSpec((TM, d), lambda i: (i, 0)),
                  _mod_spec1(2, d, lat_tile_to_mod)],
        out_specs=pl.BlockSpec((TM, d), lambda i: (i, 0)),
        compiler_params=pltpu.CompilerParams(
            dimension_semantics=("arbitrary",),
            vmem_limit_bytes=_vmem_limit(6 * TM * d_rnn * 2 + d_rnn * d * 2 + 4 * TM * d * 4
                                         + 2 * TM * d_rnn * 4 + 4 * TM * TN_CHUNK * 4)),
        name="l1_out_proj",
    )(hf, hb, gg, w_out, h, mods)


def _router_kernel(h_ref, g_ref, sh_ref, sc_ref, rw_ref, rb_ref, u_ref, route_ref):
    u = _norm_mod(h_ref[...], g_ref[...], sh_ref[...], sc_ref[...])
    u_ref[...] = u
    logits = jnp.dot(u.astype(BF16), rw_ref[...], preferred_element_type=F32) + rb_ref[...]
    lane = lax.broadcasted_iota(jnp.int32, logits.shape, 1)
    m1 = jnp.max(logits, axis=-1, keepdims=True)
    i1 = jnp.min(jnp.where(logits == m1, lane, LANES), axis=-1, keepdims=True)
    rest = jnp.where(lane == i1, -jnp.inf, logits)
    m2 = jnp.max(rest, axis=-1, keepdims=True)
    i2 = jnp.min(jnp.where(rest == m2, lane, LANES), axis=-1, keepdims=True)
    e2 = jnp.exp(m2 - m1)
    w1 = 1.0 / (1.0 + e2)
    w2 = e2 * w1
    route = jnp.where(lane == 0, i1.astype(F32),
                      jnp.where(lane == 1, i2.astype(F32),
                                jnp.where(lane == 2, w1, jnp.where(lane == 3, w2, 0.0))))
    route_ref[...] = route


def _router(h, g, mods, rw_pad, rb_pad, lat_tile_to_mod):
    n, d = h.shape

    def mspec(chunk):
        return pl.BlockSpec((None, 1, d), lambda i: (lat_tile_to_mod(i) * N_MOD + chunk, 0, 0))

    return pl.pallas_call(
        _router_kernel,
        out_shape=(jax.ShapeDtypeStruct((n, d), F32), jax.ShapeDtypeStruct((n, LANES), F32)),
        grid=(n // TM,),
        in_specs=[pl.BlockSpec((TM, d), lambda i: (i, 0)),
                  pl.BlockSpec((1, d), lambda i: (0, 0)),
                  mspec(3), mspec(4),
                  pl.BlockSpec((d, LANES), lambda i: (0, 0)),
                  pl.BlockSpec((1, LANES), lambda i: (0, 0))],
        out_specs=(pl.BlockSpec((TM, d), lambda i: (i, 0)),
                   pl.BlockSpec((TM, LANES), lambda i: (i, 0))),
        compiler_params=pltpu.CompilerParams(
            dimension_semantics=("arbitrary",),
            vmem_limit_bytes=_vmem_limit(6 * TM * d * 4)),
        name="router",
    )(h, g.reshape(1, d), mods, mods, rw_pad, rb_pad)


def _row_copy(src, src_row, dst, dst_row, sem):
    return pltpu.make_async_copy(src.at[pl.ds(src_row, 1)], dst.at[pl.ds(dst_row, 1)], sem)


def _moe_kernel(te_ref, tv_ref, tok_ref, tok_next_ref, u_hbm, wg_ref, wu_ref, wd_ref, o_ref,
                xg_scr, x_scr, sems, *, gather_steps):
    del te_ref
    i = pl.program_id(0)
    j = pl.program_id(1)
    n_tiles = pl.num_programs(0)
    rows = x_scr.shape[0]
    per_step = rows // gather_steps
    slot = i % 2
    valid = tv_ref[i] == 1
    next_valid = (i + 1 < n_tiles) & (tv_ref[jnp.minimum(i + 1, n_tiles - 1)] == 1)

    @pl.when(valid & (j == 0))
    def _():
        @pl.when(i == 0)
        def _():
            def start(r, carry):
                _row_copy(u_hbm, tok_ref[r], xg_scr.at[0], r, sems.at[0]).start()
                return carry
            lax.fori_loop(0, rows, start, 0)

        pltpu.make_async_copy(u_hbm.at[pl.ds(0, rows)], xg_scr.at[slot], sems.at[slot]).wait()
        x_scr[...] = xg_scr[slot].astype(BF16)

    def step(with_gather):
        if with_gather:
            r0 = j * per_step
            for r in range(per_step):
                _row_copy(u_hbm, tok_next_ref[r0 + r], xg_scr.at[1 - slot], r0 + r, sems.at[1 - slot]).start()
        x = x_scr[...]
        gt = jnp.dot(x, wg_ref[...], preferred_element_type=F32)
        up = jnp.dot(x, wu_ref[...], preferred_element_type=F32)
        act = (gt * jax.nn.sigmoid(gt) * up).astype(BF16)
        y = jnp.dot(act, wd_ref[...], preferred_element_type=F32)

        @pl.when(j == 0)
        def _():
            o_ref[...] = y

        @pl.when(j != 0)
        def _():
            o_ref[...] += y

    gather_now = next_valid & (j < gather_steps)

    @pl.when(valid & gather_now)
    def _():
        step(True)

    @pl.when(valid & jnp.logical_not(gather_now))
    def _():
        step(False)

    @pl.when((tv_ref[i] == 0) & (j == 0))
    def _():
        o_ref[...] = jnp.zeros_like(o_ref)


def _moe(u, slot_tok, tile_e, tile_valid, w_gate, w_up, w_down):
    d = u.shape[1]
    cap = slot_tok.shape[0]
    n_tiles = cap // TM
    f = w_gate.shape[2]
    tf = TF_MOE
    nj = f // tf
    gather_steps = min(MOE_GATHER_STEPS, nj)
    assert TM % gather_steps == 0

    def jj(i, j, tv):
        return jnp.where(tv[i] == 1, j, nj - 1)

    return pl.pallas_call(
        functools.partial(_moe_kernel, gather_steps=gather_steps),
        out_shape=jax.ShapeDtypeStruct((cap, d), F32),
        grid_spec=pltpu.PrefetchScalarGridSpec(
            num_scalar_prefetch=2,
            grid=(n_tiles, nj),
            in_specs=[pl.BlockSpec((TM,), lambda i, j, te, tv: (i,), memory_space=pltpu.SMEM),
                      pl.BlockSpec((TM,), lambda i, j, te, tv: (jnp.minimum(i + 1, n_tiles - 1),),
                                   memory_space=pltpu.SMEM),
                      pl.BlockSpec(memory_space=pl.ANY),
                      pl.BlockSpec((None, d, tf), lambda i, j, te, tv: (te[i], 0, jj(i, j, tv))),
                      pl.BlockSpec((None, d, tf), lambda i, j, te, tv: (te[i], 0, jj(i, j, tv))),
                      pl.BlockSpec((None, tf, d), lambda i, j, te, tv: (te[i], jj(i, j, tv), 0))],
            out_specs=pl.BlockSpec((TM, d), lambda i, j, te, tv: (i, 0)),
            scratch_shapes=[pltpu.VMEM((2, TM, d), F32), pltpu.VMEM((TM, d), BF16),
                            pltpu.SemaphoreType.DMA((2,))]),
        compiler_params=pltpu.CompilerParams(
            dimension_semantics=("arbitrary", "arbitrary"),
            vmem_limit_bytes=_vmem_limit(2 * TM * d * 4 + 2 * TM * d * 4 + 6 * d * tf * 2 + TM * d * 2
                                         + 3 * TM * tf * 4 + TM * d * 4)),
        name="moe_experts",
    )(tile_e, tile_valid, slot_tok, slot_tok, u, w_gate, w_up, w_down)


def _combine_kernel(dest_ref, route_ref, h_ref, gate_ref, fg_ref, yb_hbm, o_ref, y_scr, sem, *, rows):
    def start(r, carry):
        for kk in range(TOP_K):
            _row_copy(yb_hbm, dest_ref[TOP_K * r + kk], y_scr.at[kk], r, sem).start()
        return carry

    lax.fori_loop(0, rows, start, 0)
    for kk in range(TOP_K):
        pltpu.make_async_copy(yb_hbm.at[pl.ds(0, rows)], y_scr.at[kk], sem).wait()

    route = route_ref[...]
    moe = route[:, 2:3] * y_scr[0] + route[:, 3:4] * y_scr[1]
    hl = h_ref[...] + gate_ref[...] * moe
    ms = jnp.mean(hl * hl, axis=-1, keepdims=True)
    o_ref[...] = (hl * lax.rsqrt(ms + NORM_EPS)) * fg_ref[...]


def _combine(yb, dest, route, h, mods, final_g, lat_tile_to_mod):
    n, d = h.shape
    return pl.pallas_call(
        functools.partial(_combine_kernel, rows=TM),
        out_shape=jax.ShapeDtypeStruct((n, d), F32),
        grid=(n // TM,),
        in_specs=[pl.BlockSpec((TOP_K * TM,), lambda i: (i,), memory_space=pltpu.SMEM),
                  pl.BlockSpec((TM, LANES), lambda i: (i, 0)),
                  pl.BlockSpec((TM, d), lambda i: (i, 0)),
                  pl.BlockSpec((None, 1, d), lambda i: (lat_tile_to_mod(i) * N_MOD + 5, 0, 0)),
                  pl.BlockSpec((1, d), lambda i: (0, 0)),
                  pl.BlockSpec(memory_space=pl.ANY)],
        out_specs=pl.BlockSpec((TM, d), lambda i: (i, 0)),
        scratch_shapes=[pltpu.VMEM((TOP_K, TM, d), F32), pltpu.SemaphoreType.DMA],
        compiler_params=pltpu.CompilerParams(
            dimension_semantics=("arbitrary",),
            vmem_limit_bytes=_vmem_limit(TOP_K * TM * d * 4 + 4 * TM * d * 4 + 2 * TM * d * 4)),
        name="moe_combine",
    )(dest, route, h, mods, final_g.reshape(1, d), yb)


def _rope_tables(n_ident, s_len):
    t = jnp.arange(s_len)
    row_id = (t // GRID_W).astype(F32)
    col_id = (t % GRID_W).astype(F32)
    inv_freq = ROPE_BASE ** (-jnp.arange(ROPE_PAIRS, dtype=F32) / ROPE_PAIRS)
    ang_r = row_id[:, None] * inv_freq
    ang_c = col_id[:, None] * inv_freq
    ang = jnp.concatenate([ang_r, ang_r, ang_c, ang_c], axis=-1)
    cos, sin = jnp.cos(ang), jnp.sin(ang)
    first_half = (jnp.arange(HEAD_DIM) % (2 * ROPE_PAIRS)) < ROPE_PAIRS
    sa = jnp.where(first_half, -sin, 0.0)
    sb = jnp.where(first_half, 0.0, sin)
    ones = jnp.ones((n_ident, HEAD_DIM), F32)
    zeros = jnp.zeros((n_ident, HEAD_DIM), F32)
    return (jnp.concatenate([ones, cos]), jnp.concatenate([zeros, sa]), jnp.concatenate([zeros, sb]))


def _routing_plan(route, n_tiles):
    e = route[:, :TOP_K].astype(jnp.int32).reshape(-1)
    onehot = (e[:, None] == jnp.arange(N_EXPERTS, dtype=jnp.int32)[None, :]).astype(jnp.int32)
    csum = jnp.cumsum(onehot, axis=0)
    rank = jnp.sum(csum * onehot, axis=1) - 1
    counts = csum[-1]
    padded = (counts + TM - 1) // TM * TM
    pad_end = jnp.cumsum(padded)
    pad_start = pad_end - padded
    dest = jnp.sum(onehot * pad_start[None, :], axis=1) + rank
    n_valid = pad_end[-1] // TM
    tile_idx = jnp.arange(n_tiles, dtype=jnp.int32)
    tile_valid = (tile_idx < n_valid).astype(jnp.int32)
    tile_row = jnp.minimum(tile_idx, n_valid - 1)
    tile_e = jnp.sum((pad_end[None, :] <= (tile_row * TM)[:, None]).astype(jnp.int32), axis=1)
    tile_e = jnp.minimum(tile_e, N_EXPERTS - 1)
    dest = dest.astype(jnp.int32)
    slot_tok = jnp.zeros((n_tiles * TM,), jnp.int32).at[dest].set(
        jnp.arange(e.shape[0], dtype=jnp.int32) // TOP_K)
    return dest, slot_tok, tile_e.astype(jnp.int32), tile_valid


def _mods(cvecs, w_mod, b_mod):
    d = w_mod.shape[0]
    m = _adaln(cvecs, w_mod, b_mod)[:3]
    return m.reshape(3 * N_MOD, 1, d)


def kernel(x, c, ctx, c_ctx, l0_w_mod, l0_b_mod, l0_norm1_g, l0_w_in, l0_sinks, l0_conv_w, l0_conv_b, l0_w_out, l0_norm2_g, l0_ffn_w_gate, l0_ffn_w_up, l0_ffn_w_down, l1_w_mod, l1_b_mod, l1_norm1_g, l1_w_in, l1_conv_w, l1_conv_b, l1_gate_a_w, l1_gate_a_b, l1_gate_x_w, l1_gate_x_b, l1_lambda, l1_w_out, l1_norm2_g, l1_router_w, l1_router_b, l1_moe_w_gate, l1_moe_w_up, l1_moe_w_down, final_norm_g):
    n_batch, s_len, d = x.shape
    n_ctx = ctx.shape[1]
    assert n_batch == 2 and n_batch * n_ctx == TM and n_ctx == TR
    assert s_len % TM == 0 and s_len % GRID_W == 0 and WINDOW * 2 == TR
    n_ctx_rows = n_batch * n_ctx
    n_lat = n_batch * s_len
    n_ctx_rtiles = n_ctx_rows // TR
    tiles_per_seg = s_len // TR
    mtiles_per_seg = s_len // TM
    d_rnn = l1_w_out.shape[0]
    conv_dim = l0_conv_w.shape[1]
    q_dim = N_Q_HEADS * HEAD_DIM
    kv_dim = N_KV_HEADS * HEAD_DIM

    def tile_to_mod(i):
        return jnp.where(i == 0, n_batch, (i - 1) // mtiles_per_seg)

    def lat_tile_to_mod(i):
        return i // mtiles_per_seg

    def rope_blk(i):
        return jnp.where(i == 0, 0, 1 + (i - 1) % mtiles_per_seg)

    cvecs = jnp.concatenate([c, c_ctx[None, :], jnp.zeros((8 - n_batch - 1, d), F32)], axis=0)
    mods0 = _mods(cvecs, l0_w_mod, l0_b_mod)
    mods1 = _mods(cvecs, l1_w_mod, l1_b_mod)
    ctx2 = ctx.reshape(n_ctx_rows, d)
    x2 = x.reshape(n_lat, d)

    rope = _rope_tables(TM, s_len)
    z = _l0_in_proj(ctx2, x2, l0_norm1_g, mods0, l0_w_in.astype(BF16), rope, tile_to_mod, rope_blk)
    x_col = (q_dim + 2 * kv_dim) // CONV_CW
    per = conv_dim // CONV_CW
    conv = _gated_conv(z, l0_conv_w, l0_conv_b, n_ctx_rtiles, tiles_per_seg,
                       x_col, x_col + per, x_col + 2 * per)
    attn, moe_wd = _attention(z, l0_sinks, n_batch, n_ctx_rtiles, tiles_per_seg, riders=(l1_moe_w_down,))
    hc2, hl = _l0_out_proj(attn, conv, l0_w_out.astype(BF16), ctx2, x2, mods0, tile_to_mod)
    ffn_w = (l0_ffn_w_gate.astype(BF16), l0_ffn_w_up.astype(BF16), l0_ffn_w_down.astype(BF16))
    (hc,) = _ffn(hc2, pl.BlockSpec((None, TM, d), lambda i, j: (0, 0, 0)), n_ctx_rows, TM,
                 l0_norm2_g, mods0, *ffn_w, lambda i: n_batch)
    hl, moe_wg, moe_wu = _ffn(
        hl, pl.BlockSpec((TM, d), lambda i, j: (i, 0)), n_lat, TM, l0_norm2_g, mods0, *ffn_w,
        lat_tile_to_mod, riders=(l1_moe_w_gate, l1_moe_w_up))

    gg, xr = _l1_in_proj(hc, hl, l1_norm1_g, mods1, l1_w_in.astype(BF16), d_rnn, tile_to_mod)
    hf, hb = _rglru(xr, l1_conv_w, l1_conv_b, (0.5 * l1_gate_a_w).astype(BF16), (0.5 * l1_gate_x_w).astype(BF16),
                    0.5 * l1_gate_a_b, 0.5 * l1_gate_x_b, l1_lambda, n_batch, n_ctx_rtiles, tiles_per_seg)
    h_lat = _l1_out_proj(hf, hb, gg, l1_w_out.astype(BF16), hl, mods1, n_ctx_rows // TM, lat_tile_to_mod)

    rw_pad = jnp.zeros((d, LANES), BF16).at[:, :N_EXPERTS].set(l1_router_w.astype(BF16))
    rb_pad = jnp.full((1, LANES), NEG_INF, F32).at[0, :N_EXPERTS].set(l1_router_b)
    u, route = _router(h_lat, l1_norm2_g, mods1, rw_pad, rb_pad, lat_tile_to_mod)
    n_tiles = (n_lat * TOP_K) // TM + N_EXPERTS
    dest, slot_tok, tile_e, tile_valid = _routing_plan(route, n_tiles)
    yb = _moe(u, slot_tok, tile_e, tile_valid, moe_wg, moe_wu, moe_wd)
    out = _combine(yb, dest, route, h_lat, mods1, final_norm_g, lat_tile_to_mod)
    return out.reshape(n_batch, s_len, d)
```

```python
import functools

import jax
import jax.numpy as jnp
from jax import lax
from jax.experimental import pallas as pl
from jax.experimental.pallas import tpu as pltpu

F32 = jnp.float32
BF16 = jnp.bfloat16

GRID_W = 64
HEAD_DIM = 128
N_Q_HEADS = 8
N_KV_HEADS = 2
Q_PER_KV = N_Q_HEADS // N_KV_HEADS
WINDOW = 128
ATTN_SCALE = HEAD_DIM ** -0.5
ROPE_BASE = 10000.0
ROPE_PAIRS = HEAD_DIM // 4
SHORT_CONV_OFFSETS = (-1, 0, 1)
RG_CONV_OFFSETS = (-2, -1, 0, 1)
N_RNN_HEADS = 16
RG_C = 8.0
N_EXPERTS = 8
TOP_K = 2
LOG2_E = 1.4426950408889634
NORM_EPS = 1e-6
NEG_INF = -1e30
N_MOD = 6

LANES = 128
TM = 512
CAST_BLOCK = 1024
TR = 256
HALO = 16
TN_CHUNK = 512
TF_FFN = 512
TF_MOE = 1024
MOE_GATHER_STEPS = 4
CONV_CW = 512
MOD_TN = 1024
VMEM_CAP = 56 * 1024 * 1024


def _vmem_limit(nbytes):
    return int(min(max(nbytes * 5 // 4 + (4 << 20), 32 << 20), VMEM_CAP))


def _norm_mod(h, g, shift, scale):
    ms = jnp.mean(h * h, axis=-1, keepdims=True)
    y = h * lax.rsqrt(ms + NORM_EPS)
    return (y * g) * (1.0 + scale) + shift


def _adaln_kernel(c_ref, w_ref, b_ref, o_ref):
    c = c_ref[...]
    s = (c * jax.nn.sigmoid(c)).astype(BF16)
    o_ref[...] = jnp.dot(s, w_ref[...].astype(BF16), preferred_element_type=F32) + b_ref[...]


def _adaln(cvecs, w_mod, b_mod):
    d, n = w_mod.shape
    return pl.pallas_call(
        _adaln_kernel,
        out_shape=jax.ShapeDtypeStruct((8, n), F32),
        grid=(n // MOD_TN,),
        in_specs=[pl.BlockSpec((8, d), lambda j: (0, 0)),
                  pl.BlockSpec((d, MOD_TN), lambda j: (0, j)),
                  pl.BlockSpec((1, MOD_TN), lambda j: (0, j))],
        out_specs=pl.BlockSpec((8, MOD_TN), lambda j: (0, j)),
        compiler_params=pltpu.CompilerParams(
            dimension_semantics=("arbitrary",),
            vmem_limit_bytes=_vmem_limit(2 * d * MOD_TN * 4 + d * MOD_TN * 2)),
        name="adaln",
    )(cvecs, w_mod, b_mod.reshape(1, n))


def _mod_spec(chunk, width, tile_to_mod):
    return pl.BlockSpec((None, 1, width), lambda i, j: (tile_to_mod(i) * N_MOD + chunk, 0, 0))


def _mod_spec1(chunk, width, tile_to_mod):
    return pl.BlockSpec((None, 1, width), lambda i: (tile_to_mod(i) * N_MOD + chunk, 0, 0))


def _resident(shape):
    return pl.BlockSpec(shape, lambda i: (0,) * len(shape), pipeline_mode=pl.Buffered(1))


def _l0_in_kernel(ctx_ref, x_ref, g_ref, sh_ref, sc_ref, w_ref, cos_ref, sa_ref, sb_ref, o_ref):
    i = pl.program_id(0)
    h = jnp.where(i == 0, ctx_ref[...], x_ref[...])
    u = _norm_mod(h, g_ref[...], sh_ref[...], sc_ref[...]).astype(BF16)
    cos, sa, sb = cos_ref[...], sa_ref[...], sb_ref[...]
    n_rot = N_Q_HEADS + N_KV_HEADS
    n = o_ref.shape[1]
    for c0 in range(0, n, TN_CHUNK):
        z = jnp.dot(u, w_ref[:, c0:c0 + TN_CHUNK], preferred_element_type=F32)
        if c0 >= n_rot * HEAD_DIM:
            o_ref[:, c0:c0 + TN_CHUNK] = z.astype(BF16)
            continue
        for k in range(TN_CHUNK // HEAD_DIM):
            hh = c0 // HEAD_DIM + k
            t = z[:, k * HEAD_DIM:(k + 1) * HEAD_DIM]
            if hh < n_rot:
                t = (t * cos + pltpu.roll(t, HEAD_DIM - ROPE_PAIRS, 1) * sa
                     + pltpu.roll(t, ROPE_PAIRS, 1) * sb)
            if hh < N_Q_HEADS:
                t = t * ATTN_SCALE
            o_ref[:, hh * HEAD_DIM:(hh + 1) * HEAD_DIM] = t.astype(BF16)


def _l0_in_proj(ctx2, x2, g, mods, w_in, rope, tile_to_mod, rope_blk):
    n_ctx_rows, d = ctx2.shape
    assert n_ctx_rows == TM
    r = n_ctx_rows + x2.shape[0]
    n = w_in.shape[1]
    assert n % TN_CHUNK == 0
    cos, sa, sb = rope
    rope_spec = pl.BlockSpec((TM, HEAD_DIM), lambda i: (rope_blk(i), 0))
    return pl.pallas_call(
        _l0_in_kernel,
        out_shape=jax.ShapeDtypeStruct((r, n), BF16),
        grid=(r // TM,),
        in_specs=[_resident((TM, d)),
                  pl.BlockSpec((TM, d), lambda i: (jnp.maximum(i - 1, 0), 0)),
                  pl.BlockSpec((1, d), lambda i: (0, 0)),
                  _mod_spec1(0, d, tile_to_mod),
                  _mod_spec1(1, d, tile_to_mod),
                  _resident((d, n)),
                  rope_spec, rope_spec, rope_spec],
        out_specs=pl.BlockSpec((TM, n), lambda i: (i, 0)),
        compiler_params=pltpu.CompilerParams(
            dimension_semantics=("arbitrary",),
            vmem_limit_bytes=_vmem_limit(3 * TM * d * 4 + d * n * 2 + 2 * TM * n * 2
                                         + 2 * TM * d * 4 + 4 * TM * TN_CHUNK * 4)),
        name="l0_in_proj",
    )(ctx2, x2, g.reshape(1, d), mods, mods, w_in, cos, sa, sb)


def _conv_taps(x, xp, xn, first, last, w, b, offsets):
    xp = jnp.where(first, 0.0, xp)
    xn = jnp.where(last, 0.0, xn)
    tr, cw = x.shape
    row8 = lax.broadcasted_iota(jnp.int32, (8, cw), 0)
    acc = jnp.broadcast_to(b, (tr, cw))
    for k, off in enumerate(offsets):
        wk = w[k:k + 1, :]
        if off == 0:
            y = x
        elif off < 0:
            s = -off
            r = pltpu.roll(x, s, 0)
            rp = pltpu.roll(xp, s, 0)[0:8]
            head = jnp.where(row8 < s, rp, r[0:8])
            y = jnp.concatenate([head, r[8:]], axis=0)
        else:
            r = pltpu.roll(x, tr - off, 0)
            rn = pltpu.roll(xn, HALO - off, 0)[HALO - 8:HALO]
            tail = jnp.where(row8 >= 8 - off, rn, r[tr - 8:])
            y = jnp.concatenate([r[:tr - 8], tail], axis=0)
        acc = acc + wk * y
    return acc


def _attn_kernel(sink_ref, q_ref, kc_ref, vc_ref, kp_ref, ko_ref, kn_ref, vp_ref, vo_ref, vn_ref,
                 *rest, tiles_per_seg, n_conv_tiles):
    n_half = n_conv_tiles
    xin = rest[0:n_half]
    bgate = rest[n_half:2 * n_half]
    cgate = rest[2 * n_half:3 * n_half]
    xin_p, cg_p, xin_n, cg_n = (rest[(3 + k) * n_half:(4 + k) * n_half] for k in range(4))
    cw_ref, cb_ref, o_ref, conv_ref = rest[7 * n_half:]
    t = pl.program_id(1)
    tt = t - 1

    first = (t == 0) | (tt == 0)
    last = (t == 0) | (tt == tiles_per_seg - 1)
    for hh in range(n_half):
        cs = slice(hh * CONV_CW, (hh + 1) * CONV_CW)
        x = xin[hh][...].astype(F32) * cgate[hh][...].astype(F32)
        xp = xin_p[hh][...].astype(F32) * cg_p[hh][...].astype(F32)
        xn = xin_n[hh][...].astype(F32) * cg_n[hh][...].astype(F32)
        acc = _conv_taps(x, xp, xn, first, last, cw_ref[:, cs], cb_ref[:, cs], SHORT_CONV_OFFSETS)
        conv_ref[:, cs] = (acc * bgate[hh][...].astype(F32)).astype(conv_ref.dtype)

    tr = q_ref.shape[0]
    n_ctx = kc_ref.shape[0]
    halo = kp_ref.shape[0]
    n_win = tr + 2 * halo
    qi = lax.broadcasted_iota(jnp.int32, (tr, n_win), 0)
    c = lax.broadcasted_iota(jnp.int32, (tr, n_win), 1)
    ok = (c >= qi) & (c <= qi + 2 * WINDOW)
    ok = ok & ((c >= halo) | (tt > 0)) & ((c < halo + tr) | (tt < tiles_per_seg - 1)) & (t > 0)
    bias = jnp.concatenate([jnp.zeros((tr, n_ctx), F32), jnp.where(ok, 0.0, NEG_INF)], axis=1)
    for hk in range(N_KV_HEADS):
        cs = slice(hk * HEAD_DIM, (hk + 1) * HEAD_DIM)
        k_all = jnp.concatenate([kc_ref[:, cs], kp_ref[:, cs], ko_ref[:, cs], kn_ref[:, cs]], axis=0)
        v_all = jnp.concatenate([vc_ref[:, cs], vp_ref[:, cs], vo_ref[:, cs], vn_ref[:, cs]], axis=0)
        for g in range(Q_PER_KV):
            hq = hk * Q_PER_KV + g
            qs = slice(hq * HEAD_DIM, (hq + 1) * HEAD_DIM)
            s = lax.dot_general(q_ref[:, qs], k_all, (((1,), (1,)), ((), ())),
                                preferred_element_type=F32) + bias
            sink = sink_ref[hq]
            m = jnp.maximum(jnp.max(s, axis=-1, keepdims=True), sink)
            p = jnp.exp(s - m)
            denom = jnp.sum(p, axis=-1, keepdims=True) + jnp.exp(sink - m)
            o = jnp.dot(p.astype(BF16), v_all, preferred_element_type=F32)
            o_ref[:, qs] = (o / denom).astype(o_ref.dtype)


def _attention(z, sinks, conv_w, conv_b, n_batch, n_ctx_tiles, tiles_per_seg):
    r = z.shape[0]
    q_dim = N_Q_HEADS * HEAD_DIM
    kv_dim = N_KV_HEADS * HEAD_DIM
    conv_dim = conv_w.shape[1]
    k_col = q_dim // kv_dim
    v_col = k_col + 1
    halo = WINDOW
    per = TR // halo
    n_halo_blk = r // halo
    n_cv = conv_dim // CONV_CW
    x_col = (q_dim + 2 * kv_dim) // CONV_CW
    bg_col, cg_col = x_col + n_cv, x_col + 2 * n_cv
    nh = TR // HALO
    n_conv_halo = r // HALO

    def qblk(b, t):
        return jnp.where(t == 0, b, n_ctx_tiles + b * tiles_per_seg + t - 1)

    def own(col):
        return pl.BlockSpec((TR, kv_dim), lambda b, t: (qblk(b, t), col))

    def ctx(col):
        return pl.BlockSpec((TR, kv_dim), lambda b, t: (b, col))

    def prev(col):
        return pl.BlockSpec((halo, kv_dim), lambda b, t: (jnp.maximum(qblk(b, t) * per - 1, 0), col))

    def nxt(col):
        return pl.BlockSpec((halo, kv_dim),
                            lambda b, t: (jnp.minimum((qblk(b, t) + 1) * per, n_halo_blk - 1), col))

    def cv_main(col0):
        return [pl.BlockSpec((TR, CONV_CW), lambda b, t, c=col0 + k: (qblk(b, t), c)) for k in range(n_cv)]

    def cv_prev(col0):
        return [pl.BlockSpec((HALO, CONV_CW), lambda b, t, c=col0 + k: (jnp.maximum(qblk(b, t) * nh - 1, 0), c))
                for k in range(n_cv)]

    def cv_next(col0):
        return [pl.BlockSpec((HALO, CONV_CW),
                             lambda b, t, c=col0 + k: (jnp.minimum((qblk(b, t) + 1) * nh, n_conv_halo - 1), c))
                for k in range(n_cv)]

    conv_specs = (cv_main(x_col) + cv_main(bg_col) + cv_main(cg_col)
                  + cv_prev(x_col) + cv_prev(cg_col) + cv_next(x_col) + cv_next(cg_col))
    return pl.pallas_call(
        functools.partial(_attn_kernel, tiles_per_seg=tiles_per_seg, n_conv_tiles=n_cv),
        out_shape=(jax.ShapeDtypeStruct((r, q_dim), BF16), jax.ShapeDtypeStruct((r, conv_dim), BF16)),
        grid=(n_batch, 1 + tiles_per_seg),
        in_specs=[pl.BlockSpec(memory_space=pltpu.SMEM),
                  pl.BlockSpec((TR, q_dim), lambda b, t: (qblk(b, t), 0)),
                  ctx(k_col), ctx(v_col),
                  prev(k_col), own(k_col), nxt(k_col),
                  prev(v_col), own(v_col), nxt(v_col)] + conv_specs + [
                      pl.BlockSpec((len(SHORT_CONV_OFFSETS), conv_dim), lambda b, t: (0, 0)),
                      pl.BlockSpec((1, conv_dim), lambda b, t: (0, 0))],
        out_specs=(pl.BlockSpec((TR, q_dim), lambda b, t: (qblk(b, t), 0)),
                   pl.BlockSpec((TR, conv_dim), lambda b, t: (qblk(b, t), 0))),
        compiler_params=pltpu.CompilerParams(dimension_semantics=("arbitrary", "arbitrary")),
        name="attention",
    )(sinks, *([z] * (9 + 7 * n_cv)), conv_w, conv_b.reshape(1, conv_dim))


def _l0_out_kernel(ctx_ref, x_ref, a1_ref, a2_ref, w_ref, gate_ref, oc_ref, ol_ref):
    i = pl.program_id(0)
    lhs = jnp.concatenate([a1_ref[...], a2_ref[...]], axis=1)
    d = ol_ref.shape[1]
    for c0 in range(0, d, TN_CHUNK):
        cs = slice(c0, c0 + TN_CHUNK)
        y = jnp.dot(lhs, w_ref[:, cs], preferred_element_type=F32)
        h = jnp.where(i == 0, ctx_ref[:, cs], x_ref[:, cs])
        res = h + gate_ref[:, cs] * y
        oc_ref[:, cs] = res
        ol_ref[:, cs] = res


def _l0_out_proj(attn, conv, w_out, ctx2, x2, mods, tile_to_mod):
    n_ctx_rows, d = ctx2.shape
    n_lat = x2.shape[0]
    r = n_ctx_rows + n_lat
    k1 = attn.shape[1]
    k2 = conv.shape[1]
    assert w_out.shape[0] == k1 + k2 and d % TN_CHUNK == 0
    return pl.pallas_call(
        _l0_out_kernel,
        out_shape=(jax.ShapeDtypeStruct((2, TM, d), F32), jax.ShapeDtypeStruct((n_lat, d), F32)),
        grid=(r // TM,),
        in_specs=[_resident((TM, d)),
                  pl.BlockSpec((TM, d), lambda i: (jnp.maximum(i - 1, 0), 0)),
                  pl.BlockSpec((TM, k1), lambda i: (i, 0)),
                  pl.BlockSpec((TM, k2), lambda i: (i, 0)),
                  _resident((k1 + k2, d)),
                  _mod_spec1(2, d, tile_to_mod)],
        out_specs=(pl.BlockSpec((None, TM, d), lambda i: (jnp.minimum(i, 1), 0, 0)),
                   pl.BlockSpec((TM, d), lambda i: (jnp.maximum(i - 1, 0), 0))),
        compiler_params=pltpu.CompilerParams(
            dimension_semantics=("arbitrary",),
            vmem_limit_bytes=_vmem_limit(3 * TM * d * 4 + 3 * TM * (k1 + k2) * 2 + (k1 + k2) * d * 2
                                         + 4 * TM * d * 4 + 4 * TM * TN_CHUNK * 4)),
        name="l0_out_proj",
    )(ctx2, x2, attn, conv, w_out, mods)


def _ffn_kernel(*refs, cast_blocks):
    n_cast = len(cast_blocks)
    h_ref, g_ref, sh_ref, sc_ref, gate_ref, wg_ref, wu_ref, wd_ref = refs[:8]
    src = refs[8:8 + n_cast]
    o_ref = refs[8 + n_cast]
    dst = refs[9 + n_cast:9 + 2 * n_cast]
    u_scr = refs[9 + 2 * n_cast]
    i = pl.program_id(0)
    j = pl.program_id(1)
    nj = pl.num_programs(1)
    cb = CAST_BLOCK

    if n_cast:
        cin, cout, sem_in, sem_out = refs[10 + 2 * n_cast:]
        n_blocks = sum(ne * nbr * nbc for ne, nbr, nbc in cast_blocks)
        s = i * nj + j
        slot = s % 2

        def for_block(b, fn):
            off = 0
            for k, (ne, nbr, nbc) in enumerate(cast_blocks):
                nk = ne * nbr * nbc

                @pl.when((b >= off) & (b < off + nk))
                def _(k=k, off=off, nbr=nbr, nbc=nbc):
                    q = b - off
                    rc = q % (nbr * nbc)
                    fn(k, q // (nbr * nbc), rc // nbc, rc % nbc)
                off += nk

        def block_of(ref, e, r, c):
            return ref.at[e, pl.ds(r * cb, cb), pl.ds(c * cb, cb)]

        def start_in(b, to_slot):
            for_block(b, lambda k, e, r, c: pltpu.make_async_copy(
                block_of(src[k], e, r, c), cin.at[to_slot], sem_in.at[to_slot]).start(priority=1))

        @pl.when(s == 0)
        def _():
            start_in(0, 0)

        @pl.when(s < n_blocks)
        def _():
            pltpu.make_async_copy(block_of(src[0], 0, 0, 0), cin.at[slot], sem_in.at[slot]).wait()

        @pl.when(s + 1 < n_blocks)
        def _():
            start_in(s + 1, 1 - slot)

        @pl.when((s >= 2) & (s - 2 < n_blocks))
        def _():
            pltpu.make_async_copy(cout.at[slot], block_of(dst[0], 0, 0, 0), sem_out.at[slot]).wait()

    @pl.when(j == 0)
    def _():
        u_scr[...] = _norm_mod(h_ref[...], g_ref[...], sh_ref[...], sc_ref[...]).astype(BF16)

    if n_cast:
        cout[slot] = cin[slot].astype(BF16)
    u = u_scr[...]
    gt = jnp.dot(u, wg_ref[...], preferred_element_type=F32)
    up = jnp.dot(u, wu_ref[...], preferred_element_type=F32)
    act = (gt * jax.nn.sigmoid(gt) * up).astype(BF16)
    y = jnp.dot(act, wd_ref[...], preferred_element_type=F32)

    @pl.when(j == 0)
    def _():
        o_ref[...] = y

    @pl.when(j != 0)
    def _():
        o_ref[...] += y

    @pl.when(j == nj - 1)
    def _():
        o_ref[...] = h_ref[...] + gate_ref[...] * o_ref[...]

    if n_cast:
        @pl.when(s < n_blocks)
        def _():
            for_block(s, lambda k, e, r, c: pltpu.make_async_copy(
                cout.at[slot], block_of(dst[k], e, r, c), sem_out.at[slot]).start(priority=1))


def _ffn(h, h_spec, rows, tm, g, mods, w_gate, w_up, w_down, tile_to_mod, riders=()):
    d = h.shape[-1]
    f = w_gate.shape[1]
    tf = TF_FFN
    grid = (rows // tm, f // tf)
    cast_blocks = tuple((w.shape[0], w.shape[1] // CAST_BLOCK, w.shape[2] // CAST_BLOCK) for w in riders)
    for w in riders:
        assert w.shape[1] % CAST_BLOCK == 0 and w.shape[2] % CAST_BLOCK == 0
    n_blocks = sum(ne * nbr * nbc for ne, nbr, nbc in cast_blocks)
    assert not riders or grid[0] * grid[1] >= n_blocks + 2
    any_spec = pl.BlockSpec(memory_space=pl.ANY)
    scratch = [pltpu.VMEM((tm, d), BF16)]
    if riders:
        scratch += [pltpu.VMEM((2, CAST_BLOCK, CAST_BLOCK), F32), pltpu.VMEM((2, CAST_BLOCK, CAST_BLOCK), BF16),
                    pltpu.SemaphoreType.DMA((2,)), pltpu.SemaphoreType.DMA((2,))]
    return pl.pallas_call(
        functools.partial(_ffn_kernel, cast_blocks=cast_blocks),
        out_shape=(jax.ShapeDtypeStruct((rows, d), F32),
                   *[jax.ShapeDtypeStruct(w.shape, BF16) for w in riders]),
        grid=grid,
        in_specs=[h_spec,
                  pl.BlockSpec((1, d), lambda i, j: (0, 0)),
                  _mod_spec(3, d, tile_to_mod),
                  _mod_spec(4, d, tile_to_mod),
                  _mod_spec(5, d, tile_to_mod),
                  pl.BlockSpec((d, tf), lambda i, j: (0, j)),
                  pl.BlockSpec((d, tf), lambda i, j: (0, j)),
                  pl.BlockSpec((tf, d), lambda i, j: (j, 0))] + [any_spec] * len(riders),
        out_specs=(pl.BlockSpec((tm, d), lambda i, j: (i, 0)), *[any_spec] * len(riders)),
        scratch_shapes=scratch,
        compiler_params=pltpu.CompilerParams(
            dimension_semantics=("arbitrary", "arbitrary"),
            vmem_limit_bytes=_vmem_limit(4 * tm * d * 4 + 6 * d * tf * 2 + tm * d * 2
                                         + 3 * tm * tf * 4 + tm * tf * 2
                                         + (12 * CAST_BLOCK * CAST_BLOCK if riders else 0))),
        name="ffn",
    )(h, g.reshape(1, d), mods, mods, mods, w_gate, w_up, w_down, *riders)


def _l1_in_kernel(hc_ref, hl_ref, g_ref, sh_ref, sc_ref, w_ref, gg_ref, xr_ref):
    h = jnp.where(pl.program_id(0) == 0, hc_ref[...], hl_ref[...])
    u = _norm_mod(h, g_ref[...], sh_ref[...], sc_ref[...]).astype(BF16)
    d_rnn = gg_ref.shape[1]
    for c0 in range(0, 2 * d_rnn, TN_CHUNK):
        z = jnp.dot(u, w_ref[:, c0:c0 + TN_CHUNK], preferred_element_type=F32)
        if c0 < d_rnn:
            gg_ref[:, c0:c0 + TN_CHUNK] = jax.nn.gelu(z, approximate=True).astype(gg_ref.dtype)
        else:
            xr_ref[:, c0 - d_rnn:c0 - d_rnn + TN_CHUNK] = z


def _l1_in_proj(h_ctx, h_lat, g, mods, w_in, d_rnn, tile_to_mod):
    n_ctx_rows, d = h_ctx.shape
    assert n_ctx_rows == TM
    r = n_ctx_rows + h_lat.shape[0]
    assert w_in.shape[1] == 2 * d_rnn and d_rnn % TN_CHUNK == 0
    return pl.pallas_call(
        _l1_in_kernel,
        out_shape=(jax.ShapeDtypeStruct((r, d_rnn), BF16), jax.ShapeDtypeStruct((r, d_rnn), F32)),
        grid=(r // TM,),
        in_specs=[_resident((TM, d)),
                  pl.BlockSpec((TM, d), lambda i: (jnp.maximum(i - 1, 0), 0)),
                  pl.BlockSpec((1, d), lambda i: (0, 0)),
                  _mod_spec1(0, d, tile_to_mod),
                  _mod_spec1(1, d, tile_to_mod),
                  _resident((d, 2 * d_rnn))],
        out_specs=(pl.BlockSpec((TM, d_rnn), lambda i: (i, 0)),
                   pl.BlockSpec((TM, d_rnn), lambda i: (i, 0))),
        compiler_params=pltpu.CompilerParams(
            dimension_semantics=("arbitrary",),
            vmem_limit_bytes=_vmem_limit(2 * TM * d * 4 + d * 2 * d_rnn * 2 + 2 * TM * d_rnn * 6
                                         + 2 * TM * d * 4 + 4 * TM * TN_CHUNK * 4)),
        name="l1_in_proj",
    )(h_ctx, h_lat, g.reshape(1, d), mods, mods, w_in)


def _rglru_kernel(xf_ref, xfp_ref, xfn_ref, xb_ref, xbp_ref, xbn_ref, cw_ref, cb_ref,
                  gaw_ref, gxw_ref, gab_ref, gxb_ref, lam_ref, hf_ref, hb_ref,
                  xc_scr, a_scr, b_scr, h_scr, *, tiles_per_seg):
    c = pl.program_id(1)
    t_len, d_rnn = xf_ref.shape
    hd = d_rnn // N_RNN_HEADS

    @pl.when(c == 0)
    def _():
        h_scr[...] = jnp.zeros_like(h_scr)

    is_ctx = c == 0
    chunk = (c - 1, tiles_per_seg - c)
    halos = ((xfp_ref, xfn_ref), (xbp_ref, xbn_ref))

    for z, x_ref in ((0, xf_ref), (1, xb_ref)):
        first = is_ctx | (chunk[z] == 0)
        last = is_ctx | (chunk[z] == tiles_per_seg - 1)
        xc_scr[...] = _conv_taps(x_ref[...], halos[z][0][...], halos[z][1][...], first, last,
                                 cw_ref[...], cb_ref[...], RG_CONV_OFFSETS)
        x_ref = xc_scr
        neg_lam = -lam_ref[z]
        sp = jnp.maximum(neg_lam, 0.0) + jnp.log1p(jnp.exp(-jnp.abs(neg_lam)))
        half_k = sp * (-0.5 * RG_C * LOG2_E)
        for hh in range(N_RNN_HEADS):
            sl = slice(hh * hd, (hh + 1) * hd)
            xh = x_ref[:, sl]
            xh16 = xh.astype(BF16)
            ta = jnp.tanh(jnp.dot(xh16, gaw_ref[z, hh], preferred_element_type=F32) + gab_ref[z][:, sl])
            ti = jnp.tanh(jnp.dot(xh16, gxw_ref[z, hh], preferred_element_type=F32) + gxb_ref[z][:, sl])
            hk = half_k[:, sl]
            a = jnp.exp2(ta * hk + hk)
            om = 1.0 - a * a
            root = jnp.where(om > 0.0, om * lax.rsqrt(om), 0.0)
            rx = (0.5 * root) * xh
            a_scr[z, :, sl] = a
            b_scr[z, :, sl] = rx * ti + rx

    row8 = lax.broadcasted_iota(jnp.int32, (8, d_rnn), 0)
    n_grp = t_len // 8

    def fwd_group(r0, h):
        a = a_scr[0, pl.ds(r0, 8), :]
        b = b_scr[0, pl.ds(r0, 8), :]
        for s in (1, 2, 4):
            a_sh = jnp.where(row8 >= s, pltpu.roll(a, s, 0), 1.0)
            b_sh = jnp.where(row8 >= s, pltpu.roll(b, s, 0), 0.0)
            b = a * b_sh + b
            a = a * a_sh
        out = a * h + b
        return out, jnp.broadcast_to(out[7:8, :], (8, d_rnn))

    def bwd_group(r0, h):
        a = a_scr[1, pl.ds(r0, 8), :]
        b = b_scr[1, pl.ds(r0, 8), :]
        for s in (1, 2, 4):
            a_sh = jnp.where(row8 < 8 - s, pltpu.roll(a, 8 - s, 0), 1.0)
            b_sh = jnp.where(row8 < 8 - s, pltpu.roll(b, 8 - s, 0), 0.0)
            b = a * b_sh + b
            a = a * a_sh
        out = a * h + b
        return out, jnp.broadcast_to(out[0:1, :], (8, d_rnn))

    def fwd_body(g, h):
        r0 = pl.multiple_of(g * 16, 16)
        lo, h = fwd_group(r0, h)
        hi, h = fwd_group(r0 + 8, h)
        hf_ref[pl.ds(r0, 16), :] = jnp.concatenate([lo, hi], axis=0).astype(hf_ref.dtype)
        return h

    def bwd_body(k, h):
        r0 = pl.multiple_of((n_grp // 2 - 1 - k) * 16, 16)
        hi, h = bwd_group(r0 + 8, h)
        lo, h = bwd_group(r0, h)
        hb_ref[pl.ds(r0, 16), :] = jnp.concatenate([lo, hi], axis=0).astype(hb_ref.dtype)
        return h

    h_scr[0] = lax.fori_loop(0, n_grp // 2, fwd_body, h_scr[0])
    h_scr[1] = lax.fori_loop(0, n_grp // 2, bwd_body, h_scr[1])


def _rglru(xr, conv_w, conv_b, ga_w, gx_w, ga_b, gx_b, lam, n_batch, n_ctx_tiles, tiles_per_seg):
    r, d_rnn = xr.shape
    n_lat = n_batch * tiles_per_seg * TR
    hd = d_rnn // N_RNN_HEADS
    nt = tiles_per_seg
    nh = TR // HALO
    n_halo = r // HALO
    n_taps = conv_w.shape[0]

    def fblk(b, c):
        return jnp.where(c == 0, b, n_ctx_tiles + b * nt + c - 1)

    def bblk(b, c):
        return jnp.where(c == 0, b, n_ctx_tiles + b * nt + nt - c)

    def chunk_specs(blk):
        return [pl.BlockSpec((TR, d_rnn), lambda b, c: (blk(b, c), 0)),
                pl.BlockSpec((HALO, d_rnn), lambda b, c: (jnp.maximum(blk(b, c) * nh - 1, 0), 0)),
                pl.BlockSpec((HALO, d_rnn), lambda b, c: (jnp.minimum((blk(b, c) + 1) * nh, n_halo - 1), 0))]

    def full(shape):
        return pl.BlockSpec(shape, lambda b, c: (0,) * len(shape))

    return pl.pallas_call(
        functools.partial(_rglru_kernel, tiles_per_seg=nt),
        out_shape=(jax.ShapeDtypeStruct((n_lat, d_rnn), BF16), jax.ShapeDtypeStruct((n_lat, d_rnn), BF16)),
        grid=(n_batch, 1 + nt),
        in_specs=chunk_specs(fblk) + chunk_specs(bblk) + [
            full((n_taps, d_rnn)), full((1, d_rnn)),
            full((2, N_RNN_HEADS, hd, hd)), full((2, N_RNN_HEADS, hd, hd)),
            full((2, 1, d_rnn)), full((2, 1, d_rnn)), full((2, 1, d_rnn))],
        out_specs=(pl.BlockSpec((TR, d_rnn), lambda b, c: (b * nt + jnp.maximum(c - 1, 0), 0)),
                   pl.BlockSpec((TR, d_rnn), lambda b, c: (b * nt + jnp.where(c == 0, nt - 1, nt - c), 0))),
        scratch_shapes=[pltpu.VMEM((TR, d_rnn), F32),
                        pltpu.VMEM((2, TR, d_rnn), F32), pltpu.VMEM((2, TR, d_rnn), F32),
                        pltpu.VMEM((2, 8, d_rnn), F32)],
        compiler_params=pltpu.CompilerParams(
            dimension_semantics=("arbitrary", "arbitrary"),
            vmem_limit_bytes=_vmem_limit(14 * TR * d_rnn * 4 + 8 * N_RNN_HEADS * hd * hd * 2)),
        name="rglru",
    )(xr, xr, xr, xr, xr, xr, conv_w, conv_b.reshape(1, d_rnn), ga_w, gx_w,
      ga_b.reshape(2, 1, d_rnn), gx_b.reshape(2, 1, d_rnn), lam.reshape(2, 1, d_rnn))


def _l1_out_kernel(hf_ref, hb_ref, gg_ref, w_ref, h_ref, gate_ref, o_ref):
    rec = hf_ref[...].astype(F32) + hb_ref[...].astype(F32)
    lhs = (rec * gg_ref[...].astype(F32)).astype(BF16)
    d = o_ref.shape[1]
    for c0 in range(0, d, TN_CHUNK):
        cs = slice(c0, c0 + TN_CHUNK)
        y = jnp.dot(lhs, w_ref[:, cs], preferred_element_type=F32)
        o_ref[:, cs] = h_ref[:, cs] + gate_ref[:, cs] * y


def _l1_out_proj(hf, hb, gg, w_out, h, mods, n_ctx_mtiles, lat_tile_to_mod):
    n_lat, d_rnn = hf.shape
    d = h.shape[1]
    assert d % TN_CHUNK == 0
    return pl.pallas_call(
        _l1_out_kernel,
        out_shape=jax.ShapeDtypeStruct((n_lat, d), F32),
        grid=(n_lat // TM,),
        in_specs=[pl.BlockSpec((TM, d_rnn), lambda i: (i, 0)),
                  pl.BlockSpec((TM, d_rnn), lambda i: (i, 0)),
                  pl.BlockSpec((TM, d_rnn), lambda i: (i + n_ctx_mtiles, 0)),
                  _resident((d_rnn, d)),
                  pl.BlockSpec((TM, d), lambda i: (i, 0)),
                  _mod_spec1(2, d, lat_tile_to_mod)],
        out_specs=pl.BlockSpec((TM, d), lambda i: (i, 0)),
        compiler_params=pltpu.CompilerParams(
            dimension_semantics=("arbitrary",),
            vmem_limit_bytes=_vmem_limit(6 * TM * d_rnn * 2 + d_rnn * d * 2 + 4 * TM * d * 4
                                         + 2 * TM * d_rnn * 4 + 4 * TM * TN_CHUNK * 4)),
        name="l1_out_proj",
    )(hf, hb, gg, w_out, h, mods)


def _router_kernel(h_ref, g_ref, sh_ref, sc_ref, rw_ref, rb_ref, u_ref, route_ref):
    u = _norm_mod(h_ref[...], g_ref[...], sh_ref[...], sc_ref[...])
    u_ref[...] = u
    logits = jnp.dot(u.astype(BF16), rw_ref[...], preferred_element_type=F32) + rb_ref[...]
    lane = lax.broadcasted_iota(jnp.int32, logits.shape, 1)
    m1 = jnp.max(logits, axis=-1, keepdims=True)
    i1 = jnp.min(jnp.where(logits == m1, lane, LANES), axis=-1, keepdims=True)
    rest = jnp.where(lane == i1, -jnp.inf, logits)
    m2 = jnp.max(rest, axis=-1, keepdims=True)
    i2 = jnp.min(jnp.where(rest == m2, lane, LANES), axis=-1, keepdims=True)
    e2 = jnp.exp(m2 - m1)
    w1 = 1.0 / (1.0 + e2)
    w2 = e2 * w1
    route = jnp.where(lane == 0, i1.astype(F32),
                      jnp.where(lane == 1, i2.astype(F32),
                                jnp.where(lane == 2, w1, jnp.where(lane == 3, w2, 0.0))))
    route_ref[...] = route


def _router(h, g, mods, rw_pad, rb_pad, lat_tile_to_mod):
    n, d = h.shape

    def mspec(chunk):
        return pl.BlockSpec((None, 1, d), lambda i: (lat_tile_to_mod(i) * N_MOD + chunk, 0, 0))

    return pl.pallas_call(
        _router_kernel,
        out_shape=(jax.ShapeDtypeStruct((n, d), F32), jax.ShapeDtypeStruct((n, LANES), F32)),
        grid=(n // TM,),
        in_specs=[pl.BlockSpec((TM, d), lambda i: (i, 0)),
                  pl.BlockSpec((1, d), lambda i: (0, 0)),
                  mspec(3), mspec(4),
                  pl.BlockSpec((d, LANES), lambda i: (0, 0)),
                  pl.BlockSpec((1, LANES), lambda i: (0, 0))],
        out_specs=(pl.BlockSpec((TM, d), lambda i: (i, 0)),
                   pl.BlockSpec((TM, LANES), lambda i: (i, 0))),
        compiler_params=pltpu.CompilerParams(
            dimension_semantics=("arbitrary",),
            vmem_limit_bytes=_vmem_limit(6 * TM * d * 4)),
        name="router",
    )(h, g.reshape(1, d), mods, mods, rw_pad, rb_pad)


def _row_copy(src, src_row, dst, dst_row, sem):
    return pltpu.make_async_copy(src.at[pl.ds(src_row, 1)], dst.at[pl.ds(dst_row, 1)], sem)


def _moe_kernel(te_ref, tv_ref, tok_ref, tok_next_ref, u_hbm, wg_ref, wu_ref, wd_ref, o_ref,
                xg_scr, x_scr, sems, *, gather_steps):
    del te_ref
    i = pl.program_id(0)
    j = pl.program_id(1)
    n_tiles = pl.num_programs(0)
    rows = x_scr.shape[0]
    per_step = rows // gather_steps
    slot = i % 2
    valid = tv_ref[i] == 1
    next_valid = (i + 1 < n_tiles) & (tv_ref[jnp.minimum(i + 1, n_tiles - 1)] == 1)

    @pl.when(valid & (j == 0))
    def _():
        @pl.when(i == 0)
        def _():
            def start(r, carry):
                _row_copy(u_hbm, tok_ref[r], xg_scr.at[0], r, sems.at[0]).start()
                return carry
            lax.fori_loop(0, rows, start, 0)

        pltpu.make_async_copy(u_hbm.at[pl.ds(0, rows)], xg_scr.at[slot], sems.at[slot]).wait()
        x_scr[...] = xg_scr[slot].astype(BF16)

    def step(with_gather):
        if with_gather:
            r0 = j * per_step
            for r in range(per_step):
                _row_copy(u_hbm, tok_next_ref[r0 + r], xg_scr.at[1 - slot], r0 + r, sems.at[1 - slot]).start()
        x = x_scr[...]
        gt = jnp.dot(x, wg_ref[...], preferred_element_type=F32)
        up = jnp.dot(x, wu_ref[...], preferred_element_type=F32)
        act = (gt * jax.nn.sigmoid(gt) * up).astype(BF16)
        y = jnp.dot(act, wd_ref[...], preferred_element_type=F32)

        @pl.when(j == 0)
        def _():
            o_ref[...] = y

        @pl.when(j != 0)
        def _():
            o_ref[...] += y

    gather_now = next_valid & (j < gather_steps)

    @pl.when(valid & gather_now)
    def _():
        step(True)

    @pl.when(valid & jnp.logical_not(gather_now))
    def _():
        step(False)

    @pl.when((tv_ref[i] == 0) & (j == 0))
    def _():
        o_ref[...] = jnp.zeros_like(o_ref)


def _moe(u, slot_tok, tile_e, tile_valid, w_gate, w_up, w_down):
    d = u.shape[1]
    cap = slot_tok.shape[0]
    n_tiles = cap // TM
    f = w_gate.shape[2]
    tf = TF_MOE
    nj = f // tf
    gather_steps = min(MOE_GATHER_STEPS, nj)
    assert TM % gather_steps == 0

    def jj(i, j, tv):
        return jnp.where(tv[i] == 1, j, nj - 1)

    return pl.pallas_call(
        functools.partial(_moe_kernel, gather_steps=gather_steps),
        out_shape=jax.ShapeDtypeStruct((cap, d), F32),
        grid_spec=pltpu.PrefetchScalarGridSpec(
            num_scalar_prefetch=2,
            grid=(n_tiles, nj),
            in_specs=[pl.BlockSpec((TM,), lambda i, j, te, tv: (i,), memory_space=pltpu.SMEM),
                      pl.BlockSpec((TM,), lambda i, j, te, tv: (jnp.minimum(i + 1, n_tiles - 1),),
                                   memory_space=pltpu.SMEM),
                      pl.BlockSpec(memory_space=pl.ANY),
                      pl.BlockSpec((None, d, tf), lambda i, j, te, tv: (te[i], 0, jj(i, j, tv))),
                      pl.BlockSpec((None, d, tf), lambda i, j, te, tv: (te[i], 0, jj(i, j, tv))),
                      pl.BlockSpec((None, tf, d), lambda i, j, te, tv: (te[i], jj(i, j, tv), 0))],
            out_specs=pl.BlockSpec((TM, d), lambda i, j, te, tv: (i, 0)),
            scratch_shapes=[pltpu.VMEM((2, TM, d), F32), pltpu.VMEM((TM, d), BF16),
                            pltpu.SemaphoreType.DMA((2,))]),
        compiler_params=pltpu.CompilerParams(
            dimension_semantics=("arbitrary", "arbitrary"),
            vmem_limit_bytes=_vmem_limit(2 * TM * d * 4 + 2 * TM * d * 4 + 6 * d * tf * 2 + TM * d * 2
                                         + 3 * TM * tf * 4 + TM * d * 4)),
        name="moe_experts",
    )(tile_e, tile_valid, slot_tok, slot_tok, u, w_gate, w_up, w_down)


def _combine_kernel(dest_ref, route_ref, h_ref, gate_ref, fg_ref, yb_hbm, o_ref, y_scr, sem, *, rows):
    def start(r, carry):
        for kk in range(TOP_K):
            _row_copy(yb_hbm, dest_ref[TOP_K * r + kk], y_scr.at[kk], r, sem).start()
        return carry

    lax.fori_loop(0, rows, start, 0)
    for kk in range(TOP_K):
        pltpu.make_async_copy(yb_hbm.at[pl.ds(0, rows)], y_scr.at[kk], sem).wait()

    route = route_ref[...]
    moe = route[:, 2:3] * y_scr[0] + route[:, 3:4] * y_scr[1]
    hl = h_ref[...] + gate_ref[...] * moe
    ms = jnp.mean(hl * hl, axis=-1, keepdims=True)
    o_ref[...] = (hl * lax.rsqrt(ms + NORM_EPS)) * fg_ref[...]


def _combine(yb, dest, route, h, mods, final_g, lat_tile_to_mod):
    n, d = h.shape
    return pl.pallas_call(
        functools.partial(_combine_kernel, rows=TM),
        out_shape=jax.ShapeDtypeStruct((n, d), F32),
        grid=(n // TM,),
        in_specs=[pl.BlockSpec((TOP_K * TM,), lambda i: (i,), memory_space=pltpu.SMEM),
                  pl.BlockSpec((TM, LANES), lambda i: (i, 0)),
                  pl.BlockSpec((TM, d), lambda i: (i, 0)),
                  pl.BlockSpec((None, 1, d), lambda i: (lat_tile_to_mod(i) * N_MOD + 5, 0, 0)),
                  pl.BlockSpec((1, d), lambda i: (0, 0)),
                  pl.BlockSpec(memory_space=pl.ANY)],
        out_specs=pl.BlockSpec((TM, d), lambda i: (i, 0)),
        scratch_shapes=[pltpu.VMEM((TOP_K, TM, d), F32), pltpu.SemaphoreType.DMA],
        compiler_params=pltpu.CompilerParams(
            dimension_semantics=("arbitrary",),
            vmem_limit_bytes=_vmem_limit(TOP_K * TM * d * 4 + 4 * TM * d * 4 + 2 * TM * d * 4)),
        name="moe_combine",
    )(dest, route, h, mods, final_g.reshape(1, d), yb)


def _rope_tables(n_ident, s_len):
    t = jnp.arange(s_len)
    row_id = (t // GRID_W).astype(F32)
    col_id = (t % GRID_W).astype(F32)
    inv_freq = ROPE_BASE ** (-jnp.arange(ROPE_PAIRS, dtype=F32) / ROPE_PAIRS)
    ang_r = row_id[:, None] * inv_freq
    ang_c = col_id[:, None] * inv_freq
    ang = jnp.concatenate([ang_r, ang_r, ang_c, ang_c], axis=-1)
    cos, sin = jnp.cos(ang), jnp.sin(ang)
    first_half = (jnp.arange(HEAD_DIM) % (2 * ROPE_PAIRS)) < ROPE_PAIRS
    sa = jnp.where(first_half, -sin, 0.0)
    sb = jnp.where(first_half, 0.0, sin)
    ones = jnp.ones((n_ident, HEAD_DIM), F32)
    zeros = jnp.zeros((n_ident, HEAD_DIM), F32)
    return (jnp.concatenate([ones, cos]), jnp.concatenate([zeros, sa]), jnp.concatenate([zeros, sb]))


def _routing_plan(route, n_tiles):
    e = route[:, :TOP_K].astype(jnp.int32).reshape(-1)
    onehot = (e[:, None] == jnp.arange(N_EXPERTS, dtype=jnp.int32)[None, :]).astype(jnp.int32)
    csum = jnp.cumsum(onehot, axis=0)
    rank = jnp.sum(csum * onehot, axis=1) - 1
    counts = csum[-1]
    padded = (counts + TM - 1) // TM * TM
    pad_end = jnp.cumsum(padded)
    pad_start = pad_end - padded
    dest = jnp.sum(onehot * pad_start[None, :], axis=1) + rank
    n_valid = pad_end[-1] // TM
    tile_idx = jnp.arange(n_tiles, dtype=jnp.int32)
    tile_valid = (tile_idx < n_valid).astype(jnp.int32)
    tile_row = jnp.minimum(tile_idx, n_valid - 1)
    tile_e = jnp.sum((pad_end[None, :] <= (tile_row * TM)[:, None]).astype(jnp.int32), axis=1)
    tile_e = jnp.minimum(tile_e, N_EXPERTS - 1)
    dest = dest.astype(jnp.int32)
    slot_tok = jnp.zeros((n_tiles * TM,), jnp.int32).at[dest].set(
        jnp.arange(e.shape[0], dtype=jnp.int32) // TOP_K)
    return dest, slot_tok, tile_e.astype(jnp.int32), tile_valid


def _mods(cvecs, w_mod, b_mod):
    d = w_mod.shape[0]
    m = _adaln(cvecs, w_mod, b_mod)[:3]
    return m.reshape(3 * N_MOD, 1, d)


def kernel(x, c, ctx, c_ctx, l0_w_mod, l0_b_mod, l0_norm1_g, l0_w_in, l0_sinks, l0_conv_w, l0_conv_b, l0_w_out, l0_norm2_g, l0_ffn_w_gate, l0_ffn_w_up, l0_ffn_w_down, l1_w_mod, l1_b_mod, l1_norm1_g, l1_w_in, l1_conv_w, l1_conv_b, l1_gate_a_w, l1_gate_a_b, l1_gate_x_w, l1_gate_x_b, l1_lambda, l1_w_out, l1_norm2_g, l1_router_w, l1_router_b, l1_moe_w_gate, l1_moe_w_up, l1_moe_w_down, final_norm_g):
    n_batch, s_len, d = x.shape
    n_ctx = ctx.shape[1]
    assert n_batch == 2 and n_batch * n_ctx == TM and n_ctx == TR
    assert s_len % TM == 0 and s_len % GRID_W == 0 and WINDOW * 2 == TR
    n_ctx_rows = n_batch * n_ctx
    n_lat = n_batch * s_len
    n_ctx_rtiles = n_ctx_rows // TR
    tiles_per_seg = s_len // TR
    mtiles_per_seg = s_len // TM
    d_rnn = l1_w_out.shape[0]

    def tile_to_mod(i):
        return jnp.where(i == 0, n_batch, (i - 1) // mtiles_per_seg)

    def lat_tile_to_mod(i):
        return i // mtiles_per_seg

    def rope_blk(i):
        return jnp.where(i == 0, 0, 1 + (i - 1) % mtiles_per_seg)

    cvecs = jnp.concatenate([c, c_ctx[None, :], jnp.zeros((8 - n_batch - 1, d), F32)], axis=0)
    mods0 = _mods(cvecs, l0_w_mod, l0_b_mod)
    mods1 = _mods(cvecs, l1_w_mod, l1_b_mod)
    ctx2 = ctx.reshape(n_ctx_rows, d)
    x2 = x.reshape(n_lat, d)

    rope = _rope_tables(TM, s_len)
    z = _l0_in_proj(ctx2, x2, l0_norm1_g, mods0, l0_w_in.astype(BF16), rope, tile_to_mod, rope_blk)
    attn, conv = _attention(z, l0_sinks, l0_conv_w, l0_conv_b, n_batch, n_ctx_rtiles, tiles_per_seg)
    hc2, hl = _l0_out_proj(attn, conv, l0_w_out.astype(BF16), ctx2, x2, mods0, tile_to_mod)
    ffn_w = (l0_ffn_w_gate.astype(BF16), l0_ffn_w_up.astype(BF16), l0_ffn_w_down.astype(BF16))
    (hc,) = _ffn(hc2, pl.BlockSpec((None, TM, d), lambda i, j: (0, 0, 0)), n_ctx_rows, TM,
                 l0_norm2_g, mods0, *ffn_w, lambda i: n_batch)
    hl, moe_wg, moe_wu, moe_wd = _ffn(
        hl, pl.BlockSpec((TM, d), lambda i, j: (i, 0)), n_lat, TM, l0_norm2_g, mods0, *ffn_w,
        lat_tile_to_mod, riders=(l1_moe_w_gate, l1_moe_w_up, l1_moe_w_down))

    gg, xr = _l1_in_proj(hc, hl, l1_norm1_g, mods1, l1_w_in.astype(BF16), d_rnn, tile_to_mod)
    hf, hb = _rglru(xr, l1_conv_w, l1_conv_b, (0.5 * l1_gate_a_w).astype(BF16), (0.5 * l1_gate_x_w).astype(BF16),
                    0.5 * l1_gate_a_b, 0.5 * l1_gate_x_b, l1_lambda, n_batch, n_ctx_rtiles, tiles_per_seg)
    h_lat = _l1_out_proj(hf, hb, gg, l1_w_out.astype(BF16), hl, mods1, n_ctx_rows // TM, lat_tile_to_mod)

    rw_pad = jnp.zeros((d, LANES), BF16).at[:, :N_EXPERTS].set(l1_router_w.astype(BF16))
    rb_pad = jnp.full((1, LANES), NEG_INF, F32).at[0, :N_EXPERTS].set(l1_router_b)
    u, route = _router(h_lat, l1_norm2_g, mods1, rw_pad, rb_pad, lat_tile_to_mod)
    n_tiles = (n_lat * TOP_K) // TM + N_EXPERTS
    dest, slot_tok, tile_e, tile_valid = _routing_plan(route, n_tiles)
    yb = _moe(u, slot_tok, tile_e, tile_valid, moe_wg, moe_wu, moe_wd)
    out = _combine(yb, dest, route, h_lat, mods1, final_norm_g, lat_tile_to_mod)
    return out.reshape(n_batch, s_len, d)
```

```python
import functools

import jax
import jax.numpy as jnp
from jax import lax
from jax.experimental import pallas as pl
from jax.experimental.pallas import tpu as pltpu

F32 = jnp.float32
BF16 = jnp.bfloat16

GRID_W = 64
HEAD_DIM = 128
N_Q_HEADS = 8
N_KV_HEADS = 2
Q_PER_KV = N_Q_HEADS // N_KV_HEADS
WINDOW = 128
ATTN_SCALE = HEAD_DIM ** -0.5
ROPE_BASE = 10000.0
ROPE_PAIRS = HEAD_DIM // 4
SHORT_CONV_OFFSETS = (-1, 0, 1)
RG_CONV_OFFSETS = (-2, -1, 0, 1)
N_RNN_HEADS = 16
RG_C = 8.0
N_EXPERTS = 8
TOP_K = 2
LOG2_E = 1.4426950408889634
NORM_EPS = 1e-6
NEG_INF = -1e30
N_MOD = 6

LANES = 128
TM = 512
CAST_BLOCK = 1024
TR = 256
HALO = 16
TN_CHUNK = 512
TF_FFN = 512
TF_MOE = 1024
MOE_GATHER_STEPS = 4
CONV_CW = 512
MOD_TN = 1024
VMEM_CAP = 56 * 1024 * 1024


def _vmem_limit(nbytes):
    return int(min(max(nbytes * 5 // 4 + (4 << 20), 32 << 20), VMEM_CAP))


def _norm_mod(h, g, shift, scale):
    ms = jnp.mean(h * h, axis=-1, keepdims=True)
    y = h * lax.rsqrt(ms + NORM_EPS)
    return (y * g) * (1.0 + scale) + shift


def _adaln_kernel(c_ref, w_ref, b_ref, o_ref):
    c = c_ref[...]
    s = (c * jax.nn.sigmoid(c)).astype(BF16)
    o_ref[...] = jnp.dot(s, w_ref[...].astype(BF16), preferred_element_type=F32) + b_ref[...]


def _adaln(cvecs, w_mod, b_mod):
    d, n = w_mod.shape
    return pl.pallas_call(
        _adaln_kernel,
        out_shape=jax.ShapeDtypeStruct((8, n), F32),
        grid=(n // MOD_TN,),
        in_specs=[pl.BlockSpec((8, d), lambda j: (0, 0)),
                  pl.BlockSpec((d, MOD_TN), lambda j: (0, j)),
                  pl.BlockSpec((1, MOD_TN), lambda j: (0, j))],
        out_specs=pl.BlockSpec((8, MOD_TN), lambda j: (0, j)),
        compiler_params=pltpu.CompilerParams(
            dimension_semantics=("arbitrary",),
            vmem_limit_bytes=_vmem_limit(2 * d * MOD_TN * 4 + d * MOD_TN * 2)),
        name="adaln",
    )(cvecs, w_mod, b_mod.reshape(1, n))


def _mod_spec(chunk, width, tile_to_mod):
    return pl.BlockSpec((None, 1, width), lambda i, j: (tile_to_mod(i) * N_MOD + chunk, 0, 0))


def _mod_spec1(chunk, width, tile_to_mod):
    return pl.BlockSpec((None, 1, width), lambda i: (tile_to_mod(i) * N_MOD + chunk, 0, 0))


def _resident(shape):
    return pl.BlockSpec(shape, lambda i: (0,) * len(shape), pipeline_mode=pl.Buffered(1))


def _l0_in_kernel(ctx_ref, x_ref, g_ref, sh_ref, sc_ref, w_ref, cos_ref, sa_ref, sb_ref, o_ref):
    i = pl.program_id(0)
    h = jnp.where(i == 0, ctx_ref[...], x_ref[...])
    u = _norm_mod(h, g_ref[...], sh_ref[...], sc_ref[...]).astype(BF16)
    cos, sa, sb = cos_ref[...], sa_ref[...], sb_ref[...]
    n_rot = N_Q_HEADS + N_KV_HEADS
    n = o_ref.shape[1]
    for c0 in range(0, n, TN_CHUNK):
        z = jnp.dot(u, w_ref[:, c0:c0 + TN_CHUNK], preferred_element_type=F32)
        if c0 >= n_rot * HEAD_DIM:
            o_ref[:, c0:c0 + TN_CHUNK] = z.astype(BF16)
            continue
        for k in range(TN_CHUNK // HEAD_DIM):
            hh = c0 // HEAD_DIM + k
            t = z[:, k * HEAD_DIM:(k + 1) * HEAD_DIM]
            if hh < n_rot:
                t = (t * cos + pltpu.roll(t, HEAD_DIM - ROPE_PAIRS, 1) * sa
                     + pltpu.roll(t, ROPE_PAIRS, 1) * sb)
            if hh < N_Q_HEADS:
                t = t * ATTN_SCALE
            o_ref[:, hh * HEAD_DIM:(hh + 1) * HEAD_DIM] = t.astype(BF16)


def _l0_in_proj(ctx2, x2, g, mods, w_in, rope, tile_to_mod, rope_blk):
    n_ctx_rows, d = ctx2.shape
    assert n_ctx_rows == TM
    r = n_ctx_rows + x2.shape[0]
    n = w_in.shape[1]
    assert n % TN_CHUNK == 0
    cos, sa, sb = rope
    rope_spec = pl.BlockSpec((TM, HEAD_DIM), lambda i: (rope_blk(i), 0))
    return pl.pallas_call(
        _l0_in_kernel,
        out_shape=jax.ShapeDtypeStruct((r, n), BF16),
        grid=(r // TM,),
        in_specs=[_resident((TM, d)),
                  pl.BlockSpec((TM, d), lambda i: (jnp.maximum(i - 1, 0), 0)),
                  pl.BlockSpec((1, d), lambda i: (0, 0)),
                  _mod_spec1(0, d, tile_to_mod),
                  _mod_spec1(1, d, tile_to_mod),
                  _resident((d, n)),
                  rope_spec, rope_spec, rope_spec],
        out_specs=pl.BlockSpec((TM, n), lambda i: (i, 0)),
        compiler_params=pltpu.CompilerParams(
            dimension_semantics=("arbitrary",),
            vmem_limit_bytes=_vmem_limit(3 * TM * d * 4 + d * n * 2 + 2 * TM * n * 2
                                         + 2 * TM * d * 4 + 4 * TM * TN_CHUNK * 4)),
        name="l0_in_proj",
    )(ctx2, x2, g.reshape(1, d), mods, mods, w_in, cos, sa, sb)


def _conv_taps(x, xp, xn, first, last, w, b, offsets):
    xp = jnp.where(first, 0.0, xp)
    xn = jnp.where(last, 0.0, xn)
    tr, cw = x.shape
    row8 = lax.broadcasted_iota(jnp.int32, (8, cw), 0)
    acc = jnp.broadcast_to(b, (tr, cw))
    for k, off in enumerate(offsets):
        wk = w[k:k + 1, :]
        if off == 0:
            y = x
        elif off < 0:
            s = -off
            r = pltpu.roll(x, s, 0)
            rp = pltpu.roll(xp, s, 0)[0:8]
            head = jnp.where(row8 < s, rp, r[0:8])
            y = jnp.concatenate([head, r[8:]], axis=0)
        else:
            r = pltpu.roll(x, tr - off, 0)
            rn = pltpu.roll(xn, HALO - off, 0)[HALO - 8:HALO]
            tail = jnp.where(row8 >= 8 - off, rn, r[tr - 8:])
            y = jnp.concatenate([r[:tr - 8], tail], axis=0)
        acc = acc + wk * y
    return acc


def _attn_kernel(sink_ref, q_ref, kc_ref, vc_ref, kp_ref, ko_ref, kn_ref, vp_ref, vo_ref, vn_ref,
                 *rest, tiles_per_seg, n_conv_tiles):
    n_half = n_conv_tiles
    xin = rest[0:n_half]
    bgate = rest[n_half:2 * n_half]
    cgate = rest[2 * n_half:3 * n_half]
    xin_p, cg_p, xin_n, cg_n = (rest[(3 + k) * n_half:(4 + k) * n_half] for k in range(4))
    cw_ref, cb_ref, o_ref, conv_ref = rest[7 * n_half:]
    t = pl.program_id(1)
    tt = t - 1

    first = (t == 0) | (tt == 0)
    last = (t == 0) | (tt == tiles_per_seg - 1)
    for hh in range(n_half):
        cs = slice(hh * CONV_CW, (hh + 1) * CONV_CW)
        x = xin[hh][...].astype(F32) * cgate[hh][...].astype(F32)
        xp = xin_p[hh][...].astype(F32) * cg_p[hh][...].astype(F32)
        xn = xin_n[hh][...].astype(F32) * cg_n[hh][...].astype(F32)
        acc = _conv_taps(x, xp, xn, first, last, cw_ref[:, cs], cb_ref[:, cs], SHORT_CONV_OFFSETS)
        conv_ref[:, cs] = (acc * bgate[hh][...].astype(F32)).astype(conv_ref.dtype)

    tr = q_ref.shape[0]
    n_ctx = kc_ref.shape[0]
    halo = kp_ref.shape[0]
    n_win = tr + 2 * halo
    qi = lax.broadcasted_iota(jnp.int32, (tr, n_win), 0)
    c = lax.broadcasted_iota(jnp.int32, (tr, n_win), 1)
    ok = (c >= qi) & (c <= qi + 2 * WINDOW)
    ok = ok & ((c >= halo) | (tt > 0)) & ((c < halo + tr) | (tt < tiles_per_seg - 1)) & (t > 0)
    bias = jnp.concatenate([jnp.zeros((tr, n_ctx), F32), jnp.where(ok, 0.0, NEG_INF)], axis=1)
    for hk in range(N_KV_HEADS):
        cs = slice(hk * HEAD_DIM, (hk + 1) * HEAD_DIM)
        k_all = jnp.concatenate([kc_ref[:, cs], kp_ref[:, cs], ko_ref[:, cs], kn_ref[:, cs]], axis=0)
        v_all = jnp.concatenate([vc_ref[:, cs], vp_ref[:, cs], vo_ref[:, cs], vn_ref[:, cs]], axis=0)
        for g in range(Q_PER_KV):
            hq = hk * Q_PER_KV + g
            qs = slice(hq * HEAD_DIM, (hq + 1) * HEAD_DIM)
            s = lax.dot_general(q_ref[:, qs], k_all, (((1,), (1,)), ((), ())),
                                preferred_element_type=F32) + bias
            sink = sink_ref[hq]
            m = jnp.maximum(jnp.max(s, axis=-1, keepdims=True), sink)
            p = jnp.exp(s - m)
            denom = jnp.sum(p, axis=-1, keepdims=True) + jnp.exp(sink - m)
            o = jnp.dot(p.astype(BF16), v_all, preferred_element_type=F32)
            o_ref[:, qs] = (o / denom).astype(o_ref.dtype)


def _attention(z, sinks, conv_w, conv_b, n_batch, n_ctx_tiles, tiles_per_seg):
    r = z.shape[0]
    q_dim = N_Q_HEADS * HEAD_DIM
    kv_dim = N_KV_HEADS * HEAD_DIM
    conv_dim = conv_w.shape[1]
    k_col = q_dim // kv_dim
    v_col = k_col + 1
    halo = WINDOW
    per = TR // halo
    n_halo_blk = r // halo
    n_cv = conv_dim // CONV_CW
    x_col = (q_dim + 2 * kv_dim) // CONV_CW
    bg_col, cg_col = x_col + n_cv, x_col + 2 * n_cv
    nh = TR // HALO
    n_conv_halo = r // HALO

    def qblk(b, t):
        return jnp.where(t == 0, b, n_ctx_tiles + b * tiles_per_seg + t - 1)

    def own(col):
        return pl.BlockSpec((TR, kv_dim), lambda b, t: (qblk(b, t), col))

    def ctx(col):
        return pl.BlockSpec((TR, kv_dim), lambda b, t: (b, col))

    def prev(col):
        return pl.BlockSpec((halo, kv_dim), lambda b, t: (jnp.maximum(qblk(b, t) * per - 1, 0), col))

    def nxt(col):
        return pl.BlockSpec((halo, kv_dim),
                            lambda b, t: (jnp.minimum((qblk(b, t) + 1) * per, n_halo_blk - 1), col))

    def cv_main(col0):
        return [pl.BlockSpec((TR, CONV_CW), lambda b, t, c=col0 + k: (qblk(b, t), c)) for k in range(n_cv)]

    def cv_prev(col0):
        return [pl.BlockSpec((HALO, CONV_CW), lambda b, t, c=col0 + k: (jnp.maximum(qblk(b, t) * nh - 1, 0), c))
                for k in range(n_cv)]

    def cv_next(col0):
        return [pl.BlockSpec((HALO, CONV_CW),
                             lambda b, t, c=col0 + k: (jnp.minimum((qblk(b, t) + 1) * nh, n_conv_halo - 1), c))
                for k in range(n_cv)]

    conv_specs = (cv_main(x_col) + cv_main(bg_col) + cv_main(cg_col)
                  + cv_prev(x_col) + cv_prev(cg_col) + cv_next(x_col) + cv_next(cg_col))
    return pl.pallas_call(
        functools.partial(_attn_kernel, tiles_per_seg=tiles_per_seg, n_conv_tiles=n_cv),
        out_shape=(jax.ShapeDtypeStruct((r, q_dim), BF16), jax.ShapeDtypeStruct((r, conv_dim), BF16)),
        grid=(n_batch, 1 + tiles_per_seg),
        in_specs=[pl.BlockSpec(memory_space=pltpu.SMEM),
                  pl.BlockSpec((TR, q_dim), lambda b, t: (qblk(b, t), 0)),
                  ctx(k_col), ctx(v_col),
                  prev(k_col), own(k_col), nxt(k_col),
                  prev(v_col), own(v_col), nxt(v_col)] + conv_specs + [
                      pl.BlockSpec((len(SHORT_CONV_OFFSETS), conv_dim), lambda b, t: (0, 0)),
                      pl.BlockSpec((1, conv_dim), lambda b, t: (0, 0))],
        out_specs=(pl.BlockSpec((TR, q_dim), lambda b, t: (qblk(b, t), 0)),
                   pl.BlockSpec((TR, conv_dim), lambda b, t: (qblk(b, t), 0))),
        compiler_params=pltpu.CompilerParams(dimension_semantics=("arbitrary", "arbitrary")),
        name="attention",
    )(sinks, *([z] * (9 + 7 * n_cv)), conv_w, conv_b.reshape(1, conv_dim))


def _l0_out_kernel(ctx_ref, x_ref, a1_ref, a2_ref, w_ref, gate_ref, oc_ref, ol_ref):
    i = pl.program_id(0)
    lhs = jnp.concatenate([a1_ref[...], a2_ref[...]], axis=1)
    d = ol_ref.shape[1]
    for c0 in range(0, d, TN_CHUNK):
        cs = slice(c0, c0 + TN_CHUNK)
        y = jnp.dot(lhs, w_ref[:, cs], preferred_element_type=F32)
        h = jnp.where(i == 0, ctx_ref[:, cs], x_ref[:, cs])
        res = h + gate_ref[:, cs] * y
        oc_ref[:, cs] = res
        ol_ref[:, cs] = res


def _l0_out_proj(attn, conv, w_out, ctx2, x2, mods, tile_to_mod):
    n_ctx_rows, d = ctx2.shape
    n_lat = x2.shape[0]
    r = n_ctx_rows + n_lat
    k1 = attn.shape[1]
    k2 = conv.shape[1]
    assert w_out.shape[0] == k1 + k2 and d % TN_CHUNK == 0
    return pl.pallas_call(
        _l0_out_kernel,
        out_shape=(jax.ShapeDtypeStruct((2, TM, d), F32), jax.ShapeDtypeStruct((n_lat, d), F32)),
        grid=(r // TM,),
        in_specs=[_resident((TM, d)),
                  pl.BlockSpec((TM, d), lambda i: (jnp.maximum(i - 1, 0), 0)),
                  pl.BlockSpec((TM, k1), lambda i: (i, 0)),
                  pl.BlockSpec((TM, k2), lambda i: (i, 0)),
                  _resident((k1 + k2, d)),
                  _mod_spec1(2, d, tile_to_mod)],
        out_specs=(pl.BlockSpec((None, TM, d), lambda i: (jnp.minimum(i, 1), 0, 0)),
                   pl.BlockSpec((TM, d), lambda i: (jnp.maximum(i - 1, 0), 0))),
        compiler_params=pltpu.CompilerParams(
            dimension_semantics=("arbitrary",),
            vmem_limit_bytes=_vmem_limit(3 * TM * d * 4 + 3 * TM * (k1 + k2) * 2 + (k1 + k2) * d * 2
                                         + 4 * TM * d * 4 + 4 * TM * TN_CHUNK * 4)),
        name="l0_out_proj",
    )(ctx2, x2, attn, conv, w_out, mods)


def _ffn_kernel(*refs, cast_blocks):
    n_cast = len(cast_blocks)
    h_ref, g_ref, sh_ref, sc_ref, gate_ref, wg_ref, wu_ref, wd_ref = refs[:8]
    src = refs[8:8 + n_cast]
    o_ref = refs[8 + n_cast]
    dst = refs[9 + n_cast:9 + 2 * n_cast]
    u_scr = refs[9 + 2 * n_cast]
    i = pl.program_id(0)
    j = pl.program_id(1)
    nj = pl.num_programs(1)
    cb = CAST_BLOCK

    if n_cast:
        cin, cout, sem_in, sem_out = refs[10 + 2 * n_cast:]
        n_blocks = sum(ne * nbr * nbc for ne, nbr, nbc in cast_blocks)
        s = i * nj + j
        slot = s % 2

        def for_block(b, fn):
            off = 0
            for k, (ne, nbr, nbc) in enumerate(cast_blocks):
                nk = ne * nbr * nbc

                @pl.when((b >= off) & (b < off + nk))
                def _(k=k, off=off, nbr=nbr, nbc=nbc):
                    q = b - off
                    rc = q % (nbr * nbc)
                    fn(k, q // (nbr * nbc), rc // nbc, rc % nbc)
                off += nk

        def block_of(ref, e, r, c):
            return ref.at[e, pl.ds(r * cb, cb), pl.ds(c * cb, cb)]

        def start_in(b, to_slot):
            for_block(b, lambda k, e, r, c: pltpu.make_async_copy(
                block_of(src[k], e, r, c), cin.at[to_slot], sem_in.at[to_slot]).start(priority=1))

        @pl.when(s == 0)
        def _():
            start_in(0, 0)

        @pl.when(s < n_blocks)
        def _():
            pltpu.make_async_copy(block_of(src[0], 0, 0, 0), cin.at[slot], sem_in.at[slot]).wait()

        @pl.when(s + 1 < n_blocks)
        def _():
            start_in(s + 1, 1 - slot)

        @pl.when((s >= 2) & (s - 2 < n_blocks))
        def _():
            pltpu.make_async_copy(cout.at[slot], block_of(dst[0], 0, 0, 0), sem_out.at[slot]).wait()

    @pl.when(j == 0)
    def _():
        u_scr[...] = _norm_mod(h_ref[...], g_ref[...], sh_ref[...], sc_ref[...]).astype(BF16)

    if n_cast:
        cout[slot] = cin[slot].astype(BF16)
    u = u_scr[...]
    gt = jnp.dot(u, wg_ref[...], preferred_element_type=F32)
    up = jnp.dot(u, wu_ref[...], preferred_element_type=F32)
    act = (gt * jax.nn.sigmoid(gt) * up).astype(BF16)
    y = jnp.dot(act, wd_ref[...], preferred_element_type=F32)

    @pl.when(j == 0)
    def _():
        o_ref[...] = y

    @pl.when(j != 0)
    def _():
        o_ref[...] += y

    @pl.when(j == nj - 1)
    def _():
        o_ref[...] = h_ref[...] + gate_ref[...] * o_ref[...]

    if n_cast:
        @pl.when(s < n_blocks)
        def _():
            for_block(s, lambda k, e, r, c: pltpu.make_async_copy(
                cout.at[slot], block_of(dst[k], e, r, c), sem_out.at[slot]).start(priority=1))


def _ffn(h, h_spec, rows, tm, g, mods, w_gate, w_up, w_down, tile_to_mod, riders=()):
    d = h.shape[-1]
    f = w_gate.shape[1]
    tf = TF_FFN
    grid = (rows // tm, f // tf)
    cast_blocks = tuple((w.shape[0], w.shape[1] // CAST_BLOCK, w.shape[2] // CAST_BLOCK) for w in riders)
    for w in riders:
        assert w.shape[1] % CAST_BLOCK == 0 and w.shape[2] % CAST_BLOCK == 0
    n_blocks = sum(ne * nbr * nbc for ne, nbr, nbc in cast_blocks)
    assert not riders or grid[0] * grid[1] >= n_blocks + 2
    any_spec = pl.BlockSpec(memory_space=pl.ANY)
    scratch = [pltpu.VMEM((tm, d), BF16)]
    if riders:
        scratch += [pltpu.VMEM((2, CAST_BLOCK, CAST_BLOCK), F32), pltpu.VMEM((2, CAST_BLOCK, CAST_BLOCK), BF16),
                    pltpu.SemaphoreType.DMA((2,)), pltpu.SemaphoreType.DMA((2,))]
    return pl.pallas_call(
        functools.partial(_ffn_kernel, cast_blocks=cast_blocks),
        out_shape=(jax.ShapeDtypeStruct((rows, d), F32),
                   *[jax.ShapeDtypeStruct(w.shape, BF16) for w in riders]),
        grid=grid,
        in_specs=[h_spec,
                  pl.BlockSpec((1, d), lambda i, j: (0, 0)),
                  _mod_spec(3, d, tile_to_mod),
                  _mod_spec(4, d, tile_to_mod),
                  _mod_spec(5, d, tile_to_mod),
                  pl.BlockSpec((d, tf), lambda i, j: (0, j)),
                  pl.BlockSpec((d, tf), lambda i, j: (0, j)),
                  pl.BlockSpec((tf, d), lambda i, j: (j, 0))] + [any_spec] * len(riders),
        out_specs=(pl.BlockSpec((tm, d), lambda i, j: (i, 0)), *[any_spec] * len(riders)),
        scratch_shapes=scratch,
        compiler_params=pltpu.CompilerParams(
            dimension_semantics=("arbitrary", "arbitrary"),
            vmem_limit_bytes=_vmem_limit(4 * tm * d * 4 + 6 * d * tf * 2 + tm * d * 2
                                         + 3 * tm * tf * 4 + tm * tf * 2
                                         + (12 * CAST_BLOCK * CAST_BLOCK if riders else 0))),
        name="ffn",
    )(h, g.reshape(1, d), mods, mods, mods, w_gate, w_up, w_down, *riders)


def _l1_in_kernel(hc_ref, hl_ref, g_ref, sh_ref, sc_ref, w_ref, gg_ref, xr_ref):
    h = jnp.where(pl.program_id(0) == 0, hc_ref[...], hl_ref[...])
    u = _norm_mod(h, g_ref[...], sh_ref[...], sc_ref[...]).astype(BF16)
    d_rnn = gg_ref.shape[1]
    for c0 in range(0, 2 * d_rnn, TN_CHUNK):
        z = jnp.dot(u, w_ref[:, c0:c0 + TN_CHUNK], preferred_element_type=F32)
        if c0 < d_rnn:
            gg_ref[:, c0:c0 + TN_CHUNK] = jax.nn.gelu(z, approximate=True).astype(gg_ref.dtype)
        else:
            xr_ref[:, c0 - d_rnn:c0 - d_rnn + TN_CHUNK] = z


def _l1_in_proj(h_ctx, h_lat, g, mods, w_in, d_rnn, tile_to_mod):
    n_ctx_rows, d = h_ctx.shape
    assert n_ctx_rows == TM
    r = n_ctx_rows + h_lat.shape[0]
    assert w_in.shape[1] == 2 * d_rnn and d_rnn % TN_CHUNK == 0
    return pl.pallas_call(
        _l1_in_kernel,
        out_shape=(jax.ShapeDtypeStruct((r, d_rnn), BF16), jax.ShapeDtypeStruct((r, d_rnn), F32)),
        grid=(r // TM,),
        in_specs=[_resident((TM, d)),
                  pl.BlockSpec((TM, d), lambda i: (jnp.maximum(i - 1, 0), 0)),
                  pl.BlockSpec((1, d), lambda i: (0, 0)),
                  _mod_spec1(0, d, tile_to_mod),
                  _mod_spec1(1, d, tile_to_mod),
                  _resident((d, 2 * d_rnn))],
        out_specs=(pl.BlockSpec((TM, d_rnn), lambda i: (i, 0)),
                   pl.BlockSpec((TM, d_rnn), lambda i: (i, 0))),
        compiler_params=pltpu.CompilerParams(
            dimension_semantics=("arbitrary",),
            vmem_limit_bytes=_vmem_limit(2 * TM * d * 4 + d * 2 * d_rnn * 2 + 2 * TM * d_rnn * 6
                                         + 2 * TM * d * 4 + 4 * TM * TN_CHUNK * 4)),
        name="l1_in_proj",
    )(h_ctx, h_lat, g.reshape(1, d), mods, mods, w_in)


def _rglru_kernel(xf_ref, xfp_ref, xfn_ref, xb_ref, xbp_ref, xbn_ref, cw_ref, cb_ref,
                  gaw_ref, gxw_ref, gab_ref, gxb_ref, lam_ref, hf_ref, hb_ref,
                  xc_scr, a_scr, b_scr, h_scr, *, tiles_per_seg):
    c = pl.program_id(1)
    t_len, d_rnn = xf_ref.shape
    hd = d_rnn // N_RNN_HEADS

    @pl.when(c == 0)
    def _():
        h_scr[...] = jnp.zeros_like(h_scr)

    is_ctx = c == 0
    chunk = (c - 1, tiles_per_seg - c)
    halos = ((xfp_ref, xfn_ref), (xbp_ref, xbn_ref))

    for z, x_ref in ((0, xf_ref), (1, xb_ref)):
        first = is_ctx | (chunk[z] == 0)
        last = is_ctx | (chunk[z] == tiles_per_seg - 1)
        xc_scr[...] = _conv_taps(x_ref[...], halos[z][0][...], halos[z][1][...], first, last,
                                 cw_ref[...], cb_ref[...], RG_CONV_OFFSETS)
        x_ref = xc_scr
        neg_lam = -lam_ref[z]
        sp = jnp.maximum(neg_lam, 0.0) + jnp.log1p(jnp.exp(-jnp.abs(neg_lam)))
        half_k = sp * (-0.5 * RG_C * LOG2_E)
        for hh in range(N_RNN_HEADS):
            sl = slice(hh * hd, (hh + 1) * hd)
            xh = x_ref[:, sl]
            xh16 = xh.astype(BF16)
            ta = jnp.tanh(jnp.dot(xh16, gaw_ref[z, hh], preferred_element_type=F32) + gab_ref[z][:, sl])
            ti = jnp.tanh(jnp.dot(xh16, gxw_ref[z, hh], preferred_element_type=F32) + gxb_ref[z][:, sl])
            hk = half_k[:, sl]
            a = jnp.exp2(ta * hk + hk)
            om = 1.0 - a * a
            root = jnp.where(om > 0.0, om * lax.rsqrt(om), 0.0)
            rx = (0.5 * root) * xh
            a_scr[z, :, sl] = a
            b_scr[z, :, sl] = rx * ti + rx

    row8 = lax.broadcasted_iota(jnp.int32, (8, d_rnn), 0)
    n_grp = t_len // 8

    def fwd_group(r0, h):
        a = a_scr[0, pl.ds(r0, 8), :]
        b = b_scr[0, pl.ds(r0, 8), :]
        for s in (1, 2, 4):
            a_sh = jnp.where(row8 >= s, pltpu.roll(a, s, 0), 1.0)
            b_sh = jnp.where(row8 >= s, pltpu.roll(b, s, 0), 0.0)
            b = a * b_sh + b
            a = a * a_sh
        out = a * h + b
        return out, jnp.broadcast_to(out[7:8, :], (8, d_rnn))

    def bwd_group(r0, h):
        a = a_scr[1, pl.ds(r0, 8), :]
        b = b_scr[1, pl.ds(r0, 8), :]
        for s in (1, 2, 4):
            a_sh = jnp.where(row8 < 8 - s, pltpu.roll(a, 8 - s, 0), 1.0)
            b_sh = jnp.where(row8 < 8 - s, pltpu.roll(b, 8 - s, 0), 0.0)
            b = a * b_sh + b
            a = a * a_sh
        out = a * h + b
        return out, jnp.broadcast_to(out[0:1, :], (8, d_rnn))

    def fwd_body(g, h):
        r0 = pl.multiple_of(g * 16, 16)
        lo, h = fwd_group(r0, h)
        hi, h = fwd_group(r0 + 8, h)
        hf_ref[pl.ds(r0, 16), :] = jnp.concatenate([lo, hi], axis=0).astype(hf_ref.dtype)
        return h

    def bwd_body(k, h):
        r0 = pl.multiple_of((n_grp // 2 - 1 - k) * 16, 16)
        hi, h = bwd_group(r0 + 8, h)
        lo, h = bwd_group(r0, h)
        hb_ref[pl.ds(r0, 16), :] = jnp.concatenate([lo, hi], axis=0).astype(hb_ref.dtype)
        return h

    h_scr[0] = lax.fori_loop(0, n_grp // 2, fwd_body, h_scr[0])
    h_scr[1] = lax.fori_loop(0, n_grp // 2, bwd_body, h_scr[1])


def _rglru(xr, conv_w, conv_b, ga_w, gx_w, ga_b, gx_b, lam, n_batch, n_ctx_tiles, tiles_per_seg):
    r, d_rnn = xr.shape
    n_lat = n_batch * tiles_per_seg * TR
    hd = d_rnn // N_RNN_HEADS
    nt = tiles_per_seg
    nh = TR // HALO
    n_halo = r // HALO
    n_taps = conv_w.shape[0]

    def fblk(b, c):
        return jnp.where(c == 0, b, n_ctx_tiles + b * nt + c - 1)

    def bblk(b, c):
        return jnp.where(c == 0, b, n_ctx_tiles + b * nt + nt - c)

    def chunk_specs(blk):
        return [pl.BlockSpec((TR, d_rnn), lambda b, c: (blk(b, c), 0)),
                pl.BlockSpec((HALO, d_rnn), lambda b, c: (jnp.maximum(blk(b, c) * nh - 1, 0), 0)),
                pl.BlockSpec((HALO, d_rnn), lambda b, c: (jnp.minimum((blk(b, c) + 1) * nh, n_halo - 1), 0))]

    def full(shape):
        return pl.BlockSpec(shape, lambda b, c: (0,) * len(shape))

    return pl.pallas_call(
        functools.partial(_rglru_kernel, tiles_per_seg=nt),
        out_shape=(jax.ShapeDtypeStruct((n_lat, d_rnn), BF16), jax.ShapeDtypeStruct((n_lat, d_rnn), BF16)),
        grid=(n_batch, 1 + nt),
        in_specs=chunk_specs(fblk) + chunk_specs(bblk) + [
            full((n_taps, d_rnn)), full((1, d_rnn)),
            full((2, N_RNN_HEADS, hd, hd)), full((2, N_RNN_HEADS, hd, hd)),
            full((2, 1, d_rnn)), full((2, 1, d_rnn)), full((2, 1, d_rnn))],
        out_specs=(pl.BlockSpec((TR, d_rnn), lambda b, c: (b * nt + jnp.maximum(c - 1, 0), 0)),
                   pl.BlockSpec((TR, d_rnn), lambda b, c: (b * nt + jnp.where(c == 0, nt - 1, nt - c), 0))),
        scratch_shapes=[pltpu.VMEM((TR, d_rnn), F32),
                        pltpu.VMEM((2, TR, d_rnn), F32), pltpu.VMEM((2, TR, d_rnn), F32),
                        pltpu.VMEM((2, 8, d_rnn), F32)],
        compiler_params=pltpu.CompilerParams(
            dimension_semantics=("arbitrary", "arbitrary"),
            vmem_limit_bytes=_vmem_limit(14 * TR * d_rnn * 4 + 8 * N_RNN_HEADS * hd * hd * 2)),
        name="rglru",
    )(xr, xr, xr, xr, xr, xr, conv_w, conv_b.reshape(1, d_rnn), ga_w, gx_w,
      ga_b.reshape(2, 1, d_rnn), gx_b.reshape(2, 1, d_rnn), lam.reshape(2, 1, d_rnn))


def _l1_out_kernel(hf_ref, hb_ref, gg_ref, w_ref, h_ref, gate_ref, o_ref):
    rec = hf_ref[...].astype(F32) + hb_ref[...].astype(F32)
    lhs = (rec * gg_ref[...].astype(F32)).astype(BF16)
    d = o_ref.shape[1]
    for c0 in range(0, d, TN_CHUNK):
        cs = slice(c0, c0 + TN_CHUNK)
        y = jnp.dot(lhs, w_ref[:, cs], preferred_element_type=F32)
        o_ref[:, cs] = h_ref[:, cs] + gate_ref[:, cs] * y


def _l1_out_proj(hf, hb, gg, w_out, h, mods, n_ctx_mtiles, lat_tile_to_mod):
    n_lat, d_rnn = hf.shape
    d = h.shape[1]
    assert d % TN_CHUNK == 0
    return pl.pallas_call(
        _l1_out_kernel,
        out_shape=jax.ShapeDtypeStruct((n_lat, d), F32),
        grid=(n_lat // TM,),
        in_specs=[pl.BlockSpec((TM, d_rnn), lambda i: (i, 0)),
                  pl.BlockSpec((TM, d_rnn), lambda i: (i, 0)),
                  pl.BlockSpec((TM, d_rnn), lambda i: (i + n_ctx_mtiles, 0)),
                  _resident((d_rnn, d)),
                  pl.BlockSpec((TM, d), lambda i: (i, 0)),
                  _mod_spec1(2, d, lat_tile_to_mod)],
        out_specs=pl.BlockSpec((TM, d), lambda i: (i, 0)),
        compiler_params=pltpu.CompilerParams(
            dimension_semantics=("arbitrary",),
            vmem_limit_bytes=_vmem_limit(6 * TM * d_rnn * 2 + d_rnn * d * 2 + 4 * TM * d * 4
                                         + 2 * TM * d_rnn * 4 + 4 * TM * TN_CHUNK * 4)),
        name="l1_out_proj",
    )(hf, hb, gg, w_out, h, mods)


def _router_kernel(h_ref, g_ref, sh_ref, sc_ref, rw_ref, rb_ref, u_ref, route_ref):
    u = _norm_mod(h_ref[...], g_ref[...], sh_ref[...], sc_ref[...])
    u_ref[...] = u
    logits = jnp.dot(u.astype(BF16), rw_ref[...], preferred_element_type=F32) + rb_ref[...]
    lane = lax.broadcasted_iota(jnp.int32, logits.shape, 1)
    m1 = jnp.max(logits, axis=-1, keepdims=True)
    i1 = jnp.min(jnp.where(logits == m1, lane, LANES), axis=-1, keepdims=True)
    rest = jnp.where(lane == i1, -jnp.inf, logits)
    m2 = jnp.max(rest, axis=-1, keepdims=True)
    i2 = jnp.min(jnp.where(rest == m2, lane, LANES), axis=-1, keepdims=True)
    e2 = jnp.exp(m2 - m1)
    w1 = 1.0 / (1.0 + e2)
    w2 = e2 * w1
    route = jnp.where(lane == 0, i1.astype(F32),
                      jnp.where(lane == 1, i2.astype(F32),
                                jnp.where(lane == 2, w1, jnp.where(lane == 3, w2, 0.0))))
    route_ref[...] = route


def _router(h, g, mods, rw_pad, rb_pad, lat_tile_to_mod):
    n, d = h.shape

    def mspec(chunk):
        return pl.BlockSpec((None, 1, d), lambda i: (lat_tile_to_mod(i) * N_MOD + chunk, 0, 0))

    return pl.pallas_call(
        _router_kernel,
        out_shape=(jax.ShapeDtypeStruct((n, d), F32), jax.ShapeDtypeStruct((n, LANES), F32)),
        grid=(n // TM,),
        in_specs=[pl.BlockSpec((TM, d), lambda i: (i, 0)),
                  pl.BlockSpec((1, d), lambda i: (0, 0)),
                  mspec(3), mspec(4),
                  pl.BlockSpec((d, LANES), lambda i: (0, 0)),
                  pl.BlockSpec((1, LANES), lambda i: (0, 0))],
        out_specs=(pl.BlockSpec((TM, d), lambda i: (i, 0)),
                   pl.BlockSpec((TM, LANES), lambda i: (i, 0))),
        compiler_params=pltpu.CompilerParams(
            dimension_semantics=("arbitrary",),
            vmem_limit_bytes=_vmem_limit(6 * TM * d * 4)),
        name="router",
    )(h, g.reshape(1, d), mods, mods, rw_pad, rb_pad)


def _row_copy(src, src_row, dst, dst_row, sem):
    return pltpu.make_async_copy(src.at[pl.ds(src_row, 1)], dst.at[pl.ds(dst_row, 1)], sem)


def _moe_kernel(te_ref, tv_ref, tc_ref, tok_ref, tok_next_ref, u_hbm, wg_ref, wu_ref, wd_ref, o_ref,
                xg_scr, x_scr, sems, *, gather_steps):
    del te_ref
    i = pl.program_id(0)
    j = pl.program_id(1)
    n_tiles = pl.num_programs(0)
    rows = x_scr.shape[0]
    per_step = rows // gather_steps
    slot = i % 2
    valid = tv_ref[i] == 1
    next_valid = (i + 1 < n_tiles) & (tv_ref[jnp.minimum(i + 1, n_tiles - 1)] == 1)

    @pl.when(valid & (j == 0))
    def _():
        @pl.when(i == 0)
        def _():
            def start(r, carry):
                _row_copy(u_hbm, tok_ref[r], xg_scr.at[0], r, sems.at[0]).start()
                return carry
            lax.fori_loop(0, rows, start, 0)

        pltpu.make_async_copy(u_hbm.at[pl.ds(0, rows)], xg_scr.at[slot], sems.at[slot]).wait()
        x_scr[...] = xg_scr[slot].astype(BF16)

    def step(with_gather, m):
        if with_gather:
            r0 = j * per_step
            for r in range(per_step):
                _row_copy(u_hbm, tok_next_ref[r0 + r], xg_scr.at[1 - slot], r0 + r, sems.at[1 - slot]).start()
        x = x_scr[0:m, :]
        gt = jnp.dot(x, wg_ref[...], preferred_element_type=F32)
        up = jnp.dot(x, wu_ref[...], preferred_element_type=F32)
        act = (gt * jax.nn.sigmoid(gt) * up).astype(BF16)
        y = jnp.dot(act, wd_ref[...], preferred_element_type=F32)

        @pl.when(j == 0)
        def _():
            o_ref[0:m, :] = y
            if m < rows:
                o_ref[m:, :] = jnp.zeros((rows - m, o_ref.shape[1]), F32)

        @pl.when(j != 0)
        def _():
            o_ref[0:m, :] += y

    gather_now = next_valid & (j < gather_steps)
    half = tc_ref[i] <= rows // 2
    for with_gather in (True, False):
        for is_half in (True, False):
            cond = valid & (gather_now if with_gather else jnp.logical_not(gather_now))
            cond = cond & (half if is_half else jnp.logical_not(half))

            @pl.when(cond)
            def _(with_gather=with_gather, is_half=is_half):
                step(with_gather, rows // 2 if is_half else rows)

    @pl.when((tv_ref[i] == 0) & (j == 0))
    def _():
        o_ref[...] = jnp.zeros_like(o_ref)


def _moe(u, slot_tok, tile_e, tile_valid, tile_count, w_gate, w_up, w_down):
    d = u.shape[1]
    cap = slot_tok.shape[0]
    n_tiles = cap // TM
    f = w_gate.shape[2]
    tf = TF_MOE
    nj = f // tf
    gather_steps = min(MOE_GATHER_STEPS, nj)
    assert TM % gather_steps == 0

    def jj(i, j, tv):
        return jnp.where(tv[i] == 1, j, nj - 1)

    return pl.pallas_call(
        functools.partial(_moe_kernel, gather_steps=gather_steps),
        out_shape=jax.ShapeDtypeStruct((cap, d), F32),
        grid_spec=pltpu.PrefetchScalarGridSpec(
            num_scalar_prefetch=3,
            grid=(n_tiles, nj),
            in_specs=[pl.BlockSpec((TM,), lambda i, j, te, tv, tc: (i,), memory_space=pltpu.SMEM),
                      pl.BlockSpec((TM,), lambda i, j, te, tv, tc: (jnp.minimum(i + 1, n_tiles - 1),),
                                   memory_space=pltpu.SMEM),
                      pl.BlockSpec(memory_space=pl.ANY),
                      pl.BlockSpec((None, d, tf), lambda i, j, te, tv, tc: (te[i], 0, jj(i, j, tv))),
                      pl.BlockSpec((None, d, tf), lambda i, j, te, tv, tc: (te[i], 0, jj(i, j, tv))),
                      pl.BlockSpec((None, tf, d), lambda i, j, te, tv, tc: (te[i], jj(i, j, tv), 0))],
            out_specs=pl.BlockSpec((TM, d), lambda i, j, te, tv, tc: (i, 0)),
            scratch_shapes=[pltpu.VMEM((2, TM, d), F32), pltpu.VMEM((TM, d), BF16),
                            pltpu.SemaphoreType.DMA((2,))]),
        compiler_params=pltpu.CompilerParams(
            dimension_semantics=("arbitrary", "arbitrary"),
            vmem_limit_bytes=_vmem_limit(2 * TM * d * 4 + 2 * TM * d * 4 + 6 * d * tf * 2 + TM * d * 2
                                         + 3 * TM * tf * 4 + TM * d * 4)),
        name="moe_experts",
    )(tile_e, tile_valid, tile_count, slot_tok, slot_tok, u, w_gate, w_up, w_down)


def _combine_kernel(dest_ref, route_ref, h_ref, gate_ref, fg_ref, yb_hbm, o_ref, y_scr, sem, *, rows):
    def start(r, carry):
        for kk in range(TOP_K):
            _row_copy(yb_hbm, dest_ref[TOP_K * r + kk], y_scr.at[kk], r, sem).start()
        return carry

    lax.fori_loop(0, rows, start, 0)
    for kk in range(TOP_K):
        pltpu.make_async_copy(yb_hbm.at[pl.ds(0, rows)], y_scr.at[kk], sem).wait()

    route = route_ref[...]
    moe = route[:, 2:3] * y_scr[0] + route[:, 3:4] * y_scr[1]
    hl = h_ref[...] + gate_ref[...] * moe
    ms = jnp.mean(hl * hl, axis=-1, keepdims=True)
    o_ref[...] = (hl * lax.rsqrt(ms + NORM_EPS)) * fg_ref[...]


def _combine(yb, dest, route, h, mods, final_g, lat_tile_to_mod):
    n, d = h.shape
    return pl.pallas_call(
        functools.partial(_combine_kernel, rows=TM),
        out_shape=jax.ShapeDtypeStruct((n, d), F32),
        grid=(n // TM,),
        in_specs=[pl.BlockSpec((TOP_K * TM,), lambda i: (i,), memory_space=pltpu.SMEM),
                  pl.BlockSpec((TM, LANES), lambda i: (i, 0)),
                  pl.BlockSpec((TM, d), lambda i: (i, 0)),
                  pl.BlockSpec((None, 1, d), lambda i: (lat_tile_to_mod(i) * N_MOD + 5, 0, 0)),
                  pl.BlockSpec((1, d), lambda i: (0, 0)),
                  pl.BlockSpec(memory_space=pl.ANY)],
        out_specs=pl.BlockSpec((TM, d), lambda i: (i, 0)),
        scratch_shapes=[pltpu.VMEM((TOP_K, TM, d), F32), pltpu.SemaphoreType.DMA],
        compiler_params=pltpu.CompilerParams(
            dimension_semantics=("arbitrary",),
            vmem_limit_bytes=_vmem_limit(TOP_K * TM * d * 4 + 4 * TM * d * 4 + 2 * TM * d * 4)),
        name="moe_combine",
    )(dest, route, h, mods, final_g.reshape(1, d), yb)


def _rope_tables(n_ident, s_len):
    t = jnp.arange(s_len)
    row_id = (t // GRID_W).astype(F32)
    col_id = (t % GRID_W).astype(F32)
    inv_freq = ROPE_BASE ** (-jnp.arange(ROPE_PAIRS, dtype=F32) / ROPE_PAIRS)
    ang_r = row_id[:, None] * inv_freq
    ang_c = col_id[:, None] * inv_freq
    ang = jnp.concatenate([ang_r, ang_r, ang_c, ang_c], axis=-1)
    cos, sin = jnp.cos(ang), jnp.sin(ang)
    first_half = (jnp.arange(HEAD_DIM) % (2 * ROPE_PAIRS)) < ROPE_PAIRS
    sa = jnp.where(first_half, -sin, 0.0)
    sb = jnp.where(first_half, 0.0, sin)
    ones = jnp.ones((n_ident, HEAD_DIM), F32)
    zeros = jnp.zeros((n_ident, HEAD_DIM), F32)
    return (jnp.concatenate([ones, cos]), jnp.concatenate([zeros, sa]), jnp.concatenate([zeros, sb]))


def _routing_plan(route, n_tiles):
    e = route[:, :TOP_K].astype(jnp.int32).reshape(-1)
    onehot = (e[:, None] == jnp.arange(N_EXPERTS, dtype=jnp.int32)[None, :]).astype(jnp.int32)
    csum = jnp.cumsum(onehot, axis=0)
    rank = jnp.sum(csum * onehot, axis=1) - 1
    counts = csum[-1]
    padded = (counts + TM - 1) // TM * TM
    pad_end = jnp.cumsum(padded)
    pad_start = pad_end - padded
    dest = jnp.sum(onehot * pad_start[None, :], axis=1) + rank
    n_valid = pad_end[-1] // TM
    tile_idx = jnp.arange(n_tiles, dtype=jnp.int32)
    tile_valid = (tile_idx < n_valid).astype(jnp.int32)
    tile_row = jnp.minimum(tile_idx, n_valid - 1)
    tile_e = jnp.sum((pad_end[None, :] <= (tile_row * TM)[:, None]).astype(jnp.int32), axis=1)
    tile_e = jnp.minimum(tile_e, N_EXPERTS - 1)
    real_end = jnp.sum((tile_e[:, None] == jnp.arange(N_EXPERTS)[None, :]) * (pad_start + counts)[None, :], axis=1)
    tile_count = jnp.clip(real_end - tile_idx * TM, 0, TM) * tile_valid
    dest = dest.astype(jnp.int32)
    slot_tok = jnp.zeros((n_tiles * TM,), jnp.int32).at[dest].set(
        jnp.arange(e.shape[0], dtype=jnp.int32) // TOP_K)
    return dest, slot_tok, tile_e.astype(jnp.int32), tile_valid, tile_count.astype(jnp.int32)


def _mods(cvecs, w_mod, b_mod):
    d = w_mod.shape[0]
    m = _adaln(cvecs, w_mod, b_mod)[:3]
    return m.reshape(3 * N_MOD, 1, d)


def kernel(x, c, ctx, c_ctx, l0_w_mod, l0_b_mod, l0_norm1_g, l0_w_in, l0_sinks, l0_conv_w, l0_conv_b, l0_w_out, l0_norm2_g, l0_ffn_w_gate, l0_ffn_w_up, l0_ffn_w_down, l1_w_mod, l1_b_mod, l1_norm1_g, l1_w_in, l1_conv_w, l1_conv_b, l1_gate_a_w, l1_gate_a_b, l1_gate_x_w, l1_gate_x_b, l1_lambda, l1_w_out, l1_norm2_g, l1_router_w, l1_router_b, l1_moe_w_gate, l1_moe_w_up, l1_moe_w_down, final_norm_g):
    n_batch, s_len, d = x.shape
    n_ctx = ctx.shape[1]
    assert n_batch == 2 and n_batch * n_ctx == TM and n_ctx == TR
    assert s_len % TM == 0 and s_len % GRID_W == 0 and WINDOW * 2 == TR
    n_ctx_rows = n_batch * n_ctx
    n_lat = n_batch * s_len
    n_ctx_rtiles = n_ctx_rows // TR
    tiles_per_seg = s_len // TR
    mtiles_per_seg = s_len // TM
    d_rnn = l1_w_out.shape[0]

    def tile_to_mod(i):
        return jnp.where(i == 0, n_batch, (i - 1) // mtiles_per_seg)

    def lat_tile_to_mod(i):
        return i // mtiles_per_seg

    def rope_blk(i):
        return jnp.where(i == 0, 0, 1 + (i - 1) % mtiles_per_seg)

    cvecs = jnp.concatenate([c, c_ctx[None, :], jnp.zeros((8 - n_batch - 1, d), F32)], axis=0)
    mods0 = _mods(cvecs, l0_w_mod, l0_b_mod)
    mods1 = _mods(cvecs, l1_w_mod, l1_b_mod)
    ctx2 = ctx.reshape(n_ctx_rows, d)
    x2 = x.reshape(n_lat, d)

    rope = _rope_tables(TM, s_len)
    z = _l0_in_proj(ctx2, x2, l0_norm1_g, mods0, l0_w_in.astype(BF16), rope, tile_to_mod, rope_blk)
    attn, conv = _attention(z, l0_sinks, l0_conv_w, l0_conv_b, n_batch, n_ctx_rtiles, tiles_per_seg)
    hc2, hl = _l0_out_proj(attn, conv, l0_w_out.astype(BF16), ctx2, x2, mods0, tile_to_mod)
    ffn_w = (l0_ffn_w_gate.astype(BF16), l0_ffn_w_up.astype(BF16), l0_ffn_w_down.astype(BF16))
    (hc,) = _ffn(hc2, pl.BlockSpec((None, TM, d), lambda i, j: (0, 0, 0)), n_ctx_rows, TM,
                 l0_norm2_g, mods0, *ffn_w, lambda i: n_batch)
    hl, moe_wg, moe_wu, moe_wd = _ffn(
        hl, pl.BlockSpec((TM, d), lambda i, j: (i, 0)), n_lat, TM, l0_norm2_g, mods0, *ffn_w,
        lat_tile_to_mod, riders=(l1_moe_w_gate, l1_moe_w_up, l1_moe_w_down))

    gg, xr = _l1_in_proj(hc, hl, l1_norm1_g, mods1, l1_w_in.astype(BF16), d_rnn, tile_to_mod)
    hf, hb = _rglru(xr, l1_conv_w, l1_conv_b, (0.5 * l1_gate_a_w).astype(BF16), (0.5 * l1_gate_x_w).astype(BF16),
                    0.5 * l1_gate_a_b, 0.5 * l1_gate_x_b, l1_lambda, n_batch, n_ctx_rtiles, tiles_per_seg)
    h_lat = _l1_out_proj(hf, hb, gg, l1_w_out.astype(BF16), hl, mods1, n_ctx_rows // TM, lat_tile_to_mod)

    rw_pad = jnp.zeros((d, LANES), BF16).at[:, :N_EXPERTS].set(l1_router_w.astype(BF16))
    rb_pad = jnp.full((1, LANES), NEG_INF, F32).at[0, :N_EXPERTS].set(l1_router_b)
    u, route = _router(h_lat, l1_norm2_g, mods1, rw_pad, rb_pad, lat_tile_to_mod)
    n_tiles = (n_lat * TOP_K) // TM + N_EXPERTS
    dest, slot_tok, tile_e, tile_valid, tile_count = _routing_plan(route, n_tiles)
    yb = _moe(u, slot_tok, tile_e, tile_valid, tile_count, moe_wg, moe_wu, moe_wd)
    out = _combine(yb, dest, route, h_lat, mods1, final_norm_g, lat_tile_to_mod)
    return out.reshape(n_batch, s_len, d)
```

```python
import functools

import jax
import jax.numpy as jnp
from jax import lax
from jax.experimental import pallas as pl
from jax.experimental.pallas import tpu as pltpu

F32 = jnp.float32
BF16 = jnp.bfloat16

GRID_W = 64
HEAD_DIM = 128
N_Q_HEADS = 8
N_KV_HEADS = 2
Q_PER_KV = N_Q_HEADS // N_KV_HEADS
WINDOW = 128
ATTN_SCALE = HEAD_DIM ** -0.5
ROPE_BASE = 10000.0
ROPE_PAIRS = HEAD_DIM // 4
SHORT_CONV_OFFSETS = (-1, 0, 1)
RG_CONV_OFFSETS = (-2, -1, 0, 1)
N_RNN_HEADS = 16
RG_C = 8.0
N_EXPERTS = 8
TOP_K = 2
LOG2_E = 1.4426950408889634
NORM_EPS = 1e-6
NEG_INF = -1e30
N_MOD = 6

LANES = 128
TM = 512
CAST_BLOCK = 1024
TR = 256
HALO = 16
TN_CHUNK = 512
TF_FFN = 512
TF_MOE = 1024
MOE_GATHER_STEPS = 4
CONV_CW = 512
MOD_TN = 1024
VMEM_CAP = 56 * 1024 * 1024


def _vmem_limit(nbytes):
    return int(min(max(nbytes * 5 // 4 + (4 << 20), 32 << 20), VMEM_CAP))


def _silu(x):
    hx = 0.5 * x
    return hx * jnp.tanh(hx) + hx


def _norm_mod(h, g, shift, scale):
    ms = jnp.mean(h * h, axis=-1, keepdims=True)
    y = h * lax.rsqrt(ms + NORM_EPS)
    return (y * g) * (1.0 + scale) + shift


def _adaln_kernel(c_ref, w_ref, b_ref, o_ref):
    c = c_ref[...]
    s = (c * jax.nn.sigmoid(c)).astype(BF16)
    o_ref[...] = jnp.dot(s, w_ref[...].astype(BF16), preferred_element_type=F32) + b_ref[...]


def _adaln(cvecs, w_mod, b_mod):
    d, n = w_mod.shape
    return pl.pallas_call(
        _adaln_kernel,
        out_shape=jax.ShapeDtypeStruct((8, n), F32),
        grid=(n // MOD_TN,),
        in_specs=[pl.BlockSpec((8, d), lambda j: (0, 0)),
                  pl.BlockSpec((d, MOD_TN), lambda j: (0, j)),
                  pl.BlockSpec((1, MOD_TN), lambda j: (0, j))],
        out_specs=pl.BlockSpec((8, MOD_TN), lambda j: (0, j)),
        compiler_params=pltpu.CompilerParams(
            dimension_semantics=("arbitrary",),
            vmem_limit_bytes=_vmem_limit(2 * d * MOD_TN * 4 + d * MOD_TN * 2)),
        name="adaln",
    )(cvecs, w_mod, b_mod.reshape(1, n))


def _mod_spec(chunk, width, tile_to_mod):
    return pl.BlockSpec((None, 1, width), lambda i, j: (tile_to_mod(i) * N_MOD + chunk, 0, 0))


def _mod_spec1(chunk, width, tile_to_mod):
    return pl.BlockSpec((None, 1, width), lambda i: (tile_to_mod(i) * N_MOD + chunk, 0, 0))


def _resident(shape):
    return pl.BlockSpec(shape, lambda i: (0,) * len(shape), pipeline_mode=pl.Buffered(1))


def _l0_in_kernel(ctx_ref, x_ref, g_ref, sh_ref, sc_ref, w_ref, cos_ref, sa_ref, sb_ref, o_ref):
    i = pl.program_id(0)
    h = jnp.where(i == 0, ctx_ref[...], x_ref[...])
    u = _norm_mod(h, g_ref[...], sh_ref[...], sc_ref[...]).astype(BF16)
    cos, sa, sb = cos_ref[...], sa_ref[...], sb_ref[...]
    n_rot = N_Q_HEADS + N_KV_HEADS
    n = o_ref.shape[1]
    for c0 in range(0, n, TN_CHUNK):
        z = jnp.dot(u, w_ref[:, c0:c0 + TN_CHUNK], preferred_element_type=F32)
        if c0 >= n_rot * HEAD_DIM:
            o_ref[:, c0:c0 + TN_CHUNK] = z.astype(BF16)
            continue
        for k in range(TN_CHUNK // HEAD_DIM):
            hh = c0 // HEAD_DIM + k
            t = z[:, k * HEAD_DIM:(k + 1) * HEAD_DIM]
            if hh < n_rot:
                t = (t * cos + pltpu.roll(t, HEAD_DIM - ROPE_PAIRS, 1) * sa
                     + pltpu.roll(t, ROPE_PAIRS, 1) * sb)
            if hh < N_Q_HEADS:
                t = t * ATTN_SCALE
            o_ref[:, hh * HEAD_DIM:(hh + 1) * HEAD_DIM] = t.astype(BF16)


def _l0_in_proj(ctx2, x2, g, mods, w_in, rope, tile_to_mod, rope_blk):
    n_ctx_rows, d = ctx2.shape
    assert n_ctx_rows == TM
    r = n_ctx_rows + x2.shape[0]
    n = w_in.shape[1]
    assert n % TN_CHUNK == 0
    cos, sa, sb = rope
    rope_spec = pl.BlockSpec((TM, HEAD_DIM), lambda i: (rope_blk(i), 0))
    return pl.pallas_call(
        _l0_in_kernel,
        out_shape=jax.ShapeDtypeStruct((r, n), BF16),
        grid=(r // TM,),
        in_specs=[_resident((TM, d)),
                  pl.BlockSpec((TM, d), lambda i: (jnp.maximum(i - 1, 0), 0)),
                  pl.BlockSpec((1, d), lambda i: (0, 0)),
                  _mod_spec1(0, d, tile_to_mod),
                  _mod_spec1(1, d, tile_to_mod),
                  _resident((d, n)),
                  rope_spec, rope_spec, rope_spec],
        out_specs=pl.BlockSpec((TM, n), lambda i: (i, 0)),
        compiler_params=pltpu.CompilerParams(
            dimension_semantics=("arbitrary",),
            vmem_limit_bytes=_vmem_limit(3 * TM * d * 4 + d * n * 2 + 2 * TM * n * 2
                                         + 2 * TM * d * 4 + 4 * TM * TN_CHUNK * 4)),
        name="l0_in_proj",
    )(ctx2, x2, g.reshape(1, d), mods, mods, w_in, cos, sa, sb)


def _conv_taps(x, xp, xn, first, last, w, b, offsets):
    xp = jnp.where(first, 0.0, xp)
    xn = jnp.where(last, 0.0, xn)
    tr, cw = x.shape
    row8 = lax.broadcasted_iota(jnp.int32, (8, cw), 0)
    acc = jnp.broadcast_to(b, (tr, cw))
    for k, off in enumerate(offsets):
        wk = w[k:k + 1, :]
        if off == 0:
            y = x
        elif off < 0:
            s = -off
            r = pltpu.roll(x, s, 0)
            rp = pltpu.roll(xp, s, 0)[0:8]
            head = jnp.where(row8 < s, rp, r[0:8])
            y = jnp.concatenate([head, r[8:]], axis=0)
        else:
            r = pltpu.roll(x, tr - off, 0)
            rn = pltpu.roll(xn, HALO - off, 0)[HALO - 8:HALO]
            tail = jnp.where(row8 >= 8 - off, rn, r[tr - 8:])
            y = jnp.concatenate([r[:tr - 8], tail], axis=0)
        acc = acc + wk * y
    return acc


def _attn_kernel(sink_ref, q_ref, kc_ref, vc_ref, kp_ref, ko_ref, kn_ref, vp_ref, vo_ref, vn_ref,
                 *rest, tiles_per_seg, n_conv_tiles):
    n_half = n_conv_tiles
    xin = rest[0:n_half]
    bgate = rest[n_half:2 * n_half]
    cgate = rest[2 * n_half:3 * n_half]
    xin_p, cg_p, xin_n, cg_n = (rest[(3 + k) * n_half:(4 + k) * n_half] for k in range(4))
    cw_ref, cb_ref, o_ref, conv_ref = rest[7 * n_half:]
    t = pl.program_id(1)
    tt = t - 1

    first = (t == 0) | (tt == 0)
    last = (t == 0) | (tt == tiles_per_seg - 1)
    for hh in range(n_half):
        cs = slice(hh * CONV_CW, (hh + 1) * CONV_CW)
        x = xin[hh][...].astype(F32) * cgate[hh][...].astype(F32)
        xp = xin_p[hh][...].astype(F32) * cg_p[hh][...].astype(F32)
        xn = xin_n[hh][...].astype(F32) * cg_n[hh][...].astype(F32)
        acc = _conv_taps(x, xp, xn, first, last, cw_ref[:, cs], cb_ref[:, cs], SHORT_CONV_OFFSETS)
        conv_ref[:, cs] = (acc * bgate[hh][...].astype(F32)).astype(conv_ref.dtype)

    tr = q_ref.shape[0]
    n_ctx = kc_ref.shape[0]
    halo = kp_ref.shape[0]
    n_win = tr + 2 * halo
    qi = lax.broadcasted_iota(jnp.int32, (tr, n_win), 0)
    c = lax.broadcasted_iota(jnp.int32, (tr, n_win), 1)
    ok = (c >= qi) & (c <= qi + 2 * WINDOW)
    ok = ok & ((c >= halo) | (tt > 0)) & ((c < halo + tr) | (tt < tiles_per_seg - 1)) & (t > 0)
    bias = jnp.concatenate([jnp.zeros((tr, n_ctx), F32), jnp.where(ok, 0.0, NEG_INF)], axis=1)
    for hk in range(N_KV_HEADS):
        cs = slice(hk * HEAD_DIM, (hk + 1) * HEAD_DIM)
        k_all = jnp.concatenate([kc_ref[:, cs], kp_ref[:, cs], ko_ref[:, cs], kn_ref[:, cs]], axis=0)
        v_all = jnp.concatenate([vc_ref[:, cs], vp_ref[:, cs], vo_ref[:, cs], vn_ref[:, cs]], axis=0)
        for g in range(Q_PER_KV):
            hq = hk * Q_PER_KV + g
            qs = slice(hq * HEAD_DIM, (hq + 1) * HEAD_DIM)
            s = lax.dot_general(q_ref[:, qs], k_all, (((1,), (1,)), ((), ())),
                                preferred_element_type=F32) + bias
            sink = sink_ref[hq]
            m = jnp.maximum(jnp.max(s, axis=-1, keepdims=True), sink)
            p = jnp.exp(s - m)
            denom = jnp.sum(p, axis=-1, keepdims=True) + jnp.exp(sink - m)
            o = jnp.dot(p.astype(BF16), v_all, preferred_element_type=F32)
            o_ref[:, qs] = (o / denom).astype(o_ref.dtype)


def _attention(z, sinks, conv_w, conv_b, n_batch, n_ctx_tiles, tiles_per_seg):
    r = z.shape[0]
    q_dim = N_Q_HEADS * HEAD_DIM
    kv_dim = N_KV_HEADS * HEAD_DIM
    conv_dim = conv_w.shape[1]
    k_col = q_dim // kv_dim
    v_col = k_col + 1
    halo = WINDOW
    per = TR // halo
    n_halo_blk = r // halo
    n_cv = conv_dim // CONV_CW
    x_col = (q_dim + 2 * kv_dim) // CONV_CW
    bg_col, cg_col = x_col + n_cv, x_col + 2 * n_cv
    nh = TR // HALO
    n_conv_halo = r // HALO

    def qblk(b, t):
        return jnp.where(t == 0, b, n_ctx_tiles + b * tiles_per_seg + t - 1)

    def own(col):
        return pl.BlockSpec((TR, kv_dim), lambda b, t: (qblk(b, t), col))

    def ctx(col):
        return pl.BlockSpec((TR, kv_dim), lambda b, t: (b, col))

    def prev(col):
        return pl.BlockSpec((halo, kv_dim), lambda b, t: (jnp.maximum(qblk(b, t) * per - 1, 0), col))

    def nxt(col):
        return pl.BlockSpec((halo, kv_dim),
                            lambda b, t: (jnp.minimum((qblk(b, t) + 1) * per, n_halo_blk - 1), col))

    def cv_main(col0):
        return [pl.BlockSpec((TR, CONV_CW), lambda b, t, c=col0 + k: (qblk(b, t), c)) for k in range(n_cv)]

    def cv_prev(col0):
        return [pl.BlockSpec((HALO, CONV_CW), lambda b, t, c=col0 + k: (jnp.maximum(qblk(b, t) * nh - 1, 0), c))
                for k in range(n_cv)]

    def cv_next(col0):
        return [pl.BlockSpec((HALO, CONV_CW),
                             lambda b, t, c=col0 + k: (jnp.minimum((qblk(b, t) + 1) * nh, n_conv_halo - 1), c))
                for k in range(n_cv)]

    conv_specs = (cv_main(x_col) + cv_main(bg_col) + cv_main(cg_col)
                  + cv_prev(x_col) + cv_prev(cg_col) + cv_next(x_col) + cv_next(cg_col))
    return pl.pallas_call(
        functools.partial(_attn_kernel, tiles_per_seg=tiles_per_seg, n_conv_tiles=n_cv),
        out_shape=(jax.ShapeDtypeStruct((r, q_dim), BF16), jax.ShapeDtypeStruct((r, conv_dim), BF16)),
        grid=(n_batch, 1 + tiles_per_seg),
        in_specs=[pl.BlockSpec(memory_space=pltpu.SMEM),
                  pl.BlockSpec((TR, q_dim), lambda b, t: (qblk(b, t), 0)),
                  ctx(k_col), ctx(v_col),
                  prev(k_col), own(k_col), nxt(k_col),
                  prev(v_col), own(v_col), nxt(v_col)] + conv_specs + [
                      pl.BlockSpec((len(SHORT_CONV_OFFSETS), conv_dim), lambda b, t: (0, 0)),
                      pl.BlockSpec((1, conv_dim), lambda b, t: (0, 0))],
        out_specs=(pl.BlockSpec((TR, q_dim), lambda b, t: (qblk(b, t), 0)),
                   pl.BlockSpec((TR, conv_dim), lambda b, t: (qblk(b, t), 0))),
        compiler_params=pltpu.CompilerParams(dimension_semantics=("arbitrary", "arbitrary")),
        name="attention",
    )(sinks, *([z] * (9 + 7 * n_cv)), conv_w, conv_b.reshape(1, conv_dim))


def _l0_out_kernel(ctx_ref, x_ref, a1_ref, a2_ref, w_ref, gate_ref, oc_ref, ol_ref):
    i = pl.program_id(0)
    lhs = jnp.concatenate([a1_ref[...], a2_ref[...]], axis=1)
    d = ol_ref.shape[1]
    for c0 in range(0, d, TN_CHUNK):
        cs = slice(c0, c0 + TN_CHUNK)
        y = jnp.dot(lhs, w_ref[:, cs], preferred_element_type=F32)
        h = jnp.where(i == 0, ctx_ref[:, cs], x_ref[:, cs])
        res = h + gate_ref[:, cs] * y
        oc_ref[:, cs] = res
        ol_ref[:, cs] = res


def _l0_out_proj(attn, conv, w_out, ctx2, x2, mods, tile_to_mod):
    n_ctx_rows, d = ctx2.shape
    n_lat = x2.shape[0]
    r = n_ctx_rows + n_lat
    k1 = attn.shape[1]
    k2 = conv.shape[1]
    assert w_out.shape[0] == k1 + k2 and d % TN_CHUNK == 0
    return pl.pallas_call(
        _l0_out_kernel,
        out_shape=(jax.ShapeDtypeStruct((2, TM, d), F32), jax.ShapeDtypeStruct((n_lat, d), F32)),
        grid=(r // TM,),
        in_specs=[_resident((TM, d)),
                  pl.BlockSpec((TM, d), lambda i: (jnp.maximum(i - 1, 0), 0)),
                  pl.BlockSpec((TM, k1), lambda i: (i, 0)),
                  pl.BlockSpec((TM, k2), lambda i: (i, 0)),
                  _resident((k1 + k2, d)),
                  _mod_spec1(2, d, tile_to_mod)],
        out_specs=(pl.BlockSpec((None, TM, d), lambda i: (jnp.minimum(i, 1), 0, 0)),
                   pl.BlockSpec((TM, d), lambda i: (jnp.maximum(i - 1, 0), 0))),
        compiler_params=pltpu.CompilerParams(
            dimension_semantics=("arbitrary",),
            vmem_limit_bytes=_vmem_limit(3 * TM * d * 4 + 3 * TM * (k1 + k2) * 2 + (k1 + k2) * d * 2
                                         + 4 * TM * d * 4 + 4 * TM * TN_CHUNK * 4)),
        name="l0_out_proj",
    )(ctx2, x2, attn, conv, w_out, mods)


def _ffn_kernel(*refs, cast_blocks):
    n_cast = len(cast_blocks)
    h_ref, g_ref, sh_ref, sc_ref, gate_ref, wg_ref, wu_ref, wd_ref = refs[:8]
    src = refs[8:8 + n_cast]
    o_ref = refs[8 + n_cast]
    dst = refs[9 + n_cast:9 + 2 * n_cast]
    u_scr = refs[9 + 2 * n_cast]
    i = pl.program_id(0)
    j = pl.program_id(1)
    nj = pl.num_programs(1)
    cb = CAST_BLOCK

    if n_cast:
        cin, cout, sem_in, sem_out = refs[10 + 2 * n_cast:]
        n_blocks = sum(ne * nbr * nbc for ne, nbr, nbc in cast_blocks)
        s = i * nj + j
        slot = s % 2

        def for_block(b, fn):
            off = 0
            for k, (ne, nbr, nbc) in enumerate(cast_blocks):
                nk = ne * nbr * nbc

                @pl.when((b >= off) & (b < off + nk))
                def _(k=k, off=off, nbr=nbr, nbc=nbc):
                    q = b - off
                    rc = q % (nbr * nbc)
                    fn(k, q // (nbr * nbc), rc // nbc, rc % nbc)
                off += nk

        def block_of(ref, e, r, c):
            return ref.at[e, pl.ds(r * cb, cb), pl.ds(c * cb, cb)]

        def start_in(b, to_slot):
            for_block(b, lambda k, e, r, c: pltpu.make_async_copy(
                block_of(src[k], e, r, c), cin.at[to_slot], sem_in.at[to_slot]).start(priority=1))

        @pl.when(s == 0)
        def _():
            start_in(0, 0)

        @pl.when(s < n_blocks)
        def _():
            pltpu.make_async_copy(block_of(src[0], 0, 0, 0), cin.at[slot], sem_in.at[slot]).wait()

        @pl.when(s + 1 < n_blocks)
        def _():
            start_in(s + 1, 1 - slot)

        @pl.when((s >= 2) & (s - 2 < n_blocks))
        def _():
            pltpu.make_async_copy(cout.at[slot], block_of(dst[0], 0, 0, 0), sem_out.at[slot]).wait()

    @pl.when(j == 0)
    def _():
        u_scr[...] = _norm_mod(h_ref[...], g_ref[...], sh_ref[...], sc_ref[...]).astype(BF16)

    if n_cast:
        cout[slot] = cin[slot].astype(BF16)
    u = u_scr[...]
    gt = jnp.dot(u, wg_ref[...], preferred_element_type=F32)
    up = jnp.dot(u, wu_ref[...], preferred_element_type=F32)
    act = (_silu(gt) * up).astype(BF16)
    y = jnp.dot(act, wd_ref[...], preferred_element_type=F32)

    @pl.when(j == 0)
    def _():
        o_ref[...] = y

    @pl.when(j != 0)
    def _():
        o_ref[...] += y

    @pl.when(j == nj - 1)
    def _():
        o_ref[...] = h_ref[...] + gate_ref[...] * o_ref[...]

    if n_cast:
        @pl.when(s < n_blocks)
        def _():
            for_block(s, lambda k, e, r, c: pltpu.make_async_copy(
                cout.at[slot], block_of(dst[k], e, r, c), sem_out.at[slot]).start(priority=1))


def _ffn(h, h_spec, rows, tm, g, mods, w_gate, w_up, w_down, tile_to_mod, riders=()):
    d = h.shape[-1]
    f = w_gate.shape[1]
    tf = TF_FFN
    grid = (rows // tm, f // tf)
    cast_blocks = tuple((w.shape[0], w.shape[1] // CAST_BLOCK, w.shape[2] // CAST_BLOCK) for w in riders)
    for w in riders:
        assert w.shape[1] % CAST_BLOCK == 0 and w.shape[2] % CAST_BLOCK == 0
    n_blocks = sum(ne * nbr * nbc for ne, nbr, nbc in cast_blocks)
    assert not riders or grid[0] * grid[1] >= n_blocks + 2
    any_spec = pl.BlockSpec(memory_space=pl.ANY)
    scratch = [pltpu.VMEM((tm, d), BF16)]
    if riders:
        scratch += [pltpu.VMEM((2, CAST_BLOCK, CAST_BLOCK), F32), pltpu.VMEM((2, CAST_BLOCK, CAST_BLOCK), BF16),
                    pltpu.SemaphoreType.DMA((2,)), pltpu.SemaphoreType.DMA((2,))]
    return pl.pallas_call(
        functools.partial(_ffn_kernel, cast_blocks=cast_blocks),
        out_shape=(jax.ShapeDtypeStruct((rows, d), F32),
                   *[jax.ShapeDtypeStruct(w.shape, BF16) for w in riders]),
        grid=grid,
        in_specs=[h_spec,
                  pl.BlockSpec((1, d), lambda i, j: (0, 0)),
                  _mod_spec(3, d, tile_to_mod),
                  _mod_spec(4, d, tile_to_mod),
                  _mod_spec(5, d, tile_to_mod),
                  pl.BlockSpec((d, tf), lambda i, j: (0, j)),
                  pl.BlockSpec((d, tf), lambda i, j: (0, j)),
                  pl.BlockSpec((tf, d), lambda i, j: (j, 0))] + [any_spec] * len(riders),
        out_specs=(pl.BlockSpec((tm, d), lambda i, j: (i, 0)), *[any_spec] * len(riders)),
        scratch_shapes=scratch,
        compiler_params=pltpu.CompilerParams(
            dimension_semantics=("arbitrary", "arbitrary"),
            vmem_limit_bytes=_vmem_limit(4 * tm * d * 4 + 6 * d * tf * 2 + tm * d * 2
                                         + 3 * tm * tf * 4 + tm * tf * 2
                                         + (12 * CAST_BLOCK * CAST_BLOCK if riders else 0))),
        name="ffn",
    )(h, g.reshape(1, d), mods, mods, mods, w_gate, w_up, w_down, *riders)


def _l1_in_kernel(hc_ref, hl_ref, g_ref, sh_ref, sc_ref, w_ref, gg_ref, xr_ref):
    h = jnp.where(pl.program_id(0) == 0, hc_ref[...], hl_ref[...])
    u = _norm_mod(h, g_ref[...], sh_ref[...], sc_ref[...]).astype(BF16)
    d_rnn = gg_ref.shape[1]
    for c0 in range(0, 2 * d_rnn, TN_CHUNK):
        z = jnp.dot(u, w_ref[:, c0:c0 + TN_CHUNK], preferred_element_type=F32)
        if c0 < d_rnn:
            gg_ref[:, c0:c0 + TN_CHUNK] = jax.nn.gelu(z, approximate=True).astype(gg_ref.dtype)
        else:
            xr_ref[:, c0 - d_rnn:c0 - d_rnn + TN_CHUNK] = z


def _l1_in_proj(h_ctx, h_lat, g, mods, w_in, d_rnn, tile_to_mod):
    n_ctx_rows, d = h_ctx.shape
    assert n_ctx_rows == TM
    r = n_ctx_rows + h_lat.shape[0]
    assert w_in.shape[1] == 2 * d_rnn and d_rnn % TN_CHUNK == 0
    return pl.pallas_call(
        _l1_in_kernel,
        out_shape=(jax.ShapeDtypeStruct((r, d_rnn), BF16), jax.ShapeDtypeStruct((r, d_rnn), F32)),
        grid=(r // TM,),
        in_specs=[_resident((TM, d)),
                  pl.BlockSpec((TM, d), lambda i: (jnp.maximum(i - 1, 0), 0)),
                  pl.BlockSpec((1, d), lambda i: (0, 0)),
                  _mod_spec1(0, d, tile_to_mod),
                  _mod_spec1(1, d, tile_to_mod),
                  _resident((d, 2 * d_rnn))],
        out_specs=(pl.BlockSpec((TM, d_rnn), lambda i: (i, 0)),
                   pl.BlockSpec((TM, d_rnn), lambda i: (i, 0))),
        compiler_params=pltpu.CompilerParams(
            dimension_semantics=("arbitrary",),
            vmem_limit_bytes=_vmem_limit(2 * TM * d * 4 + d * 2 * d_rnn * 2 + 2 * TM * d_rnn * 6
                                         + 2 * TM * d * 4 + 4 * TM * TN_CHUNK * 4)),
        name="l1_in_proj",
    )(h_ctx, h_lat, g.reshape(1, d), mods, mods, w_in)


def _rglru_kernel(xf_ref, xfp_ref, xfn_ref, xb_ref, xbp_ref, xbn_ref, cw_ref, cb_ref,
                  gaw_ref, gxw_ref, gab_ref, gxb_ref, lam_ref, hf_ref, hb_ref,
                  xc_scr, a_scr, b_scr, h_scr, *, tiles_per_seg):
    c = pl.program_id(1)
    t_len, d_rnn = xf_ref.shape
    hd = d_rnn // N_RNN_HEADS

    @pl.when(c == 0)
    def _():
        h_scr[...] = jnp.zeros_like(h_scr)

    is_ctx = c == 0
    chunk = (c - 1, tiles_per_seg - c)
    halos = ((xfp_ref, xfn_ref), (xbp_ref, xbn_ref))

    for z, x_ref in ((0, xf_ref), (1, xb_ref)):
        first = is_ctx | (chunk[z] == 0)
        last = is_ctx | (chunk[z] == tiles_per_seg - 1)
        xc_scr[...] = _conv_taps(x_ref[...], halos[z][0][...], halos[z][1][...], first, last,
                                 cw_ref[...], cb_ref[...], RG_CONV_OFFSETS)
        x_ref = xc_scr
        neg_lam = -lam_ref[z]
        sp = jnp.maximum(neg_lam, 0.0) + jnp.log1p(jnp.exp(-jnp.abs(neg_lam)))
        half_k = sp * (-0.5 * RG_C * LOG2_E)
        for hh in range(N_RNN_HEADS):
            sl = slice(hh * hd, (hh + 1) * hd)
            xh = x_ref[:, sl]
            xh16 = xh.astype(BF16)
            ta = jnp.tanh(jnp.dot(xh16, gaw_ref[z, hh], preferred_element_type=F32) + gab_ref[z][:, sl])
            ti = jnp.tanh(jnp.dot(xh16, gxw_ref[z, hh], preferred_element_type=F32) + gxb_ref[z][:, sl])
            hk = half_k[:, sl]
            a = jnp.exp2(ta * hk + hk)
            om = 1.0 - a * a
            root = jnp.where(om > 0.0, om * lax.rsqrt(om), 0.0)
            rx = (0.5 * root) * xh
            a_scr[z, :, sl] = a
            b_scr[z, :, sl] = rx * ti + rx

    row8 = lax.broadcasted_iota(jnp.int32, (8, d_rnn), 0)
    n_grp = t_len // 8

    def fwd_group(r0, h):
        a = a_scr[0, pl.ds(r0, 8), :]
        b = b_scr[0, pl.ds(r0, 8), :]
        for s in (1, 2, 4):
            a_sh = jnp.where(row8 >= s, pltpu.roll(a, s, 0), 1.0)
            b_sh = jnp.where(row8 >= s, pltpu.roll(b, s, 0), 0.0)
            b = a * b_sh + b
            a = a * a_sh
        out = a * h + b
        return out, jnp.broadcast_to(out[7:8, :], (8, d_rnn))

    def bwd_group(r0, h):
        a = a_scr[1, pl.ds(r0, 8), :]
        b = b_scr[1, pl.ds(r0, 8), :]
        for s in (1, 2, 4):
            a_sh = jnp.where(row8 < 8 - s, pltpu.roll(a, 8 - s, 0), 1.0)
            b_sh = jnp.where(row8 < 8 - s, pltpu.roll(b, 8 - s, 0), 0.0)
            b = a * b_sh + b
            a = a * a_sh
        out = a * h + b
        return out, jnp.broadcast_to(out[0:1, :], (8, d_rnn))

    def fwd_body(g, h):
        r0 = pl.multiple_of(g * 16, 16)
        lo, h = fwd_group(r0, h)
        hi, h = fwd_group(r0 + 8, h)
        hf_ref[pl.ds(r0, 16), :] = jnp.concatenate([lo, hi], axis=0).astype(hf_ref.dtype)
        return h

    def bwd_body(k, h):
        r0 = pl.multiple_of((n_grp // 2 - 1 - k) * 16, 16)
        hi, h = bwd_group(r0 + 8, h)
        lo, h = bwd_group(r0, h)
        hb_ref[pl.ds(r0, 16), :] = jnp.concatenate([lo, hi], axis=0).astype(hb_ref.dtype)
        return h

    h_scr[0] = lax.fori_loop(0, n_grp // 2, fwd_body, h_scr[0])
    h_scr[1] = lax.fori_loop(0, n_grp // 2, bwd_body, h_scr[1])


def _rglru(xr, conv_w, conv_b, ga_w, gx_w, ga_b, gx_b, lam, n_batch, n_ctx_tiles, tiles_per_seg):
    r, d_rnn = xr.shape
    n_lat = n_batch * tiles_per_seg * TR
    hd = d_rnn // N_RNN_HEADS
    nt = tiles_per_seg
    nh = TR // HALO
    n_halo = r // HALO
    n_taps = conv_w.shape[0]

    def fblk(b, c):
        return jnp.where(c == 0, b, n_ctx_tiles + b * nt + c - 1)

    def bblk(b, c):
        return jnp.where(c == 0, b, n_ctx_tiles + b * nt + nt - c)

    def chunk_specs(blk):
        return [pl.BlockSpec((TR, d_rnn), lambda b, c: (blk(b, c), 0)),
                pl.BlockSpec((HALO, d_rnn), lambda b, c: (jnp.maximum(blk(b, c) * nh - 1, 0), 0)),
                pl.BlockSpec((HALO, d_rnn), lambda b, c: (jnp.minimum((blk(b, c) + 1) * nh, n_halo - 1), 0))]

    def full(shape):
        return pl.BlockSpec(shape, lambda b, c: (0,) * len(shape))

    return pl.pallas_call(
        functools.partial(_rglru_kernel, tiles_per_seg=nt),
        out_shape=(jax.ShapeDtypeStruct((n_lat, d_rnn), BF16), jax.ShapeDtypeStruct((n_lat, d_rnn), BF16)),
        grid=(n_batch, 1 + nt),
        in_specs=chunk_specs(fblk) + chunk_specs(bblk) + [
            full((n_taps, d_rnn)), full((1, d_rnn)),
            full((2, N_RNN_HEADS, hd, hd)), full((2, N_RNN_HEADS, hd, hd)),
            full((2, 1, d_rnn)), full((2, 1, d_rnn)), full((2, 1, d_rnn))],
        out_specs=(pl.BlockSpec((TR, d_rnn), lambda b, c: (b * nt + jnp.maximum(c - 1, 0), 0)),
                   pl.BlockSpec((TR, d_rnn), lambda b, c: (b * nt + jnp.where(c == 0, nt - 1, nt - c), 0))),
        scratch_shapes=[pltpu.VMEM((TR, d_rnn), F32),
                        pltpu.VMEM((2, TR, d_rnn), F32), pltpu.VMEM((2, TR, d_rnn), F32),
                        pltpu.VMEM((2, 8, d_rnn), F32)],
        compiler_params=pltpu.CompilerParams(
            dimension_semantics=("arbitrary", "arbitrary"),
            vmem_limit_bytes=_vmem_limit(14 * TR * d_rnn * 4 + 8 * N_RNN_HEADS * hd * hd * 2)),
        name="rglru",
    )(xr, xr, xr, xr, xr, xr, conv_w, conv_b.reshape(1, d_rnn), ga_w, gx_w,
      ga_b.reshape(2, 1, d_rnn), gx_b.reshape(2, 1, d_rnn), lam.reshape(2, 1, d_rnn))


def _l1_out_kernel(hf_ref, hb_ref, gg_ref, w_ref, h_ref, gate_ref, o_ref):
    rec = hf_ref[...].astype(F32) + hb_ref[...].astype(F32)
    lhs = (rec * gg_ref[...].astype(F32)).astype(BF16)
    d = o_ref.shape[1]
    for c0 in range(0, d, TN_CHUNK):
        cs = slice(c0, c0 + TN_CHUNK)
        y = jnp.dot(lhs, w_ref[:, cs], preferred_element_type=F32)
        o_ref[:, cs] = h_ref[:, cs] + gate_ref[:, cs] * y


def _l1_out_proj(hf, hb, gg, w_out, h, mods, n_ctx_mtiles, lat_tile_to_mod):
    n_lat, d_rnn = hf.shape
    d = h.shape[1]
    assert d % TN_CHUNK == 0
    return pl.pallas_call(
        _l1_out_kernel,
        out_shape=jax.ShapeDtypeStruct((n_lat, d), F32),
        grid=(n_lat // TM,),
        in_specs=[pl.BlockSpec((TM, d_rnn), lambda i: (i, 0)),
                  pl.BlockSpec((TM, d_rnn), lambda i: (i, 0)),
                  pl.BlockSpec((TM, d_rnn), lambda i: (i + n_ctx_mtiles, 0)),
                  _resident((d_rnn, d)),
                  pl.BlockSpec((TM, d), lambda i: (i, 0)),
                  _mod_spec1(2, d, lat_tile_to_mod)],
        out_specs=pl.BlockSpec((TM, d), lambda i: (i, 0)),
        compiler_params=pltpu.CompilerParams(
            dimension_semantics=("arbitrary",),
            vmem_limit_bytes=_vmem_limit(6 * TM * d_rnn * 2 + d_rnn * d * 2 + 4 * TM * d * 4
                                         + 2 * TM * d_rnn * 4 + 4 * TM * TN_CHUNK * 4)),
        name="l1_out_proj",
    )(hf, hb, gg, w_out, h, mods)


def _router_kernel(h_ref, g_ref, sh_ref, sc_ref, rw_ref, rb_ref, u_ref, route_ref):
    u = _norm_mod(h_ref[...], g_ref[...], sh_ref[...], sc_ref[...])
    u_ref[...] = u
    logits = jnp.dot(u.astype(BF16), rw_ref[...], preferred_element_type=F32) + rb_ref[...]
    lane = lax.broadcasted_iota(jnp.int32, logits.shape, 1)
    m1 = jnp.max(logits, axis=-1, keepdims=True)
    i1 = jnp.min(jnp.where(logits == m1, lane, LANES), axis=-1, keepdims=True)
    rest = jnp.where(lane == i1, -jnp.inf, logits)
    m2 = jnp.max(rest, axis=-1, keepdims=True)
    i2 = jnp.min(jnp.where(rest == m2, lane, LANES), axis=-1, keepdims=True)
    e2 = jnp.exp(m2 - m1)
    w1 = 1.0 / (1.0 + e2)
    w2 = e2 * w1
    route = jnp.where(lane == 0, i1.astype(F32),
                      jnp.where(lane == 1, i2.astype(F32),
                                jnp.where(lane == 2, w1, jnp.where(lane == 3, w2, 0.0))))
    route_ref[...] = route


def _router(h, g, mods, rw_pad, rb_pad, lat_tile_to_mod):
    n, d = h.shape

    def mspec(chunk):
        return pl.BlockSpec((None, 1, d), lambda i: (lat_tile_to_mod(i) * N_MOD + chunk, 0, 0))

    return pl.pallas_call(
        _router_kernel,
        out_shape=(jax.ShapeDtypeStruct((n, d), F32), jax.ShapeDtypeStruct((n, LANES), F32)),
        grid=(n // TM,),
        in_specs=[pl.BlockSpec((TM, d), lambda i: (i, 0)),
                  pl.BlockSpec((1, d), lambda i: (0, 0)),
                  mspec(3), mspec(4),
                  pl.BlockSpec((d, LANES), lambda i: (0, 0)),
                  pl.BlockSpec((1, LANES), lambda i: (0, 0))],
        out_specs=(pl.BlockSpec((TM, d), lambda i: (i, 0)),
                   pl.BlockSpec((TM, LANES), lambda i: (i, 0))),
        compiler_params=pltpu.CompilerParams(
            dimension_semantics=("arbitrary",),
            vmem_limit_bytes=_vmem_limit(6 * TM * d * 4)),
        name="router",
    )(h, g.reshape(1, d), mods, mods, rw_pad, rb_pad)


def _row_copy(src, src_row, dst, dst_row, sem):
    return pltpu.make_async_copy(src.at[pl.ds(src_row, 1)], dst.at[pl.ds(dst_row, 1)], sem)


def _moe_kernel(te_ref, tv_ref, tc_ref, tok_ref, tok_next_ref, u_hbm, wg_ref, wu_ref, wd_ref, o_ref,
                xg_scr, x_scr, sems, *, gather_steps):
    del te_ref
    i = pl.program_id(0)
    j = pl.program_id(1)
    n_tiles = pl.num_programs(0)
    rows = x_scr.shape[0]
    per_step = rows // gather_steps
    slot = i % 2
    valid = tv_ref[i] == 1
    next_valid = (i + 1 < n_tiles) & (tv_ref[jnp.minimum(i + 1, n_tiles - 1)] == 1)

    @pl.when(valid & (j == 0))
    def _():
        @pl.when(i == 0)
        def _():
            def start(r, carry):
                _row_copy(u_hbm, tok_ref[r], xg_scr.at[0], r, sems.at[0]).start()
                return carry
            lax.fori_loop(0, rows, start, 0)

        pltpu.make_async_copy(u_hbm.at[pl.ds(0, rows)], xg_scr.at[slot], sems.at[slot]).wait()
        x_scr[...] = xg_scr[slot].astype(BF16)

    def step(with_gather, m):
        if with_gather:
            r0 = j * per_step
            for r in range(per_step):
                _row_copy(u_hbm, tok_next_ref[r0 + r], xg_scr.at[1 - slot], r0 + r, sems.at[1 - slot]).start()
        x = x_scr[0:m, :]
        gt = jnp.dot(x, wg_ref[...], preferred_element_type=F32)
        up = jnp.dot(x, wu_ref[...], preferred_element_type=F32)
        act = (_silu(gt) * up).astype(BF16)
        y = jnp.dot(act, wd_ref[...], preferred_element_type=F32)

        @pl.when(j == 0)
        def _():
            o_ref[0:m, :] = y
            if m < rows:
                o_ref[m:, :] = jnp.zeros((rows - m, o_ref.shape[1]), F32)

        @pl.when(j != 0)
        def _():
            o_ref[0:m, :] += y

    gather_now = next_valid & (j < gather_steps)
    half = tc_ref[i] <= rows // 2
    for with_gather in (True, False):
        for is_half in (True, False):
            cond = valid & (gather_now if with_gather else jnp.logical_not(gather_now))
            cond = cond & (half if is_half else jnp.logical_not(half))

            @pl.when(cond)
            def _(with_gather=with_gather, is_half=is_half):
                step(with_gather, rows // 2 if is_half else rows)

    @pl.when((tv_ref[i] == 0) & (j == 0))
    def _():
        o_ref[...] = jnp.zeros_like(o_ref)


def _moe(u, slot_tok, tile_e, tile_valid, tile_count, w_gate, w_up, w_down):
    d = u.shape[1]
    cap = slot_tok.shape[0]
    n_tiles = cap // TM
    f = w_gate.shape[2]
    tf = TF_MOE
    nj = f // tf
    gather_steps = min(MOE_GATHER_STEPS, nj)
    assert TM % gather_steps == 0

    def jj(i, j, tv):
        return jnp.where(tv[i] == 1, j, nj - 1)

    return pl.pallas_call(
        functools.partial(_moe_kernel, gather_steps=gather_steps),
        out_shape=jax.ShapeDtypeStruct((cap, d), F32),
        grid_spec=pltpu.PrefetchScalarGridSpec(
            num_scalar_prefetch=3,
            grid=(n_tiles, nj),
            in_specs=[pl.BlockSpec((TM,), lambda i, j, te, tv, tc: (i,), memory_space=pltpu.SMEM),
                      pl.BlockSpec((TM,), lambda i, j, te, tv, tc: (jnp.minimum(i + 1, n_tiles - 1),),
                                   memory_space=pltpu.SMEM),
                      pl.BlockSpec(memory_space=pl.ANY),
                      pl.BlockSpec((None, d, tf), lambda i, j, te, tv, tc: (te[i], 0, jj(i, j, tv))),
                      pl.BlockSpec((None, d, tf), lambda i, j, te, tv, tc: (te[i], 0, jj(i, j, tv))),
                      pl.BlockSpec((None, tf, d), lambda i, j, te, tv, tc: (te[i], jj(i, j, tv), 0))],
            out_specs=pl.BlockSpec((TM, d), lambda i, j, te, tv, tc: (i, 0)),
            scratch_shapes=[pltpu.VMEM((2, TM, d), F32), pltpu.VMEM((TM, d), BF16),
                            pltpu.SemaphoreType.DMA((2,))]),
        compiler_params=pltpu.CompilerParams(
            dimension_semantics=("arbitrary", "arbitrary"),
            vmem_limit_bytes=_vmem_limit(2 * TM * d * 4 + 2 * TM * d * 4 + 6 * d * tf * 2 + TM * d * 2
                                         + 3 * TM * tf * 4 + TM * d * 4)),
        name="moe_experts",
    )(tile_e, tile_valid, tile_count, slot_tok, slot_tok, u, w_gate, w_up, w_down)


def _combine_kernel(dest_ref, route_ref, h_ref, gate_ref, fg_ref, yb_hbm, o_ref, y_scr, sem, *, rows):
    def start(r, carry):
        for kk in range(TOP_K):
            _row_copy(yb_hbm, dest_ref[TOP_K * r + kk], y_scr.at[kk], r, sem).start()
        return carry

    lax.fori_loop(0, rows, start, 0)
    for kk in range(TOP_K):
        pltpu.make_async_copy(yb_hbm.at[pl.ds(0, rows)], y_scr.at[kk], sem).wait()

    route = route_ref[...]
    moe = route[:, 2:3] * y_scr[0] + route[:, 3:4] * y_scr[1]
    hl = h_ref[...] + gate_ref[...] * moe
    ms = jnp.mean(hl * hl, axis=-1, keepdims=True)
    o_ref[...] = (hl * lax.rsqrt(ms + NORM_EPS)) * fg_ref[...]


def _combine(yb, dest, route, h, mods, final_g, lat_tile_to_mod):
    n, d = h.shape
    return pl.pallas_call(
        functools.partial(_combine_kernel, rows=TM),
        out_shape=jax.ShapeDtypeStruct((n, d), F32),
        grid=(n // TM,),
        in_specs=[pl.BlockSpec((TOP_K * TM,), lambda i: (i,), memory_space=pltpu.SMEM),
                  pl.BlockSpec((TM, LANES), lambda i: (i, 0)),
                  pl.BlockSpec((TM, d), lambda i: (i, 0)),
                  pl.BlockSpec((None, 1, d), lambda i: (lat_tile_to_mod(i) * N_MOD + 5, 0, 0)),
                  pl.BlockSpec((1, d), lambda i: (0, 0)),
                  pl.BlockSpec(memory_space=pl.ANY)],
        out_specs=pl.BlockSpec((TM, d), lambda i: (i, 0)),
        scratch_shapes=[pltpu.VMEM((TOP_K, TM, d), F32), pltpu.SemaphoreType.DMA],
        compiler_params=pltpu.CompilerParams(
            dimension_semantics=("arbitrary",),
            vmem_limit_bytes=_vmem_limit(TOP_K * TM * d * 4 + 4 * TM * d * 4 + 2 * TM * d * 4)),
        name="moe_combine",
    )(dest, route, h, mods, final_g.reshape(1, d), yb)


def _rope_tables(n_ident, s_len):
    t = jnp.arange(s_len)
    row_id = (t // GRID_W).astype(F32)
    col_id = (t % GRID_W).astype(F32)
    inv_freq = ROPE_BASE ** (-jnp.arange(ROPE_PAIRS, dtype=F32) / ROPE_PAIRS)
    ang_r = row_id[:, None] * inv_freq
    ang_c = col_id[:, None] * inv_freq
    ang = jnp.concatenate([ang_r, ang_r, ang_c, ang_c], axis=-1)
    cos, sin = jnp.cos(ang), jnp.sin(ang)
    first_half = (jnp.arange(HEAD_DIM) % (2 * ROPE_PAIRS)) < ROPE_PAIRS
    sa = jnp.where(first_half, -sin, 0.0)
    sb = jnp.where(first_half, 0.0, sin)
    ones = jnp.ones((n_ident, HEAD_DIM), F32)
    zeros = jnp.zeros((n_ident, HEAD_DIM), F32)
    return (jnp.concatenate([ones, cos]), jnp.concatenate([zeros, sa]), jnp.concatenate([zeros, sb]))


def _routing_plan(route, n_tiles):
    e = route[:, :TOP_K].astype(jnp.int32).reshape(-1)
    onehot = (e[:, None] == jnp.arange(N_EXPERTS, dtype=jnp.int32)[None, :]).astype(jnp.int32)
    csum = jnp.cumsum(onehot, axis=0)
    rank = jnp.sum(csum * onehot, axis=1) - 1
    counts = csum[-1]
    padded = (counts + TM - 1) // TM * TM
    pad_end = jnp.cumsum(padded)
    pad_start = pad_end - padded
    dest = jnp.sum(onehot * pad_start[None, :], axis=1) + rank
    n_valid = pad_end[-1] // TM
    tile_idx = jnp.arange(n_tiles, dtype=jnp.int32)
    tile_valid = (tile_idx < n_valid).astype(jnp.int32)
    tile_row = jnp.minimum(tile_idx, n_valid - 1)
    tile_e = jnp.sum((pad_end[None, :] <= (tile_row * TM)[:, None]).astype(jnp.int32), axis=1)
    tile_e = jnp.minimum(tile_e, N_EXPERTS - 1)
    real_end = jnp.sum((tile_e[:, None] == jnp.arange(N_EXPERTS)[None, :]) * (pad_start + counts)[None, :], axis=1)
    tile_count = jnp.clip(real_end - tile_idx * TM, 0, TM) * tile_valid
    dest = dest.astype(jnp.int32)
    slot_tok = jnp.zeros((n_tiles * TM,), jnp.int32).at[dest].set(
        jnp.arange(e.shape[0], dtype=jnp.int32) // TOP_K)
    return dest, slot_tok, tile_e.astype(jnp.int32), tile_valid, tile_count.astype(jnp.int32)


def _mods(cvecs, w_mod, b_mod):
    d = w_mod.shape[0]
    m = _adaln(cvecs, w_mod, b_mod)[:3]
    return m.reshape(3 * N_MOD, 1, d)


def kernel(x, c, ctx, c_ctx, l0_w_mod, l0_b_mod, l0_norm1_g, l0_w_in, l0_sinks, l0_conv_w, l0_conv_b, l0_w_out, l0_norm2_g, l0_ffn_w_gate, l0_ffn_w_up, l0_ffn_w_down, l1_w_mod, l1_b_mod, l1_norm1_g, l1_w_in, l1_conv_w, l1_conv_b, l1_gate_a_w, l1_gate_a_b, l1_gate_x_w, l1_gate_x_b, l1_lambda, l1_w_out, l1_norm2_g, l1_router_w, l1_router_b, l1_moe_w_gate, l1_moe_w_up, l1_moe_w_down, final_norm_g):
    n_batch, s_len, d = x.shape
    n_ctx = ctx.shape[1]
    assert n_batch == 2 and n_batch * n_ctx == TM and n_ctx == TR
    assert s_len % TM == 0 and s_len % GRID_W == 0 and WINDOW * 2 == TR
    n_ctx_rows = n_batch * n_ctx
    n_lat = n_batch * s_len
    n_ctx_rtiles = n_ctx_rows // TR
    tiles_per_seg = s_len // TR
    mtiles_per_seg = s_len // TM
    d_rnn = l1_w_out.shape[0]

    def tile_to_mod(i):
        return jnp.where(i == 0, n_batch, (i - 1) // mtiles_per_seg)

    def lat_tile_to_mod(i):
        return i // mtiles_per_seg

    def rope_blk(i):
        return jnp.where(i == 0, 0, 1 + (i - 1) % mtiles_per_seg)

    cvecs = jnp.concatenate([c, c_ctx[None, :], jnp.zeros((8 - n_batch - 1, d), F32)], axis=0)
    mods0 = _mods(cvecs, l0_w_mod, l0_b_mod)
    mods1 = _mods(cvecs, l1_w_mod, l1_b_mod)
    ctx2 = ctx.reshape(n_ctx_rows, d)
    x2 = x.reshape(n_lat, d)

    rope = _rope_tables(TM, s_len)
    z = _l0_in_proj(ctx2, x2, l0_norm1_g, mods0, l0_w_in.astype(BF16), rope, tile_to_mod, rope_blk)
    attn, conv = _attention(z, l0_sinks, l0_conv_w, l0_conv_b, n_batch, n_ctx_rtiles, tiles_per_seg)
    hc2, hl = _l0_out_proj(attn, conv, l0_w_out.astype(BF16), ctx2, x2, mods0, tile_to_mod)
    ffn_w = (l0_ffn_w_gate.astype(BF16), l0_ffn_w_up.astype(BF16), l0_ffn_w_down.astype(BF16))
    (hc,) = _ffn(hc2, pl.BlockSpec((None, TM, d), lambda i, j: (0, 0, 0)), n_ctx_rows, TM,
                 l0_norm2_g, mods0, *ffn_w, lambda i: n_batch)
    hl, moe_wg, moe_wu, moe_wd = _ffn(
        hl, pl.BlockSpec((TM, d), lambda i, j: (i, 0)), n_lat, TM, l0_norm2_g, mods0, *ffn_w,
        lat_tile_to_mod, riders=(l1_moe_w_gate, l1_moe_w_up, l1_moe_w_down))

    gg, xr = _l1_in_proj(hc, hl, l1_norm1_g, mods1, l1_w_in.astype(BF16), d_rnn, tile_to_mod)
    hf, hb = _rglru(xr, l1_conv_w, l1_conv_b, (0.5 * l1_gate_a_w).astype(BF16), (0.5 * l1_gate_x_w).astype(BF16),
                    0.5 * l1_gate_a_b, 0.5 * l1_gate_x_b, l1_lambda, n_batch, n_ctx_rtiles, tiles_per_seg)
    h_lat = _l1_out_proj(hf, hb, gg, l1_w_out.astype(BF16), hl, mods1, n_ctx_rows // TM, lat_tile_to_mod)

    rw_pad = jnp.zeros((d, LANES), BF16).at[:, :N_EXPERTS].set(l1_router_w.astype(BF16))
    rb_pad = jnp.full((1, LANES), NEG_INF, F32).at[0, :N_EXPERTS].set(l1_router_b)
    u, route = _router(h_lat, l1_norm2_g, mods1, rw_pad, rb_pad, lat_tile_to_mod)
    n_tiles = (n_lat * TOP_K) // TM + N_EXPERTS
    dest, slot_tok, tile_e, tile_valid, tile_count = _routing_plan(route, n_tiles)
    yb = _moe(u, slot_tok, tile_e, tile_valid, tile_count, moe_wg, moe_wu, moe_wd)
    out = _combine(yb, dest, route, h_lat, mods1, final_norm_g, lat_tile_to_mod)
    return out.reshape(n_batch, s_len, d)
```
